```python
import jax
import jax.numpy as jnp
from jax import lax
import numpy as np

D_MODEL = 4096
BATCH = 1
SEQ = 16384
DEPTH = 1
DEC_BATCH = 2
DEC_SEQ = 8192
PAST_LEN = 128

D_MIX = 2 * D_MODEL
D_SSD = D_MIX // 2
SSD_HEAD_DIM = 64
SSD_HEADS = D_SSD // SSD_HEAD_DIM
SSD_GROUPS = 8
SSD_HEADS_PER_GROUP = SSD_HEADS // SSD_GROUPS
SSD_STATE = 128
SSD_XBC = D_SSD + 2 * SSD_GROUPS * SSD_STATE
D_CONV = 5
D_MLSTM = D_MIX - D_SSD
MLSTM_HEADS = 16
MLSTM_V_DIM = D_MLSTM // MLSTM_HEADS
MLSTM_QK_DIM = MLSTM_V_DIM // 2
D_FF = 4 * D_MODEL
CHUNK = 128
EPS = 1e-5
PROJ_WIDTHS = (D_SSD, SSD_XBC, 2 * SSD_HEADS, MLSTM_HEADS * MLSTM_QK_DIM, MLSTM_HEADS * MLSTM_QK_DIM, D_MLSTM, D_MLSTM, 2 * MLSTM_HEADS, 2 * MLSTM_HEADS)
D_IN_PROJ = sum(PROJ_WIDTHS)

kernel_name = "hymba_ssd_mlstm_bidir_encoder"


def rmsnorm(x, w):
    xf = x.astype(jnp.float32)
    y = xf * lax.rsqrt(jnp.mean(xf * xf, axis=-1, keepdims=True) + EPS)
    return (y * w.astype(jnp.float32)).astype(x.dtype)


def flip(a):
    return jnp.flip(a, axis=1)


def centred_depthwise_conv(u, w, bias):
    L = u.shape[1]
    pad = D_CONV // 2
    up = jnp.pad(u, ((0, 0), (pad, pad), (0, 0)))
    out = w[0] * up[:, 0:L]
    for j in range(1, D_CONV):
        out = out + w[j] * up[:, j:j + L]
    return out + bias


def lower_mask():
    return jnp.tril(jnp.ones((CHUNK, CHUNK), dtype=bool))


def ssd_scan(x, dt, A, Bm, Cm):
    b, L = x.shape[:2]
    nc = L // CHUNK
    G, R, P, N = SSD_GROUPS, SSD_HEADS_PER_GROUP, SSD_HEAD_DIM, SSD_STATE
    xc = x.reshape(b, nc, CHUNK, G, R, P)
    dtc = dt.reshape(b, nc, CHUNK, G, R)
    Bc = Bm.reshape(b, nc, CHUNK, G, N)
    Cc = Cm.reshape(b, nc, CHUNK, G, N)
    acum = jnp.cumsum(jnp.moveaxis(dtc * A, 2, -1), axis=-1)
    seg = acum[..., :, None] - acum[..., None, :]
    decay = jnp.exp(jnp.where(lower_mask(), seg, -jnp.inf))
    xdt = xc * dtc[..., None]
    cb = jnp.einsum('bclgn,bcsgn->bcgls', Cc, Bc)
    y_diag = jnp.einsum('bcgrls,bcsgrp->bclgrp', cb[:, :, :, None] * decay, xdt)
    decay_to_end = jnp.exp(acum[..., -1:] - acum)
    states = jnp.einsum('bclgn,bcgrl,bclgrp->bcgrpn', Bc, decay_to_end, xdt)
    chunk_decay = jnp.exp(acum[..., -1])

    def step(S, inp):
        dec, st = inp
        return dec[..., None, None] * S + st, S

    S0 = jnp.zeros_like(states[:, 0])
    _, S_in = lax.scan(step, S0, (jnp.moveaxis(chunk_decay, 1, 0), jnp.moveaxis(states, 1, 0)))
    S_in = jnp.moveaxis(S_in, 0, 1)
    y_off = jnp.einsum('bclgn,bcgrpn,bcgrl->bclgrp', Cc, S_in, jnp.exp(acum))
    return (y_diag + y_off).reshape(b, L, G, R, P)


def mlstm_chunkwise(q, k, v, log_i, log_f):
    b, L, H, DK = q.shape
    DV = v.shape[-1]
    nc = L // CHUNK
    qc = q.reshape(b, nc, CHUNK, H, DK) * (DK ** -0.5)
    kc = k.reshape(b, nc, CHUNK, H, DK)
    vc = v.reshape(b, nc, CHUNK, H, DV)
    li = jnp.moveaxis(log_i.reshape(b, nc, CHUNK, H), 2, -1)
    lf = jnp.moveaxis(log_f.reshape(b, nc, CHUNK, H), 2, -1)
    bcum = jnp.cumsum(lf, axis=-1)
    b_end = bcum[..., -1]
    g = b_end[..., None] - bcum + li
    m_loc = jnp.max(g, axis=-1)
    w = jnp.exp(g - m_loc[..., None])
    C_loc = jnp.einsum('bchl,bclhk,bclhv->bchkv', w, kc, vc)
    n_loc = jnp.einsum('bchl,bclhk->bchk', w, kc)

    def step(carry, inp):
        C, n, m = carry
        be, ml, Cl, nl = inp
        m_new = jnp.maximum(be + m, ml)
        a_prev = jnp.exp(be + m - m_new)
        a_loc = jnp.exp(ml - m_new)
        C_new = a_prev[..., None, None] * C + a_loc[..., None, None] * Cl
        n_new = a_prev[..., None] * n + a_loc[..., None] * nl
        return (C_new, n_new, m_new), (C, n, m)

    init = (jnp.zeros((b, H, DK, DV), jnp.float32), jnp.zeros((b, H, DK), jnp.float32), jnp.zeros((b, H), jnp.float32))
    xs = (jnp.moveaxis(b_end, 1, 0), jnp.moveaxis(m_loc, 1, 0), jnp.moveaxis(C_loc, 1, 0), jnp.moveaxis(n_loc, 1, 0))
    _, (C_in, n_in, m_in) = lax.scan(step, init, xs)
    C_in = jnp.moveaxis(C_in, 0, 1)
    n_in = jnp.moveaxis(n_in, 0, 1)
    m_in = jnp.moveaxis(m_in, 0, 1)
    Dlog = jnp.where(lower_mask(), bcum[..., :, None] - bcum[..., None, :] + li[..., None, :], -jnp.inf)
    inter = bcum + m_in[..., None]
    m_t = jnp.maximum(jnp.max(Dlog, axis=-1), inter)
    Pm = jnp.exp(Dlog - m_t[..., None]) * jnp.einsum('bcthk,bcshk->bchts', qc, kc)
    a_inter = jnp.exp(inter - m_t)
    num = jnp.einsum('bchts,bcshv->bcthv', Pm, vc) + jnp.einsum('bcthk,bchkv->bcthv', qc, C_in) * jnp.swapaxes(a_inter, 2, 3)[..., None]
    den = jnp.sum(Pm, axis=-1) + a_inter * jnp.einsum('bcthk,bchk->bcht', qc, n_in)
    denom = jnp.maximum(jnp.abs(den), jnp.exp(-m_t))
    h = num / jnp.swapaxes(denom, 2, 3)[..., None]
    return h.reshape(b, L, H, DV)


def encoder_layer(x, norm1_w, w_in, conv_w, conv_b, dt_bias, a_log, d_skip, ssd_norm_w, b_i, b_f, mlstm_norm_w, w_out, norm2_w, w_up, w_down):
    b, L, _ = x.shape
    G, R, P, N = SSD_GROUPS, SSD_HEADS_PER_GROUP, SSD_HEAD_DIM, SSD_STATE
    H, DK, DV = MLSTM_HEADS, MLSTM_QK_DIM, MLSTM_V_DIM
    h = rmsnorm(x, norm1_w)
    proj = h @ w_in
    z, xbc, dt_raw, q, k, v, o, i_pre, f_pre = jnp.split(proj, [int(s) for s in np.cumsum(PROJ_WIDTHS)[:-1]], axis=-1)

    xbc = jax.nn.silu(centred_depthwise_conv(xbc, conv_w, conv_b))
    xs, Bm, Cm = jnp.split(xbc, [D_SSD, D_SSD + G * N], axis=-1)
    xs = xs.reshape(b, L, G, R, P)
    Bm = Bm.reshape(b, L, G, N)
    Cm = Cm.reshape(b, L, G, N)
    dt = jax.nn.softplus(dt_raw.astype(jnp.float32).reshape(b, L, 2, G, R) + dt_bias.astype(jnp.float32).reshape(2, G, R))
    A = -jnp.exp(a_log.astype(jnp.float32)).reshape(2, G, R)
    y_fwd = ssd_scan(xs, dt[:, :, 0], A[0], Bm, Cm)
    y_bwd = flip(ssd_scan(flip(xs), flip(dt[:, :, 1]), A[1], flip(Bm), flip(Cm)))
    y = y_fwd + y_bwd + d_skip.astype(jnp.float32).reshape(G, R, 1) * xs
    y = y.reshape(b, L, D_SSD) * jax.nn.silu(z.astype(jnp.float32))
    y_ssd = rmsnorm(y.reshape(b, L, G, D_SSD // G), ssd_norm_w.reshape(G, D_SSD // G)).reshape(b, L, D_SSD)

    qh = q.reshape(b, L, H, DK)
    kh = k.reshape(b, L, H, DK)
    vh = v.reshape(b, L, H, DV)
    log_i = (i_pre.astype(jnp.float32) + b_i.astype(jnp.float32)).reshape(b, L, 2, H)
    log_f = jax.nn.log_sigmoid(f_pre.astype(jnp.float32) + b_f.astype(jnp.float32)).reshape(b, L, 2, H)
    h_fwd = mlstm_chunkwise(qh, kh, vh, log_i[:, :, 0], log_f[:, :, 0])
    h_bwd = flip(mlstm_chunkwise(flip(qh), flip(kh), flip(vh), flip(log_i[:, :, 1]), flip(log_f[:, :, 1])))
    hm = rmsnorm(h_fwd + h_bwd, mlstm_norm_w.reshape(H, DV)).reshape(b, L, D_MLSTM)
    y_m = jax.nn.sigmoid(o.astype(jnp.float32)) * hm

    mix = jnp.concatenate([y_ssd.astype(x.dtype), y_m.astype(x.dtype)], axis=-1) @ w_out
    x = x + mix.astype(x.dtype)

    h2 = rmsnorm(x, norm2_w)
    x = x + (jnp.square(jax.nn.relu(h2 @ w_up)) @ w_down).astype(x.dtype)
    return x


def trunk(x, norm1_w, w_in, conv_w, conv_b, dt_bias, a_log, d_skip, ssd_norm_w, b_i, b_f, mlstm_norm_w, w_out, norm2_w, w_up, w_down, final_norm_w):
    for l in range(DEPTH):
        x = encoder_layer(x, norm1_w[l], w_in[l], conv_w[l], conv_b[l], dt_bias[l], a_log[l], d_skip[l], ssd_norm_w[l], b_i[l], b_f[l], mlstm_norm_w[l], w_out[l], norm2_w[l], w_up[l], w_down[l])
    return rmsnorm(x, final_norm_w)


def setup_inputs(seed: int = 0) -> dict:
    key = jax.random.key(seed)
    ks = jax.random.split(key, 20)
    f32 = jnp.float32
    x_prompt = jax.random.normal(ks[0], (BATCH, SEQ, D_MODEL), f32)
    x_sample = jax.random.normal(ks[1], (DEC_BATCH, DEC_SEQ, D_MODEL), f32)
    norm1_w = 1.0 + 0.02 * jax.random.normal(ks[2], (DEPTH, D_MODEL), f32)
    w_in = jax.random.normal(ks[3], (DEPTH, D_MODEL, D_IN_PROJ), f32) * (D_MODEL ** -0.5)
    conv_w = jax.random.normal(ks[4], (DEPTH, D_CONV, SSD_XBC), f32) * (D_CONV ** -0.5)
    conv_b = 0.01 * jax.random.normal(ks[5], (DEPTH, SSD_XBC), f32)
    u = jax.random.uniform(ks[6], (DEPTH, 2, SSD_HEADS), f32)
    dt0 = jnp.exp(u * (jnp.log(0.1) - jnp.log(0.001)) + jnp.log(0.001))
    dt_bias = dt0 + jnp.log(-jnp.expm1(-dt0))
    a_log = jnp.log(jax.random.uniform(ks[7], (DEPTH, 2, SSD_HEADS), f32, 1.0, 16.0))
    d_skip = 1.0 + 0.1 * jax.random.normal(ks[8], (DEPTH, SSD_HEADS), f32)
    ssd_norm_w = 1.0 + 0.02 * jax.random.normal(ks[9], (DEPTH, D_SSD), f32)
    b_i = 0.1 * jax.random.normal(ks[10], (DEPTH, 2 * MLSTM_HEADS), f32)
    b_f = jax.random.uniform(ks[11], (DEPTH, 2 * MLSTM_HEADS), f32, 3.0, 6.0)
    mlstm_norm_w = 1.0 + 0.02 * jax.random.normal(ks[12], (DEPTH, D_MLSTM), f32)
    w_out = jax.random.normal(ks[13], (DEPTH, D_MIX, D_MODEL), f32) * (D_MIX ** -0.5)
    norm2_w = 1.0 + 0.02 * jax.random.normal(ks[14], (DEPTH, D_MODEL), f32)
    w_up = jax.random.normal(ks[15], (DEPTH, D_MODEL, D_FF), f32) * (D_MODEL ** -0.5)
    w_down = jax.random.normal(ks[16], (DEPTH, D_FF, D_MODEL), f32) * (D_FF ** -0.5)
    final_norm_w = 1.0 + 0.02 * jax.random.normal(ks[17], (D_MODEL,), f32)
    return {"x_prompt": x_prompt, "x_sample": x_sample, "norm1_w": norm1_w, "w_in": w_in, "conv_w": conv_w, "conv_b": conv_b, "dt_bias": dt_bias, "a_log": a_log, "d_skip": d_skip, "ssd_norm_w": ssd_norm_w, "b_i": b_i, "b_f": b_f, "mlstm_norm_w": mlstm_norm_w, "w_out": w_out, "norm2_w": norm2_w, "w_up": w_up, "w_down": w_down, "final_norm_w": final_norm_w}


def reference(x_prompt, x_sample, norm1_w, w_in, conv_w, conv_b, dt_bias, a_log, d_skip, ssd_norm_w, b_i, b_f, mlstm_norm_w, w_out, norm2_w, w_up, w_down, final_norm_w):
    y_prompt = trunk(x_prompt, norm1_w, w_in, conv_w, conv_b, dt_bias, a_log, d_skip, ssd_norm_w, b_i, b_f, mlstm_norm_w, w_out, norm2_w, w_up, w_down, final_norm_w)
    y_sample = trunk(x_sample, norm1_w, w_in, conv_w, conv_b, dt_bias, a_log, d_skip, ssd_norm_w, b_i, b_f, mlstm_norm_w, w_out, norm2_w, w_up, w_down, final_norm_w)
    return (y_prompt, y_sample)
```

```python
import functools

import jax
import jax.numpy as jnp
from jax import lax
from jax.experimental import pallas as pl
from jax.experimental.pallas import tpu as pltpu

F32 = jnp.float32
BF16 = jnp.bfloat16

CHUNK = 128
EPS = 1e-5
D_CONV = 5
SSD_GROUPS = 8
SSD_HEAD_DIM = 64
SSD_STATE = 128
MLSTM_HEADS = 16
VMEM_LIMIT = 56 * 1024 * 1024


def _cparams(sem):
    return pltpu.CompilerParams(dimension_semantics=sem, vmem_limit_bytes=VMEM_LIMIT)


def _sigmoid(x):
    return 1.0 / (1.0 + jnp.exp(-x))


def _softplus(x):
    return jnp.maximum(x, 0.0) + jnp.log1p(jnp.exp(-jnp.abs(x)))


def _dot(a, b):
    return jnp.dot(a, b, preferred_element_type=F32)


def _dot_nt(a, b):
    return lax.dot_general(a, b, (((1,), (1,)), ((), ())), preferred_element_type=F32)


def _rmsnorm_kernel(x_ref, w_ref, o_ref):
    x = x_ref[...].astype(F32)
    y = x * lax.rsqrt(jnp.mean(x * x, axis=-1, keepdims=True) + EPS)
    o_ref[...] = (y * w_ref[...]).astype(o_ref.dtype)


def rmsnorm_rows(x, w, out_dtype, bm=512):
    m, d = x.shape
    return pl.pallas_call(
        _rmsnorm_kernel,
        grid=(m // bm,),
        in_specs=[pl.BlockSpec((bm, d), lambda i: (i, 0)),
                  pl.BlockSpec((1, d), lambda i: (0, 0))],
        out_specs=pl.BlockSpec((bm, d), lambda i: (i, 0)),
        out_shape=jax.ShapeDtypeStruct((m, d), out_dtype),
        compiler_params=_cparams(("parallel",)),
        name="rmsnorm_rows",
    )(x, w.reshape(1, d).astype(F32))


def _add_rmsnorm_kernel(x_ref, y_ref, w_ref, o_ref):
    x = x_ref[...] + y_ref[...]
    y = x * lax.rsqrt(jnp.mean(x * x, axis=-1, keepdims=True) + EPS)
    o_ref[...] = y * w_ref[...]


def add_rmsnorm_rows(x, y, w, bm=256):
    m, d = x.shape
    return pl.pallas_call(
        _add_rmsnorm_kernel,
        grid=(m // bm,),
        in_specs=[pl.BlockSpec((bm, d), lambda i: (i, 0)),
                  pl.BlockSpec((bm, d), lambda i: (i, 0)),
                  pl.BlockSpec((1, d), lambda i: (0, 0))],
        out_specs=pl.BlockSpec((bm, d), lambda i: (i, 0)),
        out_shape=jax.ShapeDtypeStruct((m, d), F32),
        compiler_params=_cparams(("parallel",)),
        name="add_rmsnorm_rows",
    )(x, y, w.reshape(1, d).astype(F32))


def _matmul_kernel(a_ref, b_ref, o_ref):
    o_ref[...] = _dot(a_ref[...], b_ref[...]).astype(o_ref.dtype)


def matmul(a, b, out_dtype, bm=1024, bn=1024):
    m, k = a.shape
    n = b.shape[1]
    return pl.pallas_call(
        _matmul_kernel,
        grid=(n // bn, m // bm),
        in_specs=[pl.BlockSpec((bm, k), lambda j, i: (i, 0)),
                  pl.BlockSpec((k, bn), lambda j, i: (0, j))],
        out_specs=pl.BlockSpec((bm, bn), lambda j, i: (i, j)),
        out_shape=jax.ShapeDtypeStruct((m, n), out_dtype),
        compiler_params=_cparams(("parallel", "parallel")),
        name="matmul",
    )(a, b)


def _split3(x):
    hi = x.astype(BF16)
    r1 = x - hi.astype(F32)
    mid = r1.astype(BF16)
    r2 = r1 - mid.astype(F32)
    return hi, mid, r2.astype(BF16)


def _dot01(t, x):
    hi, mid, lo = _split3(x)
    return _dot(t, hi) + _dot(t, mid) + _dot(t, lo)


def _gates_kernel(h_ref, w_ref, bias_ref, alog_ref,
                  ssd_col_ref, ssd_row_ref, ml_ref, mlT_ref, *, bm):
    raw = _dot(h_ref[...], w_ref[...]) + bias_ref[...]
    dt = _softplus(raw[:, 0:128])
    a = dt * (-jnp.exp(alog_ref[...]))
    t2 = raw[:, 128:256]
    lane = lax.broadcasted_iota(jnp.int32, (CHUNK, 128), 1)
    lsig = -_softplus(-t2)
    row_i = lax.broadcasted_iota(jnp.int32, (CHUNK, CHUNK), 0)
    col_i = lax.broadcasted_iota(jnp.int32, (CHUNK, CHUNK), 1)
    lower = (col_i <= row_i).astype(BF16)
    upper = (col_i >= row_i).astype(BF16)
    ones = jnp.ones((CHUNK, CHUNK), BF16)
    ssd_fwd_lane = lane < 64
    ml_fwd_lane = (lane % 32) < 16
    for c in range(bm // CHUNK):
        sl = slice(c * CHUNK, (c + 1) * CHUNK)
        a_c = a[sl]
        dt_c = dt[sl]
        hi, mid, lo = _split3(a_c)
        cum_f = _dot(lower, hi) + _dot(lower, mid) + _dot(lower, lo)
        cum_b = _dot(upper, hi) + _dot(upper, mid) + _dot(upper, lo)
        tot = _dot(ones, hi) + _dot(ones, mid) + _dot(ones, lo)
        acum = jnp.where(ssd_fwd_lane, cum_f, cum_b)
        dtT = dt_c.T
        acumT = acum.T
        totT = tot.T
        for dg in range(16):
            shift = (128 - dg * 8) % 128
            rolled = acum if shift == 0 else pltpu.roll(acum, shift, 1)
            ssd_col_ref[dg, sl, :] = rolled
            ssd_row_ref[dg, 0:8, sl] = dtT[dg * 8:(dg + 1) * 8, :]
            ssd_row_ref[dg, 8:16, sl] = acumT[dg * 8:(dg + 1) * 8, :]
            ssd_row_ref[dg, 16:24, sl] = totT[dg * 8:(dg + 1) * 8, :]
        t2_c = t2[sl]
        ls_c = lsig[sl]
        hi, mid, lo = _split3(ls_c)
        mcum_f = _dot(lower, hi) + _dot(lower, mid) + _dot(lower, lo)
        mcum_b = _dot(upper, hi) + _dot(upper, mid) + _dot(upper, lo)
        mtot = _dot(ones, hi) + _dot(ones, mid) + _dot(ones, lo)
        mcum = jnp.where(ml_fwd_lane, mcum_f, mcum_b)
        ml_ref[sl, 0:128] = t2_c
        ml_ref[sl, 128:256] = mcum
        ml_ref[sl, 256:384] = mtot
        mlT_ref[0:128, sl] = t2_c.T
        mlT_ref[128:256, sl] = mcum.T
        mlT_ref[256:384, sl] = mtot.T


def gates(h, w_gate, bias, alog, bm=512):
    m, d = h.shape
    return pl.pallas_call(
        functools.partial(_gates_kernel, bm=bm),
        grid=(m // bm,),
        in_specs=[pl.BlockSpec((bm, d), lambda i: (i, 0)),
                  pl.BlockSpec((d, 256), lambda i: (0, 0)),
                  pl.BlockSpec((1, 256), lambda i: (0, 0)),
                  pl.BlockSpec((1, 128), lambda i: (0, 0))],
        out_specs=[pl.BlockSpec((16, bm, 128), lambda i: (0, i, 0)),
                   pl.BlockSpec((16, 24, bm), lambda i: (0, 0, i)),
                   pl.BlockSpec((bm, 384), lambda i: (i, 0)),
                   pl.BlockSpec((384, bm), lambda i: (0, i))],
        out_shape=[jax.ShapeDtypeStruct((16, m, 128), F32),
                   jax.ShapeDtypeStruct((16, 24, m), F32),
                   jax.ShapeDtypeStruct((m, 384), F32),
                   jax.ShapeDtypeStruct((384, m), F32)],
        compiler_params=_cparams(("parallel",)),
        name="gates",
    )(h, w_gate, bias, alog)


def _conv_kernel(cur_ref, prev_ref, next_ref, w_ref, b_ref, o_ref, *, bt, seq_len, transpose_out):
    i = pl.program_id(0)
    cur = cur_ref[...].astype(F32)
    at_start = (i * bt) % seq_len == 0
    at_end = ((i + 1) * bt) % seq_len == 0
    prev = jnp.where(at_start, 0.0, prev_ref[...].astype(F32))
    nxt = jnp.where(at_end, 0.0, next_ref[...].astype(F32))
    row = lax.broadcasted_iota(jnp.int32, cur.shape, 0)
    w = w_ref[...]
    m2 = pltpu.roll(cur, 2, 0)
    m2 = jnp.where(row == 0, prev[6:7], jnp.where(row == 1, prev[7:8], m2))
    m1 = pltpu.roll(cur, 1, 0)
    m1 = jnp.where(row == 0, prev[7:8], m1)
    p1 = pltpu.roll(cur, bt - 1, 0)
    p1 = jnp.where(row == bt - 1, nxt[0:1], p1)
    p2 = pltpu.roll(cur, bt - 2, 0)
    p2 = jnp.where(row == bt - 2, nxt[0:1], jnp.where(row == bt - 1, nxt[1:2], p2))
    out = w[0:1] * m2
    out = out + w[1:2] * m1
    out = out + w[2:3] * cur
    out = out + w[3:4] * p1
    out = out + w[4:5] * p2
    out = out + b_ref[...]
    out = out * _sigmoid(out)
    if transpose_out:
        o_ref[...] = out.T.astype(o_ref.dtype)
    else:
        o_ref[...] = out.astype(o_ref.dtype)


def conv_silu(proj, col0, ncols, conv_w, conv_b, seq_len, transpose_out, bt=512, bc=512):
    m = proj.shape[0]
    cb0 = col0 // bc
    nrb = m // 8
    if transpose_out:
        out_spec = pl.BlockSpec((bc, bt), lambda i, j: (j, i))
        out_shape = jax.ShapeDtypeStruct((ncols, m), BF16)
    else:
        out_spec = pl.BlockSpec((bt, bc), lambda i, j: (i, j))
        out_shape = jax.ShapeDtypeStruct((m, ncols), BF16)
    return pl.pallas_call(
        functools.partial(_conv_kernel, bt=bt, seq_len=seq_len, transpose_out=transpose_out),
        grid=(m // bt, ncols // bc),
        in_specs=[pl.BlockSpec((bt, bc), lambda i, j: (i, cb0 + j)),
                  pl.BlockSpec((8, bc), lambda i, j: (jnp.maximum(i * (bt // 8) - 1, 0), cb0 + j)),
                  pl.BlockSpec((8, bc), lambda i, j: (jnp.minimum((i + 1) * (bt // 8), nrb - 1), cb0 + j)),
                  pl.BlockSpec((D_CONV, bc), lambda i, j: (0, j)),
                  pl.BlockSpec((1, bc), lambda i, j: (0, j))],
        out_specs=out_spec,
        out_shape=out_shape,
        compiler_params=_cparams(("parallel", "parallel")),
        name="conv_silu_t" if transpose_out else "conv_silu",
    )(proj, proj, proj, conv_w, conv_b)


def _ssd_kernel(*refs, backward):
    if backward:
        (xsT_ref, b_ref, c_ref, col_ref, row_ref, yf_ref, z_ref, nw_ref, o_ref, s_ref) = refs
    else:
        (xsT_ref, b_ref, c_ref, col_ref, row_ref, dexp_ref, o_ref, s_ref) = refs
    R, P = 8, SSD_HEAD_DIM

    @pl.when(pl.program_id(2) == 0)
    def _():
        s_ref[...] = jnp.zeros_like(s_ref)

    xsT = xsT_ref[...]
    bm = b_ref[...]
    cm = c_ref[...]
    rows = row_ref[...]
    dt = rows[0:8]
    acum = rows[8:16]
    tot = rows[16:24]
    col = col_ref[...]
    cbT = _dot_nt(bm, cm)
    s_i = lax.broadcasted_iota(jnp.int32, (CHUNK, CHUNK), 0)
    l_i = lax.broadcasted_iota(jnp.int32, (CHUNK, CHUNK), 1)
    mask = (l_i <= s_i) if backward else (l_i >= s_i)
    s_old = s_ref[...]
    yoffT = _dot_nt(s_old.astype(BF16), cm)
    e_acum = jnp.exp(acum)
    dte = jnp.exp(tot - acum)
    e_tot = jnp.exp(tot)
    y_pieces = []
    xd_pieces = []
    for r in range(R):
        hs = slice(r * P, (r + 1) * P)
        xr = xsT[hs, :].astype(F32)
        xdt = xr * dt[r:r + 1, :]
        seg = acum[r:r + 1, :] - col[:, r:r + 1]
        dec = jnp.exp(jnp.where(mask, seg, -jnp.inf))
        mt = (cbT * dec).astype(BF16)
        y_r = _dot(xdt.astype(BF16), mt) + yoffT[hs, :] * e_acum[r:r + 1, :]
        if not backward:
            y_r = y_r + dexp_ref[hs, :] * xr
        y_pieces.append(y_r)
        xd_pieces.append((xdt * dte[r:r + 1, :]).astype(BF16))
    yT = jnp.concatenate(y_pieces, axis=0)
    upd = _dot(jnp.concatenate(xd_pieces, axis=0), bm)
    for r in range(R):
        hs = slice(r * P, (r + 1) * P)
        s_ref[hs, :] = s_old[hs, :] * e_tot[r:r + 1, :] + upd[hs, :]
    y = yT.T
    if backward:
        y = y + yf_ref[...]
        z = z_ref[...].astype(F32)
        y = y * (z * _sigmoid(z))
        y = y * lax.rsqrt(jnp.mean(y * y, axis=-1, keepdims=True) + EPS)
        o_ref[...] = (y * nw_ref[...]).astype(o_ref.dtype)
    else:
        o_ref[...] = y


def ssd_scan(xsT, bc, ssd_col, ssd_row, batch, seq_len, *, backward, dexp=None,
             y_fwd=None, proj=None, norm_w=None):
    m = xsT.shape[1]
    nc = seq_len // CHUNK
    G = SSD_GROUPS
    d_ssd = xsT.shape[0]
    gw = d_ssd // G
    doff = 8 if backward else 0

    def cg(b, c):
        return b * nc + ((nc - 1 - c) if backward else c)

    in_specs = [
        pl.BlockSpec((gw, CHUNK), lambda b, g, c: (g, cg(b, c))),
        pl.BlockSpec((CHUNK, SSD_STATE), lambda b, g, c: (cg(b, c), g)),
        pl.BlockSpec((CHUNK, SSD_STATE), lambda b, g, c: (cg(b, c), G + g)),
        pl.BlockSpec((None, CHUNK, 128), lambda b, g, c: (doff + g, cg(b, c), 0)),
        pl.BlockSpec((None, 24, CHUNK), lambda b, g, c: (doff + g, 0, cg(b, c))),
    ]
    args = [xsT, bc, bc, ssd_col, ssd_row]
    if backward:
        in_specs += [
            pl.BlockSpec((CHUNK, gw), lambda b, g, c: (cg(b, c), g)),
            pl.BlockSpec((CHUNK, gw), lambda b, g, c: (cg(b, c), g)),
            pl.BlockSpec((1, gw), lambda b, g, c: (0, g)),
        ]
        args += [y_fwd, proj, norm_w]
        out_dtype = BF16
    else:
        in_specs += [pl.BlockSpec((gw, 128), lambda b, g, c: (g, 0))]
        args += [dexp]
        out_dtype = F32
    return pl.pallas_call(
        functools.partial(_ssd_kernel, backward=backward),
        grid=(batch, G, nc),
        in_specs=in_specs,
        out_specs=pl.BlockSpec((CHUNK, gw), lambda b, g, c: (cg(b, c), g)),
        out_shape=jax.ShapeDtypeStruct((m, d_ssd), out_dtype),
        scratch_shapes=[pltpu.VMEM((gw, SSD_STATE), F32)],
        compiler_params=_cparams(("parallel", "parallel", "arbitrary")),
        name="ssd_bwd" if backward else "ssd_fwd",
    )(*args)


def _mlstm_kernel(*refs, backward, dk):
    if backward:
        (q_ref, k_ref, v_ref, ml_ref, mlT_ref, hf_ref, og_ref, nw_ref,
         o_ref, c_ref, n_ref, m_ref) = refs
    else:
        (q_ref, k_ref, v_ref, ml_ref, mlT_ref, o_ref, c_ref, n_ref, m_ref) = refs

    @pl.when(pl.program_id(2) == 0)
    def _():
        c_ref[...] = jnp.zeros_like(c_ref)
        n_ref[...] = jnp.zeros_like(n_ref)
        m_ref[...] = jnp.zeros_like(m_ref)

    hh = pl.program_id(1) + (MLSTM_HEADS if backward else 0)
    lane = lax.broadcasted_iota(jnp.int32, (CHUNK, 128), 1)

    def pick_col(tile, idx):
        return jnp.sum(jnp.where(lane == idx, tile, 0.0), axis=1, keepdims=True)

    li_col = pick_col(ml_ref[:, 0:128], hh)
    cum_col = pick_col(ml_ref[:, 128:256], 32 + hh)
    tot_col = pick_col(ml_ref[:, 256:384], 32 + hh)
    li_row = mlT_ref[pl.ds(hh, 1), :]
    cum_row = mlT_ref[pl.ds(128 + 32 + hh, 1), :]
    tot11 = tot_col[0:1, :]

    q = q_ref[...].astype(F32)
    k = k_ref[...].astype(F32)
    v = v_ref[...].astype(BF16)
    c_in = c_ref[...]
    n_in = n_ref[0:1, :]
    m_in = m_ref[0:1, 0:1]

    gvec = tot_col - cum_col + li_col
    m_loc = jnp.max(gvec, axis=0, keepdims=True)
    w = jnp.exp(gvec - m_loc)
    kw = k * w
    c_loc = _dot(kw.T.astype(BF16), v)
    n_loc = jnp.sum(kw, axis=0, keepdims=True)

    qs = (q * (dk ** -0.5)).astype(BF16)
    kb = k.astype(BF16)
    sqk = _dot_nt(qs, kb)
    t_i = lax.broadcasted_iota(jnp.int32, (CHUNK, CHUNK), 0)
    s_i = lax.broadcasted_iota(jnp.int32, (CHUNK, CHUNK), 1)
    mask = (s_i >= t_i) if backward else (s_i <= t_i)
    dlog = jnp.where(mask, cum_col - (cum_row - li_row), -jnp.inf)
    inter = cum_col + m_in
    m_t = jnp.maximum(jnp.max(dlog, axis=1, keepdims=True), inter)
    pm = jnp.exp(dlog - m_t) * sqk
    a_inter = jnp.exp(inter - m_t)
    num = _dot(pm.astype(BF16), v) + _dot(qs, c_in.astype(BF16)) * a_inter
    qn = jnp.sum(qs.astype(F32) * n_in, axis=1, keepdims=True)
    den = jnp.sum(pm, axis=1, keepdims=True) + a_inter * qn
    denom = jnp.maximum(jnp.abs(den), jnp.exp(-m_t))
    h = num / denom

    m_new = jnp.maximum(tot11 + m_in, m_loc)
    a_prev = jnp.exp(tot11 + m_in - m_new)
    a_loc = jnp.exp(m_loc - m_new)
    c_ref[...] = a_prev * c_in + a_loc * c_loc
    n_ref[...] = jnp.broadcast_to(a_prev * n_in + a_loc * n_loc, n_ref.shape)
    m_ref[...] = jnp.broadcast_to(m_new, m_ref.shape)

    if backward:
        h = h + hf_ref[...]
        h = h * lax.rsqrt(jnp.mean(h * h, axis=-1, keepdims=True) + EPS)
        h = h * nw_ref[...]
        o_ref[...] = (_sigmoid(og_ref[...].astype(F32)) * h).astype(o_ref.dtype)
    else:
        o_ref[...] = h


def mlstm_scan(proj, ml, mlT, batch, seq_len, cols, *, backward, h_fwd=None, norm_w=None):
    m = proj.shape[0]
    nc = seq_len // CHUNK
    H = MLSTM_HEADS
    dk, dv = cols["dk"], cols["dv"]
    qb, kb, vb, ob = cols["q"] // dk, cols["k"] // dk, cols["v"] // dv, cols["o"] // dv

    def cg(b, c):
        return b * nc + ((nc - 1 - c) if backward else c)

    in_specs = [
        pl.BlockSpec((CHUNK, dk), lambda b, h, c: (cg(b, c), qb + h)),
        pl.BlockSpec((CHUNK, dk), lambda b, h, c: (cg(b, c), kb + h)),
        pl.BlockSpec((CHUNK, dv), lambda b, h, c: (cg(b, c), vb + h)),
        pl.BlockSpec((CHUNK, 384), lambda b, h, c: (cg(b, c), 0)),
        pl.BlockSpec((384, CHUNK), lambda b, h, c: (0, cg(b, c))),
    ]
    args = [proj, proj, proj, ml, mlT]
    if backward:
        in_specs += [
            pl.BlockSpec((CHUNK, dv), lambda b, h, c: (cg(b, c), h)),
            pl.BlockSpec((CHUNK, dv), lambda b, h, c: (cg(b, c), ob + h)),
            pl.BlockSpec((1, dv), lambda b, h, c: (0, h)),
        ]
        args += [h_fwd, proj, norm_w]
        out_dtype = BF16
    else:
        out_dtype = F32
    return pl.pallas_call(
        functools.partial(_mlstm_kernel, backward=backward, dk=dk),
        grid=(batch, H, nc),
        in_specs=in_specs,
        out_specs=pl.BlockSpec((CHUNK, dv), lambda b, h, c: (cg(b, c), h)),
        out_shape=jax.ShapeDtypeStruct((m, H * dv), out_dtype),
        scratch_shapes=[pltpu.VMEM((dk, dv), F32), pltpu.VMEM((8, dk), F32),
                        pltpu.VMEM((8, 128), F32)],
        compiler_params=_cparams(("parallel", "parallel", "arbitrary")),
        name="mlstm_bwd" if backward else "mlstm_fwd",
    )(*args)


def _outproj_kernel(a1_ref, a2_ref, w1_ref, w2_ref, x_ref, o_ref):
    acc = _dot(a1_ref[...], w1_ref[...]) + _dot(a2_ref[...], w2_ref[...])
    o_ref[...] = x_ref[...] + acc


def outproj_residual(a1, a2, w1, w2, x, bm=512, bn=512):
    m, k = a1.shape
    n = w1.shape[1]
    return pl.pallas_call(
        _outproj_kernel,
        grid=(n // bn, m // bm),
        in_specs=[pl.BlockSpec((bm, k), lambda j, i: (i, 0)),
                  pl.BlockSpec((bm, k), lambda j, i: (i, 0)),
                  pl.BlockSpec((k, bn), lambda j, i: (0, j)),
                  pl.BlockSpec((k, bn), lambda j, i: (0, j)),
                  pl.BlockSpec((bm, bn), lambda j, i: (i, j))],
        out_specs=pl.BlockSpec((bm, bn), lambda j, i: (i, j)),
        out_shape=jax.ShapeDtypeStruct((m, n), F32),
        compiler_params=_cparams(("parallel", "parallel")),
        name="outproj_residual",
    )(a1, a2, w1, w2, x)


def _mlp_kernel(h_ref, wu_ref, wd_ref, o_ref):
    u = jnp.maximum(_dot(h_ref[...], wu_ref[...]), 0.0)
    part = _dot((u * u).astype(BF16), wd_ref[...])

    @pl.when(pl.program_id(1) == 0)
    def _():
        o_ref[...] = part

    @pl.when(pl.program_id(1) != 0)
    def _():
        o_ref[...] += part


def mlp(h, w_up, w_down, bm=512, bf=512):
    m, d = h.shape
    ff = w_up.shape[1]
    return pl.pallas_call(
        _mlp_kernel,
        grid=(m // bm, ff // bf),
        in_specs=[pl.BlockSpec((bm, d), lambda i, f: (i, 0)),
                  pl.BlockSpec((d, bf), lambda i, f: (0, f)),
                  pl.BlockSpec((bf, d), lambda i, f: (f, 0))],
        out_specs=pl.BlockSpec((bm, d), lambda i, f: (i, 0)),
        out_shape=jax.ShapeDtypeStruct((m, d), F32),
        compiler_params=_cparams(("parallel", "arbitrary")),
        name="mlp",
    )(h, w_up, w_down)


def _prep_layer(norm1_w, w_in, conv_w, conv_b, dt_bias, a_log, d_skip, ssd_norm_w, b_i, b_f,
                mlstm_norm_w, w_out, norm2_w, w_up, w_down):
    d_model = w_in.shape[0]
    d_mix = w_out.shape[0]
    d_ssd = d_mix // 2
    d_ml = d_mix - d_ssd
    n_ssd_heads = d_ssd // SSD_HEAD_DIM
    xbc_w = d_ssd + 2 * SSD_GROUPS * SSD_STATE
    dv = d_ml // MLSTM_HEADS
    dk = dv // 2
    widths = (d_ssd, xbc_w, 2 * n_ssd_heads, MLSTM_HEADS * dk, MLSTM_HEADS * dk, d_ml, d_ml,
              2 * MLSTM_HEADS, 2 * MLSTM_HEADS)
    offs = [0]
    for wd in widths:
        offs.append(offs[-1] + wd)
    assert offs[-1] == w_in.shape[1]
    assert 2 * n_ssd_heads == 128 and 2 * MLSTM_HEADS == 32
    w_main = jnp.concatenate([w_in[:, offs[0]:offs[2]], w_in[:, offs[3]:offs[7]]], axis=1).astype(BF16)
    w_gate = jnp.concatenate([w_in[:, offs[2]:offs[3]], w_in[:, offs[7]:offs[9]],
                              jnp.zeros((d_model, 64), w_in.dtype)], axis=1).astype(BF16)
    gate_bias = jnp.concatenate([dt_bias.reshape(-1), b_i.reshape(-1), b_f.reshape(-1),
                                 jnp.zeros((64,), F32)]).astype(F32).reshape(1, 256)
    cols = {"z": 0, "xbc": d_ssd, "q": d_ssd + xbc_w}
    cols["k"] = cols["q"] + MLSTM_HEADS * dk
    cols["v"] = cols["k"] + MLSTM_HEADS * dk
    cols["o"] = cols["v"] + d_ml
    cols["dk"], cols["dv"], cols["d_ssd"] = dk, dv, d_ssd
    return dict(
        norm1_w=norm1_w, w_main=w_main, w_gate=w_gate, gate_bias=gate_bias,
        alog=a_log.reshape(1, 128).astype(F32),
        conv_w=conv_w.astype(F32), conv_b=conv_b.reshape(1, -1).astype(F32),
        dexp=jnp.broadcast_to(jnp.repeat(d_skip.astype(F32), SSD_HEAD_DIM)[:, None], (d_ssd, 128)),
        ssd_norm_w=ssd_norm_w.reshape(1, -1).astype(F32),
        mlstm_norm_w=mlstm_norm_w.reshape(1, -1).astype(F32),
        w_out1=w_out[:d_ssd].astype(BF16), w_out2=w_out[d_ssd:].astype(BF16),
        norm2_w=norm2_w, w_up=w_up.astype(BF16), w_down=w_down.astype(BF16), cols=cols)


def _layer(x, p, batch, seq_len):
    cols = p["cols"]
    d_ssd = cols["d_ssd"]
    h = rmsnorm_rows(x, p["norm1_w"], BF16)
    proj = matmul(h, p["w_main"], F32)
    ssd_col, ssd_row, ml, mlT = gates(h, p["w_gate"], p["gate_bias"], p["alog"])
    xsT = conv_silu(proj, cols["xbc"], d_ssd, p["conv_w"][:, :d_ssd], p["conv_b"][:, :d_ssd],
                    seq_len, True)
    bc = conv_silu(proj, cols["xbc"] + d_ssd, 2 * SSD_GROUPS * SSD_STATE, p["conv_w"][:, d_ssd:],
                   p["conv_b"][:, d_ssd:], seq_len, False)
    y_f = ssd_scan(xsT, bc, ssd_col, ssd_row, batch, seq_len, backward=False, dexp=p["dexp"])
    mix1 = ssd_scan(xsT, bc, ssd_col, ssd_row, batch, seq_len, backward=True, y_fwd=y_f,
                    proj=proj, norm_w=p["ssd_norm_w"])
    h_f = mlstm_scan(proj, ml, mlT, batch, seq_len, cols, backward=False)
    mix2 = mlstm_scan(proj, ml, mlT, batch, seq_len, cols, backward=True, h_fwd=h_f,
                      norm_w=p["mlstm_norm_w"])
    x1 = outproj_residual(mix1, mix2, p["w_out1"], p["w_out2"], x)
    h2 = rmsnorm_rows(x1, p["norm2_w"], BF16)
    y = mlp(h2, p["w_up"], p["w_down"])
    return x1, y


def _trunk(x, layers, final_norm_w):
    batch, seq_len, d = x.shape
    xf = x.reshape(batch * seq_len, d)
    n_layers = len(layers)
    out = None
    for li, p in enumerate(layers):
        x1, y = _layer(xf, p, batch, seq_len)
        if li + 1 < n_layers:
            xf = x1 + y
        else:
            out = add_rmsnorm_rows(x1, y, final_norm_w)
    return out.reshape(batch, seq_len, d)


def kernel(x_prompt, x_sample, norm1_w, w_in, conv_w, conv_b, dt_bias, a_log, d_skip, ssd_norm_w,
           b_i, b_f, mlstm_norm_w, w_out, norm2_w, w_up, w_down, final_norm_w):
    depth = w_in.shape[0]
    layers = [_prep_layer(norm1_w[l], w_in[l], conv_w[l], conv_b[l], dt_bias[l], a_log[l], d_skip[l],
                          ssd_norm_w[l], b_i[l], b_f[l], mlstm_norm_w[l], w_out[l], norm2_w[l],
                          w_up[l], w_down[l]) for l in range(depth)]
    y_prompt = _trunk(x_prompt, layers, final_norm_w)
    y_sample = _trunk(x_sample, layers, final_norm_w)
    return (y_prompt, y_sample)
```

```python
import functools

import jax
import jax.numpy as jnp
from jax import lax
from jax.experimental import pallas as pl
from jax.experimental.pallas import tpu as pltpu

F32 = jnp.float32
BF16 = jnp.bfloat16

CHUNK = 128
EPS = 1e-5
D_CONV = 5
SSD_GROUPS = 8
SSD_HEAD_DIM = 64
SSD_STATE = 128
MLSTM_HEADS = 16
VMEM_LIMIT = 56 * 1024 * 1024


def _cparams(sem):
    return pltpu.CompilerParams(dimension_semantics=sem, vmem_limit_bytes=VMEM_LIMIT)


def _sigmoid(x):
    return 1.0 / (1.0 + jnp.exp(-x))


def _softplus(x):
    return jnp.maximum(x, 0.0) + jnp.log1p(jnp.exp(-jnp.abs(x)))


def _dot(a, b):
    return jnp.dot(a, b, preferred_element_type=F32)


def _dot_nt(a, b):
    return lax.dot_general(a, b, (((1,), (1,)), ((), ())), preferred_element_type=F32)


def _rmsnorm_kernel(x_ref, w_ref, o_ref):
    x = x_ref[...].astype(F32)
    y = x * lax.rsqrt(jnp.mean(x * x, axis=-1, keepdims=True) + EPS)
    o_ref[...] = (y * w_ref[...]).astype(o_ref.dtype)


def rmsnorm_rows(x, w, out_dtype, bm=512):
    m, d = x.shape
    return pl.pallas_call(
        _rmsnorm_kernel,
        grid=(m // bm,),
        in_specs=[pl.BlockSpec((bm, d), lambda i: (i, 0)),
                  pl.BlockSpec((1, d), lambda i: (0, 0))],
        out_specs=pl.BlockSpec((bm, d), lambda i: (i, 0)),
        out_shape=jax.ShapeDtypeStruct((m, d), out_dtype),
        compiler_params=_cparams(("parallel",)),
        name="rmsnorm_rows",
    )(x, w.reshape(1, d).astype(F32))


def _matmul_kernel(a_ref, b_ref, o_ref, *, relu2):
    acc = _dot(a_ref[...], b_ref[...])
    if relu2:
        acc = jnp.maximum(acc, 0.0)
        acc = acc * acc
    o_ref[...] = acc.astype(o_ref.dtype)


def matmul(a, b, out_dtype, bm=1024, bn=1024, relu2=False):
    m, k = a.shape
    n = b.shape[1]
    return pl.pallas_call(
        functools.partial(_matmul_kernel, relu2=relu2),
        grid=(n // bn, m // bm),
        in_specs=[pl.BlockSpec((bm, k), lambda j, i: (i, 0)),
                  pl.BlockSpec((k, bn), lambda j, i: (0, j))],
        out_specs=pl.BlockSpec((bm, bn), lambda j, i: (i, j)),
        out_shape=jax.ShapeDtypeStruct((m, n), out_dtype),
        compiler_params=_cparams(("parallel", "parallel")),
        name="matmul_relu2" if relu2 else "matmul",
    )(a, b)


def _matmul_ksplit_res_kernel(a_ref, b_ref, x_ref, o_ref):
    part = _dot(a_ref[...], b_ref[...])

    @pl.when(pl.program_id(2) == 0)
    def _():
        o_ref[...] = x_ref[...] + part

    @pl.when(pl.program_id(2) != 0)
    def _():
        o_ref[...] += part


def matmul_ksplit_residual(a, b, x, bm=1024, bn=1024, bk=2048):
    m, k = a.shape
    n = b.shape[1]
    return pl.pallas_call(
        _matmul_ksplit_res_kernel,
        grid=(n // bn, m // bm, k // bk),
        in_specs=[pl.BlockSpec((bm, bk), lambda j, i, kk: (i, kk)),
                  pl.BlockSpec((bk, bn), lambda j, i, kk: (kk, j)),
                  pl.BlockSpec((bm, bn), lambda j, i, kk: (i, j))],
        out_specs=pl.BlockSpec((bm, bn), lambda j, i, kk: (i, j)),
        out_shape=jax.ShapeDtypeStruct((m, n), F32),
        compiler_params=_cparams(("parallel", "parallel", "arbitrary")),
        name="matmul_ksplit_residual",
    )(a, b, x)


def _split3(x):
    hi = x.astype(BF16)
    r1 = x - hi.astype(F32)
    mid = r1.astype(BF16)
    r2 = r1 - mid.astype(F32)
    return hi, mid, r2.astype(BF16)


def _gates_kernel(h_ref, w_ref, bias_ref, alog_ref, ssd_ref, ssdT_ref, ml_ref, mlT_ref, *, bm):
    raw = _dot(h_ref[...], w_ref[...]) + bias_ref[...]
    dt = _softplus(raw[:, 0:128])
    a = dt * (-jnp.exp(alog_ref[...]))
    t2 = raw[:, 128:256]
    lane = lax.broadcasted_iota(jnp.int32, (CHUNK, 128), 1)
    row_t = lax.broadcasted_iota(jnp.int32, (CHUNK, 128), 0)
    lsig = -_softplus(-t2)
    row_i = lax.broadcasted_iota(jnp.int32, (CHUNK, CHUNK), 0)
    col_i = lax.broadcasted_iota(jnp.int32, (CHUNK, CHUNK), 1)
    lower = (col_i <= row_i).astype(BF16)
    upper = (col_i >= row_i).astype(BF16)
    ones = jnp.ones((CHUNK, CHUNK), BF16)
    ssd_fwd_lane = lane < 64
    ml_fwd_lane = (lane % 32) < 16

    def cums(x, fwd_lane):
        hi, mid, lo = _split3(x)
        cum_f = _dot(lower, hi) + _dot(lower, mid) + _dot(lower, lo)
        cum_b = _dot(upper, hi) + _dot(upper, mid) + _dot(upper, lo)
        tot = _dot(ones, hi) + _dot(ones, mid) + _dot(ones, lo)
        return jnp.where(fwd_lane, cum_f, cum_b), tot

    for c in range(bm // CHUNK):
        sl = slice(c * CHUNK, (c + 1) * CHUNK)
        dt_c = dt[sl]
        acum, tot = cums(a[sl], ssd_fwd_lane)
        ssd_ref[sl, 0:128] = dt_c
        ssd_ref[sl, 128:256] = acum
        ssd_ref[sl, 256:384] = tot
        ssdT_ref[0:128, sl] = dt_c.T
        ssdT_ref[128:256, sl] = acum.T
        ssdT_ref[256:384, sl] = tot.T
        t2_c = t2[sl]
        mcum, mtot = cums(lsig[sl], ml_fwd_lane)
        y = mcum - pltpu.roll(t2_c, 32, 1)
        y_f, y_b = y, y
        for d in (1, 2, 4, 8, 16, 32, 64):
            y_f = jnp.minimum(y_f, jnp.where(row_t >= d, pltpu.roll(y_f, d, 0), jnp.inf))
            y_b = jnp.minimum(y_b, jnp.where(row_t < CHUNK - d, pltpu.roll(y_b, CHUNK - d, 0), jnp.inf))
        rmax = mcum - jnp.where(ml_fwd_lane, y_f, y_b)
        ml_ref[sl, 0:128] = t2_c
        ml_ref[sl, 128:256] = mcum
        ml_ref[sl, 256:384] = mtot
        ml_ref[sl, 384:512] = rmax
        mlT_ref[0:128, sl] = t2_c.T
        mlT_ref[128:256, sl] = mcum.T
        mlT_ref[256:384, sl] = mtot.T


def gates(h, w_gate, bias, alog, bm=512):
    m, d = h.shape
    nat = pl.BlockSpec((bm, 384), lambda i: (i, 0))
    nat4 = pl.BlockSpec((bm, 512), lambda i: (i, 0))
    tr = pl.BlockSpec((384, bm), lambda i: (0, i))
    return pl.pallas_call(
        functools.partial(_gates_kernel, bm=bm),
        grid=(m // bm,),
        in_specs=[pl.BlockSpec((bm, d), lambda i: (i, 0)),
                  pl.BlockSpec((d, 256), lambda i: (0, 0)),
                  pl.BlockSpec((1, 256), lambda i: (0, 0)),
                  pl.BlockSpec((1, 128), lambda i: (0, 0))],
        out_specs=[nat, tr, nat4, tr],
        out_shape=[jax.ShapeDtypeStruct((m, 384), F32), jax.ShapeDtypeStruct((384, m), F32),
                   jax.ShapeDtypeStruct((m, 512), F32), jax.ShapeDtypeStruct((384, m), F32)],
        compiler_params=_cparams(("parallel",)),
        name="gates",
    )(h, w_gate, bias, alog)


def _conv_kernel(cur_ref, prev_ref, next_ref, w_ref, b_ref, o_ref, *, bt, seq_len, transpose_out):
    i = pl.program_id(0)
    cur = cur_ref[...].astype(F32)
    at_start = (i * bt) % seq_len == 0
    at_end = ((i + 1) * bt) % seq_len == 0
    prev = jnp.where(at_start, 0.0, prev_ref[...].astype(F32))
    nxt = jnp.where(at_end, 0.0, next_ref[...].astype(F32))
    row = lax.broadcasted_iota(jnp.int32, cur.shape, 0)
    w = w_ref[...]
    m2 = pltpu.roll(cur, 2, 0)
    m2 = jnp.where(row == 0, prev[6:7], jnp.where(row == 1, prev[7:8], m2))
    m1 = pltpu.roll(cur, 1, 0)
    m1 = jnp.where(row == 0, prev[7:8], m1)
    p1 = pltpu.roll(cur, bt - 1, 0)
    p1 = jnp.where(row == bt - 1, nxt[0:1], p1)
    p2 = pltpu.roll(cur, bt - 2, 0)
    p2 = jnp.where(row == bt - 2, nxt[0:1], jnp.where(row == bt - 1, nxt[1:2], p2))
    out = w[0:1] * m2
    out = out + w[1:2] * m1
    out = out + w[2:3] * cur
    out = out + w[3:4] * p1
    out = out + w[4:5] * p2
    out = out + b_ref[...]
    out = out * _sigmoid(out)
    if transpose_out:
        o_ref[...] = out.T.astype(o_ref.dtype)
    else:
        o_ref[...] = out.astype(o_ref.dtype)


def conv_silu(proj, col0, ncols, conv_w, conv_b, seq_len, transpose_out, bt=512, bc=512):
    m = proj.shape[0]
    cb0 = col0 // bc
    nrb = m // 8
    if transpose_out:
        out_spec = pl.BlockSpec((bc, bt), lambda i, j: (j, i))
        out_shape = jax.ShapeDtypeStruct((ncols, m), BF16)
    else:
        out_spec = pl.BlockSpec((bt, bc), lambda i, j: (i, j))
        out_shape = jax.ShapeDtypeStruct((m, ncols), BF16)
    return pl.pallas_call(
        functools.partial(_conv_kernel, bt=bt, seq_len=seq_len, transpose_out=transpose_out),
        grid=(m // bt, ncols // bc),
        in_specs=[pl.BlockSpec((bt, bc), lambda i, j: (i, cb0 + j)),
                  pl.BlockSpec((8, bc), lambda i, j: (jnp.maximum(i * (bt // 8) - 1, 0), cb0 + j)),
                  pl.BlockSpec((8, bc), lambda i, j: (jnp.minimum((i + 1) * (bt // 8), nrb - 1), cb0 + j)),
                  pl.BlockSpec((D_CONV, bc), lambda i, j: (0, j)),
                  pl.BlockSpec((1, bc), lambda i, j: (0, j))],
        out_specs=out_spec,
        out_shape=out_shape,
        compiler_params=_cparams(("parallel", "parallel")),
        name="conv_silu_t" if transpose_out else "conv_silu",
    )(proj, proj, proj, conv_w, conv_b)


def _ssd_kernel(*refs, backward):
    if backward:
        (xsT_ref, b_ref, c_ref, nat_ref, tr_ref, yf_ref, z_ref, nw_ref, o_ref, s_ref) = refs
    else:
        (xsT_ref, b_ref, c_ref, nat_ref, tr_ref, dexp_ref, o_ref, s_ref) = refs
    G, R, P, N = SSD_GROUPS, 8, SSD_HEAD_DIM, SSD_STATE
    gw = R * P
    h0 = G * R if backward else 0

    @pl.when(pl.program_id(1) == 0)
    def _():
        s_ref[...] = jnp.zeros_like(s_ref)

    s_i = lax.broadcasted_iota(jnp.int32, (CHUNK, CHUNK), 0)
    l_i = lax.broadcasted_iota(jnp.int32, (CHUNK, CHUNK), 1)
    mask = (l_i <= s_i) if backward else (l_i >= s_i)
    for g in range(G):
        gs = slice(g * gw, (g + 1) * gw)
        hg = h0 + g * R
        bm = b_ref[:, g * N:(g + 1) * N]
        cm = c_ref[:, g * N:(g + 1) * N]
        dt = tr_ref[hg:hg + R, :]
        acum = tr_ref[128 + hg:128 + hg + R, :]
        tot = tr_ref[256 + hg:256 + hg + R, :]
        cbT = _dot_nt(bm, cm)
        s_old = s_ref[gs, :]
        yoffT = _dot_nt(s_old.astype(BF16), cm)
        e_acum = jnp.exp(acum)
        dte = jnp.exp(tot - acum)
        e_tot = jnp.exp(tot)
        y_pieces = []
        xd_pieces = []
        for r in range(R):
            hs = slice(g * gw + r * P, g * gw + (r + 1) * P)
            xr = xsT_ref[hs, :].astype(F32)
            xdt = xr * dt[r:r + 1, :]
            col = nat_ref[:, 128 + hg + r:128 + hg + r + 1]
            seg = acum[r:r + 1, :] - col
            dec = jnp.exp(jnp.where(mask, seg, -jnp.inf))
            mt = (cbT * dec).astype(BF16)
            y_r = _dot(xdt.astype(BF16), mt) + yoffT[r * P:(r + 1) * P, :] * e_acum[r:r + 1, :]
            if not backward:
                y_r = y_r + dexp_ref[hs, :] * xr
            y_pieces.append(y_r)
            xd_pieces.append((xdt * dte[r:r + 1, :]).astype(BF16))
        yT = jnp.concatenate(y_pieces, axis=0)
        upd = _dot(jnp.concatenate(xd_pieces, axis=0), bm)
        for r in range(R):
            hs = slice(r * P, (r + 1) * P)
            s_ref[g * gw + r * P:g * gw + (r + 1) * P, :] = s_old[hs, :] * e_tot[r:r + 1, :] + upd[hs, :]
        y = yT.T
        if backward:
            y = y + yf_ref[:, gs]
            z = z_ref[:, gs].astype(F32)
            y = y * (z * _sigmoid(z))
            y = y * lax.rsqrt(jnp.mean(y * y, axis=-1, keepdims=True) + EPS)
            o_ref[:, gs] = (y * nw_ref[:, gs]).astype(o_ref.dtype)
        else:
            o_ref[:, gs] = y


def ssd_scan(xsT, bc, ssd, ssdT, batch, seq_len, *, backward, dexp=None,
             y_fwd=None, proj=None, norm_w=None):
    m = xsT.shape[1]
    nc = seq_len // CHUNK
    G = SSD_GROUPS
    d_ssd = xsT.shape[0]

    def cg(b, c):
        return b * nc + ((nc - 1 - c) if backward else c)

    in_specs = [
        pl.BlockSpec((d_ssd, CHUNK), lambda b, c: (0, cg(b, c))),
        pl.BlockSpec((CHUNK, G * SSD_STATE), lambda b, c: (cg(b, c), 0)),
        pl.BlockSpec((CHUNK, G * SSD_STATE), lambda b, c: (cg(b, c), 1)),
        pl.BlockSpec((CHUNK, 384), lambda b, c: (cg(b, c), 0)),
        pl.BlockSpec((384, CHUNK), lambda b, c: (0, cg(b, c))),
    ]
    args = [xsT, bc, bc, ssd, ssdT]
    if backward:
        in_specs += [
            pl.BlockSpec((CHUNK, d_ssd), lambda b, c: (cg(b, c), 0)),
            pl.BlockSpec((CHUNK, d_ssd), lambda b, c: (cg(b, c), 0)),
            pl.BlockSpec((1, d_ssd), lambda b, c: (0, 0)),
        ]
        args += [y_fwd, proj, norm_w]
        out_dtype = BF16
    else:
        in_specs += [pl.BlockSpec((d_ssd, 128), lambda b, c: (0, 0))]
        args += [dexp]
        out_dtype = F32
    return pl.pallas_call(
        functools.partial(_ssd_kernel, backward=backward),
        grid=(batch, nc),
        in_specs=in_specs,
        out_specs=pl.BlockSpec((CHUNK, d_ssd), lambda b, c: (cg(b, c), 0)),
        out_shape=jax.ShapeDtypeStruct((m, d_ssd), out_dtype),
        scratch_shapes=[pltpu.VMEM((d_ssd, SSD_STATE), F32)],
        compiler_params=_cparams(("parallel", "arbitrary")),
        name="ssd_bwd" if backward else "ssd_fwd",
    )(*args)


def _mlstm_kernel(*refs, backward, dk, dv, heads_per_group):
    if backward:
        (q_ref, k_ref, v_ref, nat_ref, tr_ref, hf_ref, og_ref, nw_ref, o_ref, c_ref, m_ref) = refs
    else:
        (q_ref, k_ref, v_ref, nat_ref, tr_ref, o_ref, c_ref, m_ref) = refs
    H = MLSTM_HEADS

    @pl.when(pl.program_id(1) == 0)
    def _():
        c_ref[...] = jnp.zeros_like(c_ref)
        m_ref[...] = jnp.zeros_like(m_ref)

    t_i = lax.broadcasted_iota(jnp.int32, (CHUNK, CHUNK), 0)
    s_i = lax.broadcasted_iota(jnp.int32, (CHUNK, CHUNK), 1)
    mask = (s_i >= t_i) if backward else (s_i <= t_i)
    ones_blk = jnp.ones((CHUNK, 128), BF16)
    scale = dk ** -0.5
    dense = (CHUNK, CHUNK)

    for g0 in range(0, H, heads_per_group):
        hs = list(range(g0, g0 + heads_per_group))
        lane_of = {h: h + (H if backward else 0) for h in hs}
        st = {h: {} for h in hs}
        for h in hs:
            d, hh = st[h], lane_of[h]
            li_row = tr_ref[hh:hh + 1, :]
            cum_row = tr_ref[160 + hh:161 + hh, :]
            d["tot"] = tr_ref[288 + hh:289 + hh, :]
            d["base_row"] = cum_row - li_row
            grow = d["tot"] - d["base_row"]
            d["m_loc"] = jnp.broadcast_to(jnp.max(grow, axis=1, keepdims=True), (1, CHUNK))
            d["w_row"] = jnp.exp(grow - d["m_loc"])
            d["cum_d"] = jnp.broadcast_to(nat_ref[:, 160 + hh:161 + hh], dense)
            d["rmax_d"] = jnp.broadcast_to(nat_ref[:, 416 + hh:417 + hh], dense)
            d["m_in"] = m_ref[h, 0:1, :]
            d["v_aug"] = jnp.concatenate([v_ref[:, h * dv:(h + 1) * dv].astype(BF16), ones_blk], axis=1)
        for h in hs:
            d = st[h]
            k = k_ref[:, h * dk:(h + 1) * dk].astype(F32)
            d["kb"] = k.astype(BF16)
            d["kwT"] = (k.T * d["w_row"]).astype(BF16)
            d["qs"] = (q_ref[:, h * dk:(h + 1) * dk].astype(F32) * scale).astype(BF16)
        for h in hs:
            d = st[h]
            d["c_loc"] = _dot(d["kwT"], d["v_aug"])
            d["sqk"] = _dot_nt(d["qs"], d["kb"])
            d["c_in"] = c_ref[h]
            d["qc"] = _dot(d["qs"], d["c_in"].astype(BF16))
        for h in hs:
            d = st[h]
            dlog = jnp.where(mask, d["cum_d"] - d["base_row"], -jnp.inf)
            inter = d["cum_d"] + d["m_in"]
            d["m_t"] = jnp.maximum(d["rmax_d"], inter)
            d["pm"] = (jnp.exp(dlog - d["m_t"]) * d["sqk"]).astype(BF16)
            d["a_inter"] = jnp.exp(inter - d["m_t"])
        for h in hs:
            d = st[h]
            a3 = jnp.concatenate([d["a_inter"]] * (dv // 128 + 1), axis=1)
            num = _dot(d["pm"], d["v_aug"]) + d["qc"] * a3
            den = num[:, dv:dv + 128]
            inv = 1.0 / jnp.maximum(jnp.abs(den), jnp.exp(-d["m_t"]))
            d["hout"] = num[:, 0:dv] * jnp.concatenate([inv] * (dv // 128), axis=1)
        for h in hs:
            d = st[h]
            m_new = jnp.maximum(d["tot"] + d["m_in"], d["m_loc"])
            a_prev = jnp.exp(d["tot"] + d["m_in"] - m_new)
            a_loc = jnp.exp(d["m_loc"] - m_new)
            a_prev3 = jnp.concatenate([a_prev] * (dv // 128 + 1), axis=1)
            a_loc3 = jnp.concatenate([a_loc] * (dv // 128 + 1), axis=1)
            c_ref[h] = a_prev3 * d["c_in"] + a_loc3 * d["c_loc"]
            m_ref[h] = jnp.broadcast_to(m_new, (8, 128))
        for h in hs:
            hout = st[h]["hout"]
            vs = slice(h * dv, (h + 1) * dv)
            if backward:
                hout = hout + hf_ref[:, vs]
                hout = hout * lax.rsqrt(jnp.mean(hout * hout, axis=-1, keepdims=True) + EPS)
                hout = hout * nw_ref[:, vs]
                o_ref[:, vs] = (_sigmoid(og_ref[:, vs].astype(F32)) * hout).astype(o_ref.dtype)
            else:
                o_ref[:, vs] = hout


def mlstm_scan(proj, ml, mlT, batch, seq_len, cols, *, backward, h_fwd=None, norm_w=None):
    m = proj.shape[0]
    nc = seq_len // CHUNK
    H = MLSTM_HEADS
    dk, dv = cols["dk"], cols["dv"]
    qw, vw = H * dk, H * dv
    qb, kb, vb, ob = cols["q"] // qw, cols["k"] // qw, cols["v"] // vw, cols["o"] // vw
    assert qb * qw == cols["q"] and kb * qw == cols["k"] and vb * vw == cols["v"] and ob * vw == cols["o"]

    def cg(b, c):
        return b * nc + ((nc - 1 - c) if backward else c)

    in_specs = [
        pl.BlockSpec((CHUNK, qw), lambda b, c: (cg(b, c), qb)),
        pl.BlockSpec((CHUNK, qw), lambda b, c: (cg(b, c), kb)),
        pl.BlockSpec((CHUNK, vw), lambda b, c: (cg(b, c), vb)),
        pl.BlockSpec((CHUNK, 512), lambda b, c: (cg(b, c), 0)),
        pl.BlockSpec((384, CHUNK), lambda b, c: (0, cg(b, c))),
    ]
    args = [proj, proj, proj, ml, mlT]
    if backward:
        in_specs += [
            pl.BlockSpec((CHUNK, vw), lambda b, c: (cg(b, c), 0)),
            pl.BlockSpec((CHUNK, vw), lambda b, c: (cg(b, c), ob)),
            pl.BlockSpec((1, vw), lambda b, c: (0, 0)),
        ]
        args += [h_fwd, proj, norm_w]
        out_dtype = BF16
    else:
        out_dtype = F32
    return pl.pallas_call(
        functools.partial(_mlstm_kernel, backward=backward, dk=dk, dv=dv, heads_per_group=8),
        grid=(batch, nc),
        in_specs=in_specs,
        out_specs=pl.BlockSpec((CHUNK, vw), lambda b, c: (cg(b, c), 0)),
        out_shape=jax.ShapeDtypeStruct((m, vw), out_dtype),
        scratch_shapes=[pltpu.VMEM((H, dk, dv + 128), F32), pltpu.VMEM((H, 8, 128), F32)],
        compiler_params=_cparams(("parallel", "arbitrary")),
        name="mlstm_bwd" if backward else "mlstm_fwd",
    )(*args)


def _outproj_kernel(a1_ref, a2_ref, w1_ref, w2_ref, x_ref, o_ref):
    acc = _dot(a1_ref[...], w1_ref[...]) + _dot(a2_ref[...], w2_ref[...])
    o_ref[...] = x_ref[...] + acc


def outproj_residual(a1, a2, w1, w2, x, bm=512, bn=512):
    m, k = a1.shape
    n = w1.shape[1]
    return pl.pallas_call(
        _outproj_kernel,
        grid=(n // bn, m // bm),
        in_specs=[pl.BlockSpec((bm, k), lambda j, i: (i, 0)),
                  pl.BlockSpec((bm, k), lambda j, i: (i, 0)),
                  pl.BlockSpec((k, bn), lambda j, i: (0, j)),
                  pl.BlockSpec((k, bn), lambda j, i: (0, j)),
                  pl.BlockSpec((bm, bn), lambda j, i: (i, j))],
        out_specs=pl.BlockSpec((bm, bn), lambda j, i: (i, j)),
        out_shape=jax.ShapeDtypeStruct((m, n), F32),
        compiler_params=_cparams(("parallel", "parallel")),
        name="outproj_residual",
    )(a1, a2, w1, w2, x)


def _prep_layer(norm1_w, w_in, conv_w, conv_b, dt_bias, a_log, d_skip, ssd_norm_w, b_i, b_f,
                mlstm_norm_w, w_out, norm2_w, w_up, w_down):
    d_model = w_in.shape[0]
    d_mix = w_out.shape[0]
    d_ssd = d_mix // 2
    d_ml = d_mix - d_ssd
    n_ssd_heads = d_ssd // SSD_HEAD_DIM
    xbc_w = d_ssd + 2 * SSD_GROUPS * SSD_STATE
    dv = d_ml // MLSTM_HEADS
    dk = dv // 2
    widths = (d_ssd, xbc_w, 2 * n_ssd_heads, MLSTM_HEADS * dk, MLSTM_HEADS * dk, d_ml, d_ml,
              2 * MLSTM_HEADS, 2 * MLSTM_HEADS)
    offs = [0]
    for wd in widths:
        offs.append(offs[-1] + wd)
    assert offs[-1] == w_in.shape[1]
    assert 2 * n_ssd_heads == 128 and 2 * MLSTM_HEADS == 32
    seg = lambda i: w_in[:, offs[i]:offs[i + 1]]
    w_main = jnp.concatenate([seg(0), seg(5), seg(6), seg(1), seg(3), seg(4)], axis=1).astype(BF16)
    w_gate = jnp.concatenate([seg(2), seg(7), seg(8), jnp.zeros((d_model, 64), w_in.dtype)],
                             axis=1).astype(BF16)
    gate_bias = jnp.concatenate([dt_bias.reshape(-1), b_i.reshape(-1), b_f.reshape(-1),
                                 jnp.zeros((64,), F32)]).astype(F32).reshape(1, 256)
    cols = {"z": 0, "v": d_ssd, "o": d_ssd + d_ml, "xbc": d_ssd + 2 * d_ml}
    cols["q"] = cols["xbc"] + xbc_w
    cols["k"] = cols["q"] + MLSTM_HEADS * dk
    cols["dk"], cols["dv"], cols["d_ssd"] = dk, dv, d_ssd
    return dict(
        norm1_w=norm1_w, w_main=w_main, w_gate=w_gate, gate_bias=gate_bias,
        alog=a_log.reshape(1, 128).astype(F32),
        conv_w=conv_w.astype(F32), conv_b=conv_b.reshape(1, -1).astype(F32),
        dexp=jnp.broadcast_to(jnp.repeat(d_skip.astype(F32), SSD_HEAD_DIM)[:, None], (d_ssd, 128)),
        ssd_norm_w=ssd_norm_w.reshape(1, -1).astype(F32),
        mlstm_norm_w=mlstm_norm_w.reshape(1, -1).astype(F32),
        w_out1=w_out[:d_ssd].astype(BF16), w_out2=w_out[d_ssd:].astype(BF16),
        norm2_w=norm2_w, w_up=w_up.astype(BF16), w_down=w_down.astype(BF16), cols=cols)


def _layer(x, p, batch, seq_len):
    cols = p["cols"]
    d_ssd = cols["d_ssd"]
    h = rmsnorm_rows(x, p["norm1_w"], BF16)
    proj = matmul(h, p["w_main"], F32)
    ssd, ssdT, ml, mlT = gates(h, p["w_gate"], p["gate_bias"], p["alog"])
    xsT = conv_silu(proj, cols["xbc"], d_ssd, p["conv_w"][:, :d_ssd], p["conv_b"][:, :d_ssd],
                    seq_len, True)
    bc = conv_silu(proj, cols["xbc"] + d_ssd, 2 * SSD_GROUPS * SSD_STATE, p["conv_w"][:, d_ssd:],
                   p["conv_b"][:, d_ssd:], seq_len, False)
    y_f = ssd_scan(xsT, bc, ssd, ssdT, batch, seq_len, backward=False, dexp=p["dexp"])
    mix1 = ssd_scan(xsT, bc, ssd, ssdT, batch, seq_len, backward=True, y_fwd=y_f,
                    proj=proj, norm_w=p["ssd_norm_w"])
    h_f = mlstm_scan(proj, ml, mlT, batch, seq_len, cols, backward=False)
    mix2 = mlstm_scan(proj, ml, mlT, batch, seq_len, cols, backward=True, h_fwd=h_f,
                      norm_w=p["mlstm_norm_w"])
    x1 = outproj_residual(mix1, mix2, p["w_out1"], p["w_out2"], x)
    h2 = rmsnorm_rows(x1, p["norm2_w"], BF16)
    u = matmul(h2, p["w_up"], BF16, relu2=True)
    return matmul_ksplit_residual(u, p["w_down"], x1)


def _trunk(x, layers, final_norm_w):
    batch, seq_len, d = x.shape
    xf = x.reshape(batch * seq_len, d)
    for p in layers:
        xf = _layer(xf, p, batch, seq_len)
    return rmsnorm_rows(xf, final_norm_w, F32).reshape(batch, seq_len, d)


def kernel(x_prompt, x_sample, norm1_w, w_in, conv_w, conv_b, dt_bias, a_log, d_skip, ssd_norm_w,
           b_i, b_f, mlstm_norm_w, w_out, norm2_w, w_up, w_down, final_norm_w):
    depth = w_in.shape[0]
    layers = [_prep_layer(norm1_w[l], w_in[l], conv_w[l], conv_b[l], dt_bias[l], a_log[l], d_skip[l],
                          ssd_norm_w[l], b_i[l], b_f[l], mlstm_norm_w[l], w_out[l], norm2_w[l],
                          w_up[l], w_down[l]) for l in range(depth)]
    y_prompt = _trunk(x_prompt, layers, final_norm_w)
    y_sample = _trunk(x_sample, layers, final_norm_w)
    return (y_prompt, y_sample)
```

```python
import functools

import jax
import jax.numpy as jnp
from jax import lax
from jax.experimental import pallas as pl
from jax.experimental.pallas import tpu as pltpu

F32 = jnp.float32
BF16 = jnp.bfloat16

CHUNK = 128
EPS = 1e-5
D_CONV = 5
SSD_GROUPS = 8
SSD_HEAD_DIM = 64
SSD_STATE = 128
MLSTM_HEADS = 16
VMEM_LIMIT = 56 * 1024 * 1024


def _cparams(sem):
    return pltpu.CompilerParams(dimension_semantics=sem, vmem_limit_bytes=VMEM_LIMIT)


def _sigmoid(x):
    return 1.0 / (1.0 + jnp.exp(-x))


def _softplus(x):
    return jnp.maximum(x, 0.0) + jnp.log1p(jnp.exp(-jnp.abs(x)))


def _dot(a, b):
    return jnp.dot(a, b, preferred_element_type=F32)


def _dot_nt(a, b):
    return lax.dot_general(a, b, (((1,), (1,)), ((), ())), preferred_element_type=F32)


def _rmsnorm_kernel(x_ref, w_ref, o_ref):
    x = x_ref[...].astype(F32)
    y = x * lax.rsqrt(jnp.mean(x * x, axis=-1, keepdims=True) + EPS)
    o_ref[...] = (y * w_ref[...]).astype(o_ref.dtype)


def rmsnorm_rows(x, w, out_dtype, bm=512):
    m, d = x.shape
    return pl.pallas_call(
        _rmsnorm_kernel,
        grid=(m // bm,),
        in_specs=[pl.BlockSpec((bm, d), lambda i: (i, 0)),
                  pl.BlockSpec((1, d), lambda i: (0, 0))],
        out_specs=pl.BlockSpec((bm, d), lambda i: (i, 0)),
        out_shape=jax.ShapeDtypeStruct((m, d), out_dtype),
        compiler_params=_cparams(("parallel",)),
        name="rmsnorm_rows",
    )(x, w.reshape(1, d).astype(F32))


def _matmul_kernel(a_ref, b_ref, o_ref, *, relu2):
    acc = _dot(a_ref[...], b_ref[...])
    if relu2:
        acc = jnp.maximum(acc, 0.0)
        acc = acc * acc
    o_ref[...] = acc.astype(o_ref.dtype)


def matmul(a, b, out_dtype, bm=1024, bn=1024, relu2=False):
    m, k = a.shape
    n = b.shape[1]
    return pl.pallas_call(
        functools.partial(_matmul_kernel, relu2=relu2),
        grid=(n // bn, m // bm),
        in_specs=[pl.BlockSpec((bm, k), lambda j, i: (i, 0)),
                  pl.BlockSpec((k, bn), lambda j, i: (0, j))],
        out_specs=pl.BlockSpec((bm, bn), lambda j, i: (i, j)),
        out_shape=jax.ShapeDtypeStruct((m, n), out_dtype),
        compiler_params=_cparams(("parallel", "parallel")),
        name="matmul_relu2" if relu2 else "matmul",
    )(a, b)


def _matmul_ksplit_res_kernel(a_ref, b_ref, x_ref, o_ref):
    @pl.when(pl.program_id(2) == 0)
    def _():
        o_ref[...] = x_ref[...] + _dot(a_ref[...], b_ref[...])

    @pl.when(pl.program_id(2) != 0)
    def _():
        o_ref[...] = o_ref[...] + _dot(a_ref[...], b_ref[...])


def matmul_ksplit_residual(a, b, x, bm=1024, bn=1024, bk=2048):
    m, k = a.shape
    n = b.shape[1]
    return pl.pallas_call(
        _matmul_ksplit_res_kernel,
        grid=(n // bn, m // bm, k // bk),
        in_specs=[pl.BlockSpec((bm, bk), lambda j, i, kk: (i, kk)),
                  pl.BlockSpec((bk, bn), lambda j, i, kk: (kk, j)),
                  pl.BlockSpec((bm, bn), lambda j, i, kk: (i, j))],
        out_specs=pl.BlockSpec((bm, bn), lambda j, i, kk: (i, j)),
        out_shape=jax.ShapeDtypeStruct((m, n), F32),
        compiler_params=_cparams(("parallel", "parallel", "arbitrary")),
        name="matmul_ksplit_residual",
    )(a, b, x)


def _split3(x):
    hi = x.astype(BF16)
    r1 = x - hi.astype(F32)
    mid = r1.astype(BF16)
    r2 = r1 - mid.astype(F32)
    return hi, mid, r2.astype(BF16)


def _gates_kernel(h_ref, w_ref, bias_ref, alog_ref, ssd_ref, ssdT_ref, ml_ref, mlT_ref, *, bm):
    raw = _dot(h_ref[...], w_ref[...]) + bias_ref[...]
    dt = _softplus(raw[:, 0:128])
    a = dt * (-jnp.exp(alog_ref[...]))
    t2 = raw[:, 128:256]
    lane = lax.broadcasted_iota(jnp.int32, (CHUNK, 128), 1)
    row_t = lax.broadcasted_iota(jnp.int32, (CHUNK, 128), 0)
    lsig = -_softplus(-t2)
    row_i = lax.broadcasted_iota(jnp.int32, (CHUNK, CHUNK), 0)
    col_i = lax.broadcasted_iota(jnp.int32, (CHUNK, CHUNK), 1)
    lower = (col_i <= row_i).astype(BF16)
    upper = (col_i >= row_i).astype(BF16)
    ones = jnp.ones((CHUNK, CHUNK), BF16)
    ssd_fwd_lane = lane < 64
    ml_fwd_lane = (lane % 32) < 16

    def cums(x, fwd_lane):
        hi, mid, lo = _split3(x)
        cum_f = _dot(lower, hi) + _dot(lower, mid) + _dot(lower, lo)
        cum_b = _dot(upper, hi) + _dot(upper, mid) + _dot(upper, lo)
        tot = _dot(ones, hi) + _dot(ones, mid) + _dot(ones, lo)
        return jnp.where(fwd_lane, cum_f, cum_b), tot

    for c in range(bm // CHUNK):
        sl = slice(c * CHUNK, (c + 1) * CHUNK)
        dt_c = dt[sl]
        acum, tot = cums(a[sl], ssd_fwd_lane)
        ssd_ref[sl, 0:128] = dt_c
        ssd_ref[sl, 128:256] = acum
        ssd_ref[sl, 256:384] = tot
        ssdT_ref[0:128, sl] = dt_c.T
        ssdT_ref[128:256, sl] = acum.T
        ssdT_ref[256:384, sl] = tot.T
        t2_c = t2[sl]
        mcum, mtot = cums(lsig[sl], ml_fwd_lane)
        y = mcum - pltpu.roll(t2_c, 32, 1)
        y_f, y_b = y, y
        for d in (1, 2, 4, 8, 16, 32, 64):
            y_f = jnp.minimum(y_f, jnp.where(row_t >= d, pltpu.roll(y_f, d, 0), jnp.inf))
            y_b = jnp.minimum(y_b, jnp.where(row_t < CHUNK - d, pltpu.roll(y_b, CHUNK - d, 0), jnp.inf))
        rmax = mcum - jnp.where(ml_fwd_lane, y_f, y_b)
        ml_ref[sl, 0:128] = t2_c
        ml_ref[sl, 128:256] = mcum
        ml_ref[sl, 256:384] = mtot
        ml_ref[sl, 384:512] = rmax
        mlT_ref[0:128, sl] = t2_c.T
        mlT_ref[128:256, sl] = mcum.T
        mlT_ref[256:384, sl] = mtot.T


def gates(h, w_gate, bias, alog, bm=512):
    m, d = h.shape
    nat = pl.BlockSpec((bm, 384), lambda i: (i, 0))
    nat4 = pl.BlockSpec((bm, 512), lambda i: (i, 0))
    tr = pl.BlockSpec((384, bm), lambda i: (0, i))
    return pl.pallas_call(
        functools.partial(_gates_kernel, bm=bm),
        grid=(m // bm,),
        in_specs=[pl.BlockSpec((bm, d), lambda i: (i, 0)),
                  pl.BlockSpec((d, 256), lambda i: (0, 0)),
                  pl.BlockSpec((1, 256), lambda i: (0, 0)),
                  pl.BlockSpec((1, 128), lambda i: (0, 0))],
        out_specs=[nat, tr, nat4, tr],
        out_shape=[jax.ShapeDtypeStruct((m, 384), F32), jax.ShapeDtypeStruct((384, m), F32),
                   jax.ShapeDtypeStruct((m, 512), F32), jax.ShapeDtypeStruct((384, m), F32)],
        compiler_params=_cparams(("parallel",)),
        name="gates",
    )(h, w_gate, bias, alog)


def _conv_kernel(cur_ref, prev_ref, next_ref, w_ref, b_ref, o_ref, *, bt, seq_len, transpose_out):
    i = pl.program_id(0)
    cur = cur_ref[...].astype(F32)
    at_start = (i * bt) % seq_len == 0
    at_end = ((i + 1) * bt) % seq_len == 0
    prev = jnp.where(at_start, 0.0, prev_ref[...].astype(F32))
    nxt = jnp.where(at_end, 0.0, next_ref[...].astype(F32))
    row = lax.broadcasted_iota(jnp.int32, cur.shape, 0)
    w = w_ref[...]
    m2 = pltpu.roll(cur, 2, 0)
    m2 = jnp.where(row == 0, prev[6:7], jnp.where(row == 1, prev[7:8], m2))
    m1 = pltpu.roll(cur, 1, 0)
    m1 = jnp.where(row == 0, prev[7:8], m1)
    p1 = pltpu.roll(cur, bt - 1, 0)
    p1 = jnp.where(row == bt - 1, nxt[0:1], p1)
    p2 = pltpu.roll(cur, bt - 2, 0)
    p2 = jnp.where(row == bt - 2, nxt[0:1], jnp.where(row == bt - 1, nxt[1:2], p2))
    out = w[0:1] * m2
    out = out + w[1:2] * m1
    out = out + w[2:3] * cur
    out = out + w[3:4] * p1
    out = out + w[4:5] * p2
    out = out + b_ref[...]
    out = out * _sigmoid(out)
    if transpose_out:
        o_ref[...] = out.T.astype(o_ref.dtype)
    else:
        o_ref[...] = out.astype(o_ref.dtype)


def conv_silu(proj, col0, ncols, conv_w, conv_b, seq_len, transpose_out, bt=512, bc=512):
    m = proj.shape[0]
    cb0 = col0 // bc
    nrb = m // 8
    if transpose_out:
        out_spec = pl.BlockSpec((bc, bt), lambda i, j: (j, i))
        out_shape = jax.ShapeDtypeStruct((ncols, m), BF16)
    else:
        out_spec = pl.BlockSpec((bt, bc), lambda i, j: (i, j))
        out_shape = jax.ShapeDtypeStruct((m, ncols), BF16)
    return pl.pallas_call(
        functools.partial(_conv_kernel, bt=bt, seq_len=seq_len, transpose_out=transpose_out),
        grid=(m // bt, ncols // bc),
        in_specs=[pl.BlockSpec((bt, bc), lambda i, j: (i, cb0 + j)),
                  pl.BlockSpec((8, bc), lambda i, j: (jnp.maximum(i * (bt // 8) - 1, 0), cb0 + j)),
                  pl.BlockSpec((8, bc), lambda i, j: (jnp.minimum((i + 1) * (bt // 8), nrb - 1), cb0 + j)),
                  pl.BlockSpec((D_CONV, bc), lambda i, j: (0, j)),
                  pl.BlockSpec((1, bc), lambda i, j: (0, j))],
        out_specs=out_spec,
        out_shape=out_shape,
        compiler_params=_cparams(("parallel", "parallel")),
        name="conv_silu_t" if transpose_out else "conv_silu",
    )(proj, proj, proj, conv_w, conv_b)


def _ssd_kernel(*refs, backward):
    if backward:
        (xsT_ref, b_ref, c_ref, nat_ref, tr_ref, yf_ref, z_ref, nw_ref, o_ref, s_ref) = refs
    else:
        (xsT_ref, b_ref, c_ref, nat_ref, tr_ref, dexp_ref, o_ref, s_ref) = refs
    G, R, P, N = SSD_GROUPS, 8, SSD_HEAD_DIM, SSD_STATE
    gw = R * P
    h0 = G * R if backward else 0

    @pl.when(pl.program_id(1) == 0)
    def _():
        s_ref[...] = jnp.zeros_like(s_ref)

    s_i = lax.broadcasted_iota(jnp.int32, (CHUNK, CHUNK), 0)
    l_i = lax.broadcasted_iota(jnp.int32, (CHUNK, CHUNK), 1)
    mask = (l_i <= s_i) if backward else (l_i >= s_i)
    for g in range(G):
        gs = slice(g * gw, (g + 1) * gw)
        hg = h0 + g * R
        bm = b_ref[:, g * N:(g + 1) * N]
        cm = c_ref[:, g * N:(g + 1) * N]
        dt = tr_ref[hg:hg + R, :]
        acum = tr_ref[128 + hg:128 + hg + R, :]
        tot = tr_ref[256 + hg:256 + hg + R, :]
        cbT = _dot_nt(bm, cm)
        s_old = s_ref[gs, :]
        yoffT = _dot_nt(s_old.astype(BF16), cm)
        e_acum = jnp.exp(acum)
        dte = jnp.exp(tot - acum)
        e_tot = jnp.exp(tot)
        y_pieces = []
        xd_pieces = []
        for r in range(R):
            hs = slice(g * gw + r * P, g * gw + (r + 1) * P)
            xr = xsT_ref[hs, :].astype(F32)
            xdt = xr * dt[r:r + 1, :]
            col = nat_ref[:, 128 + hg + r:128 + hg + r + 1]
            seg = acum[r:r + 1, :] - col
            dec = jnp.exp(jnp.where(mask, seg, -jnp.inf))
            mt = (cbT * dec).astype(BF16)
            y_r = _dot(xdt.astype(BF16), mt) + yoffT[r * P:(r + 1) * P, :] * e_acum[r:r + 1, :]
            if not backward:
                y_r = y_r + dexp_ref[hs, :] * xr
            y_pieces.append(y_r)
            xd_pieces.append((xdt * dte[r:r + 1, :]).astype(BF16))
        yT = jnp.concatenate(y_pieces, axis=0)
        upd = _dot(jnp.concatenate(xd_pieces, axis=0), bm)
        for r in range(R):
            hs = slice(r * P, (r + 1) * P)
            s_ref[g * gw + r * P:g * gw + (r + 1) * P, :] = s_old[hs, :] * e_tot[r:r + 1, :] + upd[hs, :]
        y = yT.T
        if backward:
            y = y + yf_ref[:, gs]
            z = z_ref[:, gs].astype(F32)
            y = y * (z * _sigmoid(z))
            y = y * lax.rsqrt(jnp.mean(y * y, axis=-1, keepdims=True) + EPS)
            o_ref[:, gs] = (y * nw_ref[:, gs]).astype(o_ref.dtype)
        else:
            o_ref[:, gs] = y


def ssd_scan(xsT, bc, ssd, ssdT, batch, seq_len, *, backward, dexp=None,
             y_fwd=None, proj=None, norm_w=None):
    m = xsT.shape[1]
    nc = seq_len // CHUNK
    G = SSD_GROUPS
    d_ssd = xsT.shape[0]

    def cg(b, c):
        return b * nc + ((nc - 1 - c) if backward else c)

    in_specs = [
        pl.BlockSpec((d_ssd, CHUNK), lambda b, c: (0, cg(b, c))),
        pl.BlockSpec((CHUNK, G * SSD_STATE), lambda b, c: (cg(b, c), 0)),
        pl.BlockSpec((CHUNK, G * SSD_STATE), lambda b, c: (cg(b, c), 1)),
        pl.BlockSpec((CHUNK, 384), lambda b, c: (cg(b, c), 0)),
        pl.BlockSpec((384, CHUNK), lambda b, c: (0, cg(b, c))),
    ]
    args = [xsT, bc, bc, ssd, ssdT]
    if backward:
        in_specs += [
            pl.BlockSpec((CHUNK, d_ssd), lambda b, c: (cg(b, c), 0)),
            pl.BlockSpec((CHUNK, d_ssd), lambda b, c: (cg(b, c), 0)),
            pl.BlockSpec((1, d_ssd), lambda b, c: (0, 0)),
        ]
        args += [y_fwd, proj, norm_w]
        out_dtype = BF16
    else:
        in_specs += [pl.BlockSpec((d_ssd, 128), lambda b, c: (0, 0))]
        args += [dexp]
        out_dtype = F32
    return pl.pallas_call(
        functools.partial(_ssd_kernel, backward=backward),
        grid=(batch, nc),
        in_specs=in_specs,
        out_specs=pl.BlockSpec((CHUNK, d_ssd), lambda b, c: (cg(b, c), 0)),
        out_shape=jax.ShapeDtypeStruct((m, d_ssd), out_dtype),
        scratch_shapes=[pltpu.VMEM((d_ssd, SSD_STATE), F32)],
        compiler_params=_cparams(("parallel", "arbitrary")),
        name="ssd_bwd" if backward else "ssd_fwd",
    )(*args)


def _mlstm_kernel(*refs, backward, dk, dv, heads_per_group):
    if backward:
        (q_ref, k_ref, v_ref, nat_ref, tr_ref, hf_ref, og_ref, nw_ref, o_ref, c_ref, m_ref) = refs
    else:
        (q_ref, k_ref, v_ref, nat_ref, tr_ref, o_ref, c_ref, m_ref) = refs
    H = MLSTM_HEADS

    @pl.when(pl.program_id(1) == 0)
    def _():
        c_ref[...] = jnp.zeros_like(c_ref)
        m_ref[...] = jnp.zeros_like(m_ref)

    t_i = lax.broadcasted_iota(jnp.int32, (CHUNK, CHUNK), 0)
    s_i = lax.broadcasted_iota(jnp.int32, (CHUNK, CHUNK), 1)
    mask = (s_i >= t_i) if backward else (s_i <= t_i)
    ones_blk = jnp.ones((CHUNK, 128), BF16)
    scale = dk ** -0.5
    dense = (CHUNK, CHUNK)

    for g0 in range(0, H, heads_per_group):
        hs = list(range(g0, g0 + heads_per_group))
        lane_of = {h: h + (H if backward else 0) for h in hs}
        st = {h: {} for h in hs}
        for h in hs:
            d, hh = st[h], lane_of[h]
            li_row = tr_ref[hh:hh + 1, :]
            cum_row = tr_ref[160 + hh:161 + hh, :]
            d["tot"] = tr_ref[288 + hh:289 + hh, :]
            d["base_row"] = cum_row - li_row
            grow = d["tot"] - d["base_row"]
            d["m_in"] = m_ref[h, 0:1, :]
            m_loc = jnp.broadcast_to(jnp.max(grow, axis=1, keepdims=True), (1, CHUNK))
            d["m_new"] = jnp.maximum(d["tot"] + d["m_in"], m_loc)
            d["w_row"] = jnp.exp(grow - d["m_new"])
            d["cum_d"] = jnp.broadcast_to(nat_ref[:, 160 + hh:161 + hh], dense)
            d["rmax_d"] = jnp.broadcast_to(nat_ref[:, 416 + hh:417 + hh], dense)
            d["v_aug"] = jnp.concatenate([v_ref[:, h * dv:(h + 1) * dv].astype(BF16), ones_blk], axis=1)
        for h in hs:
            d = st[h]
            k = k_ref[:, h * dk:(h + 1) * dk].astype(F32)
            d["kb"] = k.astype(BF16)
            d["kwT"] = (k.T * d["w_row"]).astype(BF16)
            d["qs"] = (q_ref[:, h * dk:(h + 1) * dk].astype(F32) * scale).astype(BF16)
        for h in hs:
            d = st[h]
            d["c_loc"] = _dot(d["kwT"], d["v_aug"])
            d["sqk"] = _dot_nt(d["qs"], d["kb"])
            d["c_in"] = c_ref[h]
            d["qc"] = _dot(d["qs"], d["c_in"].astype(BF16))
        for h in hs:
            d = st[h]
            dlog = jnp.where(mask, d["cum_d"] - d["base_row"], -jnp.inf)
            inter = d["cum_d"] + d["m_in"]
            d["m_t"] = jnp.maximum(d["rmax_d"], inter)
            d["pm"] = (jnp.exp(dlog - d["m_t"]) * d["sqk"]).astype(BF16)
            d["a_inter"] = jnp.exp(inter - d["m_t"])
        for h in hs:
            d = st[h]
            a3 = jnp.concatenate([d["a_inter"]] * (dv // 128 + 1), axis=1)
            num = _dot(d["pm"], d["v_aug"]) + d["qc"] * a3
            den = num[:, dv:dv + 128]
            inv = 1.0 / jnp.maximum(jnp.abs(den), jnp.exp(-d["m_t"]))
            d["hout"] = num[:, 0:dv] * jnp.concatenate([inv] * (dv // 128), axis=1)
        for h in hs:
            d = st[h]
            a_prev = jnp.exp(d["tot"] + d["m_in"] - d["m_new"])
            a_prev3 = jnp.concatenate([a_prev] * (dv // 128 + 1), axis=1)
            c_ref[h] = a_prev3 * d["c_in"] + d["c_loc"]
            m_ref[h] = jnp.broadcast_to(d["m_new"], (8, 128))
        for h in hs:
            hout = st[h]["hout"]
            vs = slice(h * dv, (h + 1) * dv)
            if backward:
                hout = hout + hf_ref[:, vs]
                hout = hout * lax.rsqrt(jnp.mean(hout * hout, axis=-1, keepdims=True) + EPS)
                hout = hout * nw_ref[:, vs]
                o_ref[:, vs] = (_sigmoid(og_ref[:, vs].astype(F32)) * hout).astype(o_ref.dtype)
            else:
                o_ref[:, vs] = hout


def mlstm_scan(proj, ml, mlT, batch, seq_len, cols, *, backward, h_fwd=None, norm_w=None):
    m = proj.shape[0]
    nc = seq_len // CHUNK
    H = MLSTM_HEADS
    dk, dv = cols["dk"], cols["dv"]
    qw, vw = H * dk, H * dv
    qb, kb, vb, ob = cols["q"] // qw, cols["k"] // qw, cols["v"] // vw, cols["o"] // vw
    assert qb * qw == cols["q"] and kb * qw == cols["k"] and vb * vw == cols["v"] and ob * vw == cols["o"]

    def cg(b, c):
        return b * nc + ((nc - 1 - c) if backward else c)

    in_specs = [
        pl.BlockSpec((CHUNK, qw), lambda b, c: (cg(b, c), qb)),
        pl.BlockSpec((CHUNK, qw), lambda b, c: (cg(b, c), kb)),
        pl.BlockSpec((CHUNK, vw), lambda b, c: (cg(b, c), vb)),
        pl.BlockSpec((CHUNK, 512), lambda b, c: (cg(b, c), 0)),
        pl.BlockSpec((384, CHUNK), lambda b, c: (0, cg(b, c))),
    ]
    args = [proj, proj, proj, ml, mlT]
    if backward:
        in_specs += [
            pl.BlockSpec((CHUNK, vw), lambda b, c: (cg(b, c), 0)),
            pl.BlockSpec((CHUNK, vw), lambda b, c: (cg(b, c), ob)),
            pl.BlockSpec((1, vw), lambda b, c: (0, 0)),
        ]
        args += [h_fwd, proj, norm_w]
        out_dtype = BF16
    else:
        out_dtype = F32
    return pl.pallas_call(
        functools.partial(_mlstm_kernel, backward=backward, dk=dk, dv=dv, heads_per_group=8),
        grid=(batch, nc),
        in_specs=in_specs,
        out_specs=pl.BlockSpec((CHUNK, vw), lambda b, c: (cg(b, c), 0)),
        out_shape=jax.ShapeDtypeStruct((m, vw), out_dtype),
        scratch_shapes=[pltpu.VMEM((H, dk, dv + 128), F32), pltpu.VMEM((H, 8, 128), F32)],
        compiler_params=_cparams(("parallel", "arbitrary")),
        name="mlstm_bwd" if backward else "mlstm_fwd",
    )(*args)


def _outproj_kernel(a1_ref, a2_ref, w1_ref, w2_ref, x_ref, o_ref):
    acc = _dot(a1_ref[...], w1_ref[...]) + _dot(a2_ref[...], w2_ref[...])
    o_ref[...] = x_ref[...] + acc


def outproj_residual(a1, a2, w1, w2, x, bm=512, bn=512):
    m, k = a1.shape
    n = w1.shape[1]
    return pl.pallas_call(
        _outproj_kernel,
        grid=(n // bn, m // bm),
        in_specs=[pl.BlockSpec((bm, k), lambda j, i: (i, 0)),
                  pl.BlockSpec((bm, k), lambda j, i: (i, 0)),
                  pl.BlockSpec((k, bn), lambda j, i: (0, j)),
                  pl.BlockSpec((k, bn), lambda j, i: (0, j)),
                  pl.BlockSpec((bm, bn), lambda j, i: (i, j))],
        out_specs=pl.BlockSpec((bm, bn), lambda j, i: (i, j)),
        out_shape=jax.ShapeDtypeStruct((m, n), F32),
        compiler_params=_cparams(("parallel", "parallel")),
        name="outproj_residual",
    )(a1, a2, w1, w2, x)


def _prep_layer(norm1_w, w_in, conv_w, conv_b, dt_bias, a_log, d_skip, ssd_norm_w, b_i, b_f,
                mlstm_norm_w, w_out, norm2_w, w_up, w_down):
    d_model = w_in.shape[0]
    d_mix = w_out.shape[0]
    d_ssd = d_mix // 2
    d_ml = d_mix - d_ssd
    n_ssd_heads = d_ssd // SSD_HEAD_DIM
    xbc_w = d_ssd + 2 * SSD_GROUPS * SSD_STATE
    dv = d_ml // MLSTM_HEADS
    dk = dv // 2
    widths = (d_ssd, xbc_w, 2 * n_ssd_heads, MLSTM_HEADS * dk, MLSTM_HEADS * dk, d_ml, d_ml,
              2 * MLSTM_HEADS, 2 * MLSTM_HEADS)
    offs = [0]
    for wd in widths:
        offs.append(offs[-1] + wd)
    assert offs[-1] == w_in.shape[1]
    assert 2 * n_ssd_heads == 128 and 2 * MLSTM_HEADS == 32
    seg = lambda i: w_in[:, offs[i]:offs[i + 1]]
    w_a = w_in[:, offs[0]:offs[2]].astype(BF16)
    w_b = w_in[:, offs[3]:offs[7]].astype(BF16)
    w_gate = jnp.concatenate([seg(2), seg(7), seg(8), jnp.zeros((d_model, 64), w_in.dtype)],
                             axis=1).astype(BF16)
    gate_bias = jnp.concatenate([dt_bias.reshape(-1), b_i.reshape(-1), b_f.reshape(-1),
                                 jnp.zeros((64,), F32)]).astype(F32).reshape(1, 256)
    cols = {"z": 0, "xbc": d_ssd, "q": 0, "k": MLSTM_HEADS * dk, "v": 2 * MLSTM_HEADS * dk}
    cols["o"] = cols["v"] + d_ml
    cols["dk"], cols["dv"], cols["d_ssd"] = dk, dv, d_ssd
    return dict(
        norm1_w=norm1_w, w_a=w_a, w_b=w_b, w_gate=w_gate, gate_bias=gate_bias,
        alog=a_log.reshape(1, 128).astype(F32),
        conv_w=conv_w.astype(F32), conv_b=conv_b.reshape(1, -1).astype(F32),
        dexp=jnp.broadcast_to(jnp.repeat(d_skip.astype(F32), SSD_HEAD_DIM)[:, None], (d_ssd, 128)),
        ssd_norm_w=ssd_norm_w.reshape(1, -1).astype(F32),
        mlstm_norm_w=mlstm_norm_w.reshape(1, -1).astype(F32),
        w_out1=w_out[:d_ssd].astype(BF16), w_out2=w_out[d_ssd:].astype(BF16),
        norm2_w=norm2_w, w_up=w_up.astype(BF16), w_down=w_down.astype(BF16), cols=cols)


def _layer(x, p, batch, seq_len):
    cols = p["cols"]
    d_ssd = cols["d_ssd"]
    h = rmsnorm_rows(x, p["norm1_w"], BF16)
    proj_a = matmul(h, p["w_a"], F32)
    proj_b = matmul(h, p["w_b"], F32)
    ssd, ssdT, ml, mlT = gates(h, p["w_gate"], p["gate_bias"], p["alog"])
    xsT = conv_silu(proj_a, cols["xbc"], d_ssd, p["conv_w"][:, :d_ssd], p["conv_b"][:, :d_ssd],
                    seq_len, True)
    bc = conv_silu(proj_a, cols["xbc"] + d_ssd, 2 * SSD_GROUPS * SSD_STATE, p["conv_w"][:, d_ssd:],
                   p["conv_b"][:, d_ssd:], seq_len, False)
    y_f = ssd_scan(xsT, bc, ssd, ssdT, batch, seq_len, backward=False, dexp=p["dexp"])
    mix1 = ssd_scan(xsT, bc, ssd, ssdT, batch, seq_len, backward=True, y_fwd=y_f,
                    proj=proj_a, norm_w=p["ssd_norm_w"])
    h_f = mlstm_scan(proj_b, ml, mlT, batch, seq_len, cols, backward=False)
    mix2 = mlstm_scan(proj_b, ml, mlT, batch, seq_len, cols, backward=True, h_fwd=h_f,
                      norm_w=p["mlstm_norm_w"])
    x1 = outproj_residual(mix1, mix2, p["w_out1"], p["w_out2"], x)
    h2 = rmsnorm_rows(x1, p["norm2_w"], BF16)
    u = matmul(h2, p["w_up"], BF16, relu2=True)
    return matmul_ksplit_residual(u, p["w_down"], x1)


def _trunk(x, layers, final_norm_w):
    batch, seq_len, d = x.shape
    xf = x.reshape(batch * seq_len, d)
    for p in layers:
        xf = _layer(xf, p, batch, seq_len)
    return rmsnorm_rows(xf, final_norm_w, F32).reshape(batch, seq_len, d)


def kernel(x_prompt, x_sample, norm1_w, w_in, conv_w, conv_b, dt_bias, a_log, d_skip, ssd_norm_w,
           b_i, b_f, mlstm_norm_w, w_out, norm2_w, w_up, w_down, final_norm_w):
    depth = w_in.shape[0]
    layers = [_prep_layer(norm1_w[l], w_in[l], conv_w[l], conv_b[l], dt_bias[l], a_log[l], d_skip[l],
                          ssd_norm_w[l], b_i[l], b_f[l], mlstm_norm_w[l], w_out[l], norm2_w[l],
                          w_up[l], w_down[l]) for l in range(depth)]
    y_prompt = _trunk(x_prompt, layers, final_norm_w)
    y_sample = _trunk(x_sample, layers, final_norm_w)
    return (y_prompt, y_sample)
```

```python
import functools

import jax
import jax.numpy as jnp
from jax import lax
from jax.experimental import pallas as pl
from jax.experimental.pallas import tpu as pltpu

F32 = jnp.float32
BF16 = jnp.bfloat16

CHUNK = 128
CHUNKS_PER_STEP = 2
EPS = 1e-5
D_CONV = 5
SSD_GROUPS = 8
SSD_HEAD_DIM = 64
SSD_STATE = 128
MLSTM_HEADS = 16
VMEM_LIMIT = 56 * 1024 * 1024


def _cparams(sem):
    return pltpu.CompilerParams(dimension_semantics=sem, vmem_limit_bytes=VMEM_LIMIT)


def _sigmoid(x):
    return 1.0 / (1.0 + jnp.exp(-x))


def _softplus(x):
    return jnp.maximum(x, 0.0) + jnp.log1p(jnp.exp(-jnp.abs(x)))


def _dot(a, b):
    return jnp.dot(a, b, preferred_element_type=F32)


def _dot_nt(a, b):
    return lax.dot_general(a, b, (((1,), (1,)), ((), ())), preferred_element_type=F32)


def _rmsnorm_kernel(x_ref, w_ref, o_ref):
    x = x_ref[...].astype(F32)
    y = x * lax.rsqrt(jnp.mean(x * x, axis=-1, keepdims=True) + EPS)
    o_ref[...] = (y * w_ref[...]).astype(o_ref.dtype)


def rmsnorm_rows(x, w, out_dtype, bm=512):
    m, d = x.shape
    return pl.pallas_call(
        _rmsnorm_kernel,
        grid=(m // bm,),
        in_specs=[pl.BlockSpec((bm, d), lambda i: (i, 0)),
                  pl.BlockSpec((1, d), lambda i: (0, 0))],
        out_specs=pl.BlockSpec((bm, d), lambda i: (i, 0)),
        out_shape=jax.ShapeDtypeStruct((m, d), out_dtype),
        compiler_params=_cparams(("parallel",)),
        name="rmsnorm_rows",
    )(x, w.reshape(1, d).astype(F32))


def _matmul_kernel(*refs, relu2, row_ssq_dim):
    if row_ssq_dim:
        a_ref, b_ref, ssq_ref, o_ref = refs
    else:
        a_ref, b_ref, o_ref = refs
    acc = _dot(a_ref[...], b_ref[...])
    if relu2:
        acc = jnp.maximum(acc, 0.0)
        acc = acc * acc
    if row_ssq_dim:
        ssq = ssq_ref[...]
        tot = ssq[:, 0:128]
        for part in range(1, ssq.shape[1] // 128):
            tot = tot + ssq[:, part * 128:(part + 1) * 128]
        r2 = 1.0 / (tot * (1.0 / row_ssq_dim) + EPS)
        acc = acc * jnp.concatenate([r2] * (acc.shape[1] // 128), axis=1)
    o_ref[...] = acc.astype(o_ref.dtype)


def matmul(a, b, out_dtype, bm=1024, bn=1024, relu2=False, col0=0, n=None, row_ssq=None):
    m, k = a.shape
    n = b.shape[1] if n is None else n
    in_specs = [pl.BlockSpec((bm, k), lambda j, i: (i, 0)),
                pl.BlockSpec((pl.Element(k), pl.Element(bn)),
                             lambda j, i: (0, pl.multiple_of(col0 + j * bn, 128)))]
    args = [a, b]
    if row_ssq is not None:
        assert relu2
        in_specs.append(pl.BlockSpec((bm, row_ssq.shape[1]), lambda j, i: (i, 0)))
        args.append(row_ssq)
    return pl.pallas_call(
        functools.partial(_matmul_kernel, relu2=relu2, row_ssq_dim=k if row_ssq is not None else 0),
        grid=(n // bn, m // bm),
        in_specs=in_specs,
        out_specs=pl.BlockSpec((bm, bn), lambda j, i: (i, j)),
        out_shape=jax.ShapeDtypeStruct((m, n), out_dtype),
        compiler_params=_cparams(("parallel", "parallel")),
        name="matmul_relu2" if relu2 else "matmul",
    )(*args)


def _matmul_ksplit_res_kernel(a_ref, b_ref, x_ref, o_ref):
    @pl.when(pl.program_id(2) == 0)
    def _():
        o_ref[...] = x_ref[...] + _dot(a_ref[...], b_ref[...])

    @pl.when(pl.program_id(2) != 0)
    def _():
        o_ref[...] = o_ref[...] + _dot(a_ref[...], b_ref[...])


def matmul_ksplit_residual(a, b, x, bm=1024, bn=1024, bk=2048):
    m, k = a.shape
    n = b.shape[1]
    return pl.pallas_call(
        _matmul_ksplit_res_kernel,
        grid=(n // bn, m // bm, k // bk),
        in_specs=[pl.BlockSpec((bm, bk), lambda j, i, kk: (i, kk)),
                  pl.BlockSpec((bk, bn), lambda j, i, kk: (kk, j)),
                  pl.BlockSpec((bm, bn), lambda j, i, kk: (i, j))],
        out_specs=pl.BlockSpec((bm, bn), lambda j, i, kk: (i, j)),
        out_shape=jax.ShapeDtypeStruct((m, n), F32),
        compiler_params=_cparams(("parallel", "parallel", "arbitrary")),
        name="matmul_ksplit_residual",
    )(a, b, x)


def _split3(x):
    hi = x.astype(BF16)
    r1 = x - hi.astype(F32)
    mid = r1.astype(BF16)
    r2 = r1 - mid.astype(F32)
    return hi, mid, r2.astype(BF16)


def _gates_kernel(h_ref, w_ref, bias_ref, alog_ref, ssd_ref, ssdT_ref, ml_ref, mlT_ref, *, bm):
    raw = _dot(h_ref[...], w_ref[...]) + bias_ref[...]
    dt = _softplus(raw[:, 0:128])
    a = dt * (-jnp.exp(alog_ref[...]))
    t2 = raw[:, 128:256]
    lane = lax.broadcasted_iota(jnp.int32, (CHUNK, 128), 1)
    row_t = lax.broadcasted_iota(jnp.int32, (CHUNK, 128), 0)
    lsig = -_softplus(-t2)
    row_i = lax.broadcasted_iota(jnp.int32, (CHUNK, CHUNK), 0)
    col_i = lax.broadcasted_iota(jnp.int32, (CHUNK, CHUNK), 1)
    lower = (col_i <= row_i).astype(BF16)
    upper = (col_i >= row_i).astype(BF16)
    ones = jnp.ones((CHUNK, CHUNK), BF16)
    ssd_fwd_lane = lane < 64
    ml_fwd_lane = (lane % 32) < 16

    def cums(x, fwd_lane):
        hi, mid, lo = _split3(x)
        cum_f = _dot(lower, hi) + _dot(lower, mid) + _dot(lower, lo)
        cum_b = _dot(upper, hi) + _dot(upper, mid) + _dot(upper, lo)
        tot = _dot(ones, hi) + _dot(ones, mid) + _dot(ones, lo)
        return jnp.where(fwd_lane, cum_f, cum_b), tot

    for c in range(bm // CHUNK):
        sl = slice(c * CHUNK, (c + 1) * CHUNK)
        dt_c = dt[sl]
        acum, tot = cums(a[sl], ssd_fwd_lane)
        ssd_ref[sl, 0:128] = dt_c
        ssd_ref[sl, 128:256] = acum
        ssd_ref[sl, 256:384] = tot
        ssdT_ref[0:128, sl] = dt_c.T
        ssdT_ref[128:256, sl] = acum.T
        ssdT_ref[256:384, sl] = tot.T
        t2_c = t2[sl]
        mcum, mtot = cums(lsig[sl], ml_fwd_lane)
        y = mcum - pltpu.roll(t2_c, 32, 1)
        y_f, y_b = y, y
        for d in (1, 2, 4, 8, 16, 32, 64):
            y_f = jnp.minimum(y_f, jnp.where(row_t >= d, pltpu.roll(y_f, d, 0), jnp.inf))
            y_b = jnp.minimum(y_b, jnp.where(row_t < CHUNK - d, pltpu.roll(y_b, CHUNK - d, 0), jnp.inf))
        rmax = mcum - jnp.where(ml_fwd_lane, y_f, y_b)
        ml_ref[sl, 0:128] = t2_c
        ml_ref[sl, 128:256] = mcum
        ml_ref[sl, 256:384] = mtot
        ml_ref[sl, 384:512] = rmax
        mlT_ref[0:128, sl] = t2_c.T
        mlT_ref[128:256, sl] = mcum.T
        mlT_ref[256:384, sl] = mtot.T


def gates(h, w_gate, bias, alog, bm=512):
    m, d = h.shape
    nat = pl.BlockSpec((bm, 384), lambda i: (i, 0))
    nat4 = pl.BlockSpec((bm, 512), lambda i: (i, 0))
    tr = pl.BlockSpec((384, bm), lambda i: (0, i))
    return pl.pallas_call(
        functools.partial(_gates_kernel, bm=bm),
        grid=(m // bm,),
        in_specs=[pl.BlockSpec((bm, d), lambda i: (i, 0)),
                  pl.BlockSpec((d, 256), lambda i: (0, 0)),
                  pl.BlockSpec((1, 256), lambda i: (0, 0)),
                  pl.BlockSpec((1, 128), lambda i: (0, 0))],
        out_specs=[nat, tr, nat4, tr],
        out_shape=[jax.ShapeDtypeStruct((m, 384), F32), jax.ShapeDtypeStruct((384, m), F32),
                   jax.ShapeDtypeStruct((m, 512), F32), jax.ShapeDtypeStruct((384, m), F32)],
        compiler_params=_cparams(("parallel",)),
        name="gates",
    )(h, w_gate, bias, alog)


def _conv_kernel(cur_ref, prev_ref, next_ref, w_ref, b_ref, o_ref, *, bt, seq_len, transpose_out):
    i = pl.program_id(0)
    cur = cur_ref[...].astype(F32)
    at_start = (i * bt) % seq_len == 0
    at_end = ((i + 1) * bt) % seq_len == 0
    prev = jnp.where(at_start, 0.0, prev_ref[...].astype(F32))
    nxt = jnp.where(at_end, 0.0, next_ref[...].astype(F32))
    row = lax.broadcasted_iota(jnp.int32, cur.shape, 0)
    w = w_ref[...]
    m2 = pltpu.roll(cur, 2, 0)
    m2 = jnp.where(row == 0, prev[6:7], jnp.where(row == 1, prev[7:8], m2))
    m1 = pltpu.roll(cur, 1, 0)
    m1 = jnp.where(row == 0, prev[7:8], m1)
    p1 = pltpu.roll(cur, bt - 1, 0)
    p1 = jnp.where(row == bt - 1, nxt[0:1], p1)
    p2 = pltpu.roll(cur, bt - 2, 0)
    p2 = jnp.where(row == bt - 2, nxt[0:1], jnp.where(row == bt - 1, nxt[1:2], p2))
    out = w[0:1] * m2
    out = out + w[1:2] * m1
    out = out + w[2:3] * cur
    out = out + w[3:4] * p1
    out = out + w[4:5] * p2
    out = out + b_ref[...]
    out = out * _sigmoid(out)
    if transpose_out:
        o_ref[...] = out.T.astype(o_ref.dtype)
    else:
        o_ref[...] = out.astype(o_ref.dtype)


def conv_silu(proj, col0, ncols, conv_w, conv_b, seq_len, transpose_out, bt=512, bc=512):
    m = proj.shape[0]
    cb0 = col0 // bc
    nrb = m // 8
    if transpose_out:
        out_spec = pl.BlockSpec((bc, bt), lambda i, j: (j, i))
        out_shape = jax.ShapeDtypeStruct((ncols, m), BF16)
    else:
        out_spec = pl.BlockSpec((bt, bc), lambda i, j: (i, j))
        out_shape = jax.ShapeDtypeStruct((m, ncols), BF16)
    return pl.pallas_call(
        functools.partial(_conv_kernel, bt=bt, seq_len=seq_len, transpose_out=transpose_out),
        grid=(m // bt, ncols // bc),
        in_specs=[pl.BlockSpec((bt, bc), lambda i, j: (i, cb0 + j)),
                  pl.BlockSpec((8, bc), lambda i, j: (jnp.maximum(i * (bt // 8) - 1, 0), cb0 + j)),
                  pl.BlockSpec((8, bc), lambda i, j: (jnp.minimum((i + 1) * (bt // 8), nrb - 1), cb0 + j)),
                  pl.BlockSpec((D_CONV, bc), lambda i, j: (0, j)),
                  pl.BlockSpec((1, bc), lambda i, j: (0, j))],
        out_specs=out_spec,
        out_shape=out_shape,
        compiler_params=_cparams(("parallel", "parallel")),
        name="conv_silu_t" if transpose_out else "conv_silu",
    )(proj, proj, proj, conv_w, conv_b)


def _ssd_kernel(*refs, backward, cb):
    if backward:
        (xsT_ref, b_ref, c_ref, nat_ref, tr_ref, yf_ref, z_ref, nw_ref, o_ref, s_ref) = refs
    else:
        (xsT_ref, b_ref, c_ref, nat_ref, tr_ref, dexp_ref, o_ref, s_ref) = refs
    G, R, P, N = SSD_GROUPS, 8, SSD_HEAD_DIM, SSD_STATE
    gw = R * P
    h0 = G * R if backward else 0

    @pl.when(pl.program_id(1) == 0)
    def _():
        s_ref[...] = jnp.zeros_like(s_ref)

    s_i = lax.broadcasted_iota(jnp.int32, (CHUNK, CHUNK), 0)
    l_i = lax.broadcasted_iota(jnp.int32, (CHUNK, CHUNK), 1)
    mask = (l_i <= s_i) if backward else (l_i >= s_i)
    chunk_order = range(cb - 1, -1, -1) if backward else range(cb)
    for j, g in [(j, g) for j in chunk_order for g in range(G)]:
        ts = slice(j * CHUNK, (j + 1) * CHUNK)
        gs = slice(g * gw, (g + 1) * gw)
        hg = h0 + g * R
        bm = b_ref[ts, g * N:(g + 1) * N]
        cm = c_ref[ts, g * N:(g + 1) * N]
        dt = tr_ref[hg:hg + R, ts]
        acum = tr_ref[128 + hg:128 + hg + R, ts]
        tot = tr_ref[256 + hg:256 + hg + R, ts]
        cbT = _dot_nt(bm, cm)
        s_old = s_ref[gs, :]
        yoffT = _dot_nt(s_old.astype(BF16), cm)
        e_acum = jnp.exp(acum)
        dte = jnp.exp(tot - acum)
        e_tot = jnp.exp(tot)
        y_pieces = []
        xd_pieces = []
        for r in range(R):
            hs = slice(g * gw + r * P, g * gw + (r + 1) * P)
            xr = xsT_ref[hs, ts].astype(F32)
            xdt = xr * dt[r:r + 1, :]
            col = nat_ref[ts, 128 + hg + r:128 + hg + r + 1]
            seg = acum[r:r + 1, :] - col
            dec = jnp.exp(jnp.where(mask, seg, -jnp.inf))
            mt = (cbT * dec).astype(BF16)
            y_r = _dot(xdt.astype(BF16), mt) + yoffT[r * P:(r + 1) * P, :] * e_acum[r:r + 1, :]
            if not backward:
                y_r = y_r + dexp_ref[hs, :] * xr
            y_pieces.append(y_r)
            xd_pieces.append((xdt * dte[r:r + 1, :]).astype(BF16))
        yT = jnp.concatenate(y_pieces, axis=0)
        upd = _dot(jnp.concatenate(xd_pieces, axis=0), bm)
        for r in range(R):
            hs = slice(r * P, (r + 1) * P)
            s_ref[g * gw + r * P:g * gw + (r + 1) * P, :] = s_old[hs, :] * e_tot[r:r + 1, :] + upd[hs, :]
        y = yT.T
        if backward:
            y = y + yf_ref[ts, gs]
            z = z_ref[ts, gs].astype(F32)
            y = y * (z * _sigmoid(z))
            y = y * lax.rsqrt(jnp.mean(y * y, axis=-1, keepdims=True) + EPS)
            o_ref[ts, gs] = (y * nw_ref[:, gs]).astype(o_ref.dtype)
        else:
            o_ref[ts, gs] = y


def ssd_scan(xsT, bc, ssd, ssdT, batch, seq_len, *, backward, dexp=None,
             y_fwd=None, proj=None, norm_w=None):
    m = xsT.shape[1]
    cb = CHUNKS_PER_STEP
    tb = CHUNK * cb
    nc = seq_len // tb
    G = SSD_GROUPS
    d_ssd = xsT.shape[0]

    def cg(b, c):
        return b * nc + ((nc - 1 - c) if backward else c)

    in_specs = [
        pl.BlockSpec((d_ssd, tb), lambda b, c: (0, cg(b, c))),
        pl.BlockSpec((tb, G * SSD_STATE), lambda b, c: (cg(b, c), 0)),
        pl.BlockSpec((tb, G * SSD_STATE), lambda b, c: (cg(b, c), 1)),
        pl.BlockSpec((tb, 384), lambda b, c: (cg(b, c), 0)),
        pl.BlockSpec((384, tb), lambda b, c: (0, cg(b, c))),
    ]
    args = [xsT, bc, bc, ssd, ssdT]
    if backward:
        in_specs += [
            pl.BlockSpec((tb, d_ssd), lambda b, c: (cg(b, c), 0)),
            pl.BlockSpec((tb, d_ssd), lambda b, c: (cg(b, c), 0)),
            pl.BlockSpec((1, d_ssd), lambda b, c: (0, 0)),
        ]
        args += [y_fwd, proj, norm_w]
        out_dtype = BF16
    else:
        in_specs += [pl.BlockSpec((d_ssd, 128), lambda b, c: (0, 0))]
        args += [dexp]
        out_dtype = F32
    return pl.pallas_call(
        functools.partial(_ssd_kernel, backward=backward, cb=cb),
        grid=(batch, nc),
        in_specs=in_specs,
        out_specs=pl.BlockSpec((tb, d_ssd), lambda b, c: (cg(b, c), 0)),
        out_shape=jax.ShapeDtypeStruct((m, d_ssd), out_dtype),
        scratch_shapes=[pltpu.VMEM((d_ssd, SSD_STATE), F32)],
        compiler_params=_cparams(("parallel", "arbitrary")),
        name="ssd_bwd" if backward else "ssd_fwd",
    )(*args)


def _mlstm_kernel(*refs, backward, dk, dv, heads_per_group, cb):
    if backward:
        (q_ref, k_ref, v_ref, nat_ref, tr_ref, hf_ref, og_ref, nw_ref, o_ref, c_ref, m_ref) = refs
    else:
        (q_ref, k_ref, v_ref, nat_ref, tr_ref, o_ref, c_ref, m_ref) = refs
    H = MLSTM_HEADS

    @pl.when(pl.program_id(1) == 0)
    def _():
        c_ref[...] = jnp.zeros_like(c_ref)
        m_ref[...] = jnp.zeros_like(m_ref)

    t_i = lax.broadcasted_iota(jnp.int32, (CHUNK, CHUNK), 0)
    s_i = lax.broadcasted_iota(jnp.int32, (CHUNK, CHUNK), 1)
    mask = (s_i >= t_i) if backward else (s_i <= t_i)
    ones_blk = jnp.ones((CHUNK, 128), BF16)
    scale = dk ** -0.5
    dense = (CHUNK, CHUNK)

    chunk_order = range(cb - 1, -1, -1) if backward else range(cb)
    for j, g0 in [(j, g0) for j in chunk_order for g0 in range(0, H, heads_per_group)]:
        ts = slice(j * CHUNK, (j + 1) * CHUNK)
        hs = list(range(g0, g0 + heads_per_group))
        lane_of = {h: h + (H if backward else 0) for h in hs}
        st = {h: {} for h in hs}
        for h in hs:
            d, hh = st[h], lane_of[h]
            li_row = tr_ref[hh:hh + 1, ts]
            cum_row = tr_ref[160 + hh:161 + hh, ts]
            d["tot"] = tr_ref[288 + hh:289 + hh, ts]
            d["base_row"] = cum_row - li_row
            grow = d["tot"] - d["base_row"]
            d["m_in"] = m_ref[h, 0:1, :]
            m_loc = jnp.broadcast_to(jnp.max(grow, axis=1, keepdims=True), (1, CHUNK))
            d["m_new"] = jnp.maximum(d["tot"] + d["m_in"], m_loc)
            d["w_row"] = jnp.exp(grow - d["m_new"])
            d["cum_d"] = jnp.broadcast_to(nat_ref[ts, 160 + hh:161 + hh], dense)
            d["rmax_d"] = jnp.broadcast_to(nat_ref[ts, 416 + hh:417 + hh], dense)
            d["v_aug"] = jnp.concatenate([v_ref[ts, h * dv:(h + 1) * dv].astype(BF16), ones_blk], axis=1)
        for h in hs:
            d = st[h]
            k = k_ref[ts, h * dk:(h + 1) * dk].astype(F32)
            d["kb"] = k.astype(BF16)
            d["kwT"] = (k.T * d["w_row"]).astype(BF16)
            d["qs"] = (q_ref[ts, h * dk:(h + 1) * dk].astype(F32) * scale).astype(BF16)
        for h in hs:
            d = st[h]
            d["c_loc"] = _dot(d["kwT"], d["v_aug"])
            d["sqk"] = _dot_nt(d["qs"], d["kb"])
            d["c_in"] = c_ref[h]
            d["qc"] = _dot(d["qs"], d["c_in"].astype(BF16))
        for h in hs:
            d = st[h]
            dlog = jnp.where(mask, d["cum_d"] - d["base_row"], -jnp.inf)
            inter = d["cum_d"] + d["m_in"]
            d["m_t"] = jnp.maximum(d["rmax_d"], inter)
            d["pm"] = (jnp.exp(dlog - d["m_t"]) * d["sqk"]).astype(BF16)
            d["a_inter"] = jnp.exp(inter - d["m_t"])
        for h in hs:
            d = st[h]
            a3 = jnp.concatenate([d["a_inter"]] * (dv // 128 + 1), axis=1)
            num = _dot(d["pm"], d["v_aug"]) + d["qc"] * a3
            den = num[:, dv:dv + 128]
            inv = 1.0 / jnp.maximum(jnp.abs(den), jnp.exp(-d["m_t"]))
            d["hout"] = num[:, 0:dv] * jnp.concatenate([inv] * (dv // 128), axis=1)
        for h in hs:
            d = st[h]
            a_prev = jnp.exp(d["tot"] + d["m_in"] - d["m_new"])
            a_prev3 = jnp.concatenate([a_prev] * (dv // 128 + 1), axis=1)
            c_ref[h] = a_prev3 * d["c_in"] + d["c_loc"]
            m_ref[h] = jnp.broadcast_to(d["m_new"], (8, 128))
        for h in hs:
            hout = st[h]["hout"]
            vs = slice(h * dv, (h + 1) * dv)
            if backward:
                hout = hout + hf_ref[ts, vs]
                hout = hout * lax.rsqrt(jnp.mean(hout * hout, axis=-1, keepdims=True) + EPS)
                hout = hout * nw_ref[:, vs]
                o_ref[ts, vs] = (_sigmoid(og_ref[ts, vs].astype(F32)) * hout).astype(o_ref.dtype)
            else:
                o_ref[ts, vs] = hout


def mlstm_scan(proj, ml, mlT, batch, seq_len, cols, *, backward, h_fwd=None, norm_w=None):
    m = proj.shape[0]
    cb = CHUNKS_PER_STEP
    tb = CHUNK * cb
    nc = seq_len // tb
    H = MLSTM_HEADS
    dk, dv = cols["dk"], cols["dv"]
    qw, vw = H * dk, H * dv
    qb, kb, vb, ob = cols["q"] // qw, cols["k"] // qw, cols["v"] // vw, cols["o"] // vw
    assert qb * qw == cols["q"] and kb * qw == cols["k"] and vb * vw == cols["v"] and ob * vw == cols["o"]

    def cg(b, c):
        return b * nc + ((nc - 1 - c) if backward else c)

    in_specs = [
        pl.BlockSpec((tb, qw), lambda b, c: (cg(b, c), qb)),
        pl.BlockSpec((tb, qw), lambda b, c: (cg(b, c), kb)),
        pl.BlockSpec((tb, vw), lambda b, c: (cg(b, c), vb)),
        pl.BlockSpec((tb, 512), lambda b, c: (cg(b, c), 0)),
        pl.BlockSpec((384, tb), lambda b, c: (0, cg(b, c))),
    ]
    args = [proj, proj, proj, ml, mlT]
    if backward:
        in_specs += [
            pl.BlockSpec((tb, vw), lambda b, c: (cg(b, c), 0)),
            pl.BlockSpec((tb, vw), lambda b, c: (cg(b, c), ob)),
            pl.BlockSpec((1, vw), lambda b, c: (0, 0)),
        ]
        args += [h_fwd, proj, norm_w]
        out_dtype = BF16
    else:
        out_dtype = F32
    return pl.pallas_call(
        functools.partial(_mlstm_kernel, backward=backward, dk=dk, dv=dv, heads_per_group=8, cb=cb),
        grid=(batch, nc),
        in_specs=in_specs,
        out_specs=pl.BlockSpec((tb, vw), lambda b, c: (cg(b, c), 0)),
        out_shape=jax.ShapeDtypeStruct((m, vw), out_dtype),
        scratch_shapes=[pltpu.VMEM((H, dk, dv + 128), F32), pltpu.VMEM((H, 8, 128), F32)],
        compiler_params=_cparams(("parallel", "arbitrary")),
        name="mlstm_bwd" if backward else "mlstm_fwd",
    )(*args)


def _outproj_kernel(a1_ref, a2_ref, w1_ref, w2_ref, x_ref, nw_ref, o_ref, xw_ref, ssq_ref):
    acc = _dot(a1_ref[...], w1_ref[...]) + _dot(a2_ref[...], w2_ref[...])
    x1 = x_ref[...] + acc
    o_ref[...] = x1
    xw_ref[...] = (x1 * nw_ref[...]).astype(xw_ref.dtype)
    ssq_ref[...] = jnp.broadcast_to(jnp.sum(x1 * x1, axis=-1, keepdims=True), ssq_ref.shape)


def outproj_residual(a1, a2, w1, w2, x, norm_w, bm=512, bn=512):
    m, k = a1.shape
    n = w1.shape[1]
    return pl.pallas_call(
        _outproj_kernel,
        grid=(n // bn, m // bm),
        in_specs=[pl.BlockSpec((bm, k), lambda j, i: (i, 0)),
                  pl.BlockSpec((bm, k), lambda j, i: (i, 0)),
                  pl.BlockSpec((k, bn), lambda j, i: (0, j)),
                  pl.BlockSpec((k, bn), lambda j, i: (0, j)),
                  pl.BlockSpec((bm, bn), lambda j, i: (i, j)),
                  pl.BlockSpec((1, bn), lambda j, i: (0, j))],
        out_specs=[pl.BlockSpec((bm, bn), lambda j, i: (i, j)),
                   pl.BlockSpec((bm, bn), lambda j, i: (i, j)),
                   pl.BlockSpec((bm, 128), lambda j, i: (i, j))],
        out_shape=[jax.ShapeDtypeStruct((m, n), F32), jax.ShapeDtypeStruct((m, n), BF16),
                   jax.ShapeDtypeStruct((m, (n // bn) * 128), F32)],
        compiler_params=_cparams(("parallel", "parallel")),
        name="outproj_residual",
    )(a1, a2, w1, w2, x, norm_w.reshape(1, n).astype(F32))


def _prep_layer(norm1_w, w_in, conv_w, conv_b, dt_bias, a_log, d_skip, ssd_norm_w, b_i, b_f,
                mlstm_norm_w, w_out, norm2_w, w_up, w_down):
    d_model = w_in.shape[0]
    d_mix = w_out.shape[0]
    d_ssd = d_mix // 2
    d_ml = d_mix - d_ssd
    n_ssd_heads = d_ssd // SSD_HEAD_DIM
    xbc_w = d_ssd + 2 * SSD_GROUPS * SSD_STATE
    dv = d_ml // MLSTM_HEADS
    dk = dv // 2
    widths = (d_ssd, xbc_w, 2 * n_ssd_heads, MLSTM_HEADS * dk, MLSTM_HEADS * dk, d_ml, d_ml,
              2 * MLSTM_HEADS, 2 * MLSTM_HEADS)
    offs = [0]
    for wd in widths:
        offs.append(offs[-1] + wd)
    assert offs[-1] == w_in.shape[1]
    assert 2 * n_ssd_heads == 128 and 2 * MLSTM_HEADS == 32
    seg = lambda i: w_in[:, offs[i]:offs[i + 1]]
    w_bf = w_in.astype(BF16)
    w_gate = jnp.concatenate([seg(2), seg(7), seg(8), jnp.zeros((d_model, 64), w_in.dtype)],
                             axis=1).astype(BF16)
    gate_bias = jnp.concatenate([dt_bias.reshape(-1), b_i.reshape(-1), b_f.reshape(-1),
                                 jnp.zeros((64,), F32)]).astype(F32).reshape(1, 256)
    cols = {"z": 0, "xbc": d_ssd, "q": 0, "k": MLSTM_HEADS * dk, "v": 2 * MLSTM_HEADS * dk}
    cols["o"] = cols["v"] + d_ml
    cols["dk"], cols["dv"], cols["d_ssd"] = dk, dv, d_ssd
    return dict(
        norm1_w=norm1_w, w_bf=w_bf, offs=offs, w_gate=w_gate, gate_bias=gate_bias,
        alog=a_log.reshape(1, 128).astype(F32),
        conv_w=conv_w.astype(F32), conv_b=conv_b.reshape(1, -1).astype(F32),
        dexp=jnp.broadcast_to(jnp.repeat(d_skip.astype(F32), SSD_HEAD_DIM)[:, None], (d_ssd, 128)),
        ssd_norm_w=ssd_norm_w.reshape(1, -1).astype(F32),
        mlstm_norm_w=mlstm_norm_w.reshape(1, -1).astype(F32),
        w_out1=w_out[:d_ssd].astype(BF16), w_out2=w_out[d_ssd:].astype(BF16),
        norm2_w=norm2_w, w_up=w_up.astype(BF16), w_down=w_down.astype(BF16), cols=cols)


def _layer(x, p, batch, seq_len):
    cols = p["cols"]
    d_ssd = cols["d_ssd"]
    h = rmsnorm_rows(x, p["norm1_w"], BF16)
    offs = p["offs"]
    proj_a = matmul(h, p["w_bf"], F32, col0=offs[0], n=offs[2] - offs[0])
    proj_b = matmul(h, p["w_bf"], BF16, col0=offs[3], n=offs[7] - offs[3])
    ssd, ssdT, ml, mlT = gates(h, p["w_gate"], p["gate_bias"], p["alog"])
    xsT = conv_silu(proj_a, cols["xbc"], d_ssd, p["conv_w"][:, :d_ssd], p["conv_b"][:, :d_ssd],
                    seq_len, True)
    bc = conv_silu(proj_a, cols["xbc"] + d_ssd, 2 * SSD_GROUPS * SSD_STATE, p["conv_w"][:, d_ssd:],
                   p["conv_b"][:, d_ssd:], seq_len, False)
    y_f = ssd_scan(xsT, bc, ssd, ssdT, batch, seq_len, backward=False, dexp=p["dexp"])
    mix1 = ssd_scan(xsT, bc, ssd, ssdT, batch, seq_len, backward=True, y_fwd=y_f,
                    proj=proj_a, norm_w=p["ssd_norm_w"])
    h_f = mlstm_scan(proj_b, ml, mlT, batch, seq_len, cols, backward=False)
    mix2 = mlstm_scan(proj_b, ml, mlT, batch, seq_len, cols, backward=True, h_fwd=h_f,
                      norm_w=p["mlstm_norm_w"])
    x1, x1w, ssq = outproj_residual(mix1, mix2, p["w_out1"], p["w_out2"], x, p["norm2_w"])
    u = matmul(x1w, p["w_up"], BF16, relu2=True, row_ssq=ssq)
    return matmul_ksplit_residual(u, p["w_down"], x1)


def _trunk(x, layers, final_norm_w):
    batch, seq_len, d = x.shape
    xf = x.reshape(batch * seq_len, d)
    for p in layers:
        xf = _layer(xf, p, batch, seq_len)
    return rmsnorm_rows(xf, final_norm_w, F32).reshape(batch, seq_len, d)


def kernel(x_prompt, x_sample, norm1_w, w_in, conv_w, conv_b, dt_bias, a_log, d_skip, ssd_norm_w,
           b_i, b_f, mlstm_norm_w, w_out, norm2_w, w_up, w_down, final_norm_w):
    depth = w_in.shape[0]
    layers = [_prep_layer(norm1_w[l], w_in[l], conv_w[l], conv_b[l], dt_bias[l], a_log[l], d_skip[l],
                          ssd_norm_w[l], b_i[l], b_f[l], mlstm_norm_w[l], w_out[l], norm2_w[l],
                          w_up[l], w_down[l]) for l in range(depth)]
    y_prompt = _trunk(x_prompt, layers, final_norm_w)
    y_sample = _trunk(x_sample, layers, final_norm_w)
    return (y_prompt, y_sample)
```

```python
import functools

import jax
import jax.numpy as jnp
from jax import lax
from jax.experimental import pallas as pl
from jax.experimental.pallas import tpu as pltpu

F32 = jnp.float32
BF16 = jnp.bfloat16

CHUNK = 128
CHUNKS_PER_STEP = 2
EPS = 1e-5
D_CONV = 5
SSD_GROUPS = 8
SSD_HEAD_DIM = 64
SSD_STATE = 128
MLSTM_HEADS = 16
VMEM_LIMIT = 56 * 1024 * 1024
KSPLIT_VMEM_LIMIT = 60 * 1024 * 1024


def _cparams(sem, vmem_limit=VMEM_LIMIT):
    return pltpu.CompilerParams(dimension_semantics=sem, vmem_limit_bytes=vmem_limit)


def _sigmoid(x):
    return 1.0 / (1.0 + jnp.exp(-x))


def _softplus(x):
    return jnp.maximum(x, 0.0) + jnp.log1p(jnp.exp(-jnp.abs(x)))


def _dot(a, b):
    return jnp.dot(a, b, preferred_element_type=F32)


def _dot_nt(a, b):
    return lax.dot_general(a, b, (((1,), (1,)), ((), ())), preferred_element_type=F32)


def _rmsnorm_kernel(x_ref, w_ref, o_ref):
    x = x_ref[...].astype(F32)
    y = x * lax.rsqrt(jnp.mean(x * x, axis=-1, keepdims=True) + EPS)
    o_ref[...] = (y * w_ref[...]).astype(o_ref.dtype)


def rmsnorm_rows(x, w, out_dtype, bm=512):
    m, d = x.shape
    return pl.pallas_call(
        _rmsnorm_kernel,
        grid=(m // bm,),
        in_specs=[pl.BlockSpec((bm, d), lambda i: (i, 0)),
                  pl.BlockSpec((1, d), lambda i: (0, 0))],
        out_specs=pl.BlockSpec((bm, d), lambda i: (i, 0)),
        out_shape=jax.ShapeDtypeStruct((m, d), out_dtype),
        compiler_params=_cparams(("parallel",)),
        name="rmsnorm_rows",
    )(x, w.reshape(1, d).astype(F32))


def _matmul_kernel(*refs, relu2, row_ssq_dim):
    if row_ssq_dim:
        a_ref, b_ref, ssq_ref, o_ref = refs
    else:
        a_ref, b_ref, o_ref = refs
    acc = _dot(a_ref[...], b_ref[...])
    if relu2:
        acc = jnp.maximum(acc, 0.0)
        acc = acc * acc
    if row_ssq_dim:
        ssq = ssq_ref[...]
        tot = ssq[:, 0:128]
        for part in range(1, ssq.shape[1] // 128):
            tot = tot + ssq[:, part * 128:(part + 1) * 128]
        r2 = 1.0 / (tot * (1.0 / row_ssq_dim) + EPS)
        acc = acc * jnp.concatenate([r2] * (acc.shape[1] // 128), axis=1)
    o_ref[...] = acc.astype(o_ref.dtype)


def matmul(a, b, out_dtype, bm=1024, bn=1024, relu2=False, col0=0, n=None, row_ssq=None):
    m, k = a.shape
    n = b.shape[1] if n is None else n
    in_specs = [pl.BlockSpec((bm, k), lambda j, i: (i, 0)),
                pl.BlockSpec((pl.Element(k), pl.Element(bn)),
                             lambda j, i: (0, pl.multiple_of(col0 + j * bn, 128)))]
    args = [a, b]
    if row_ssq is not None:
        assert relu2
        in_specs.append(pl.BlockSpec((bm, row_ssq.shape[1]), lambda j, i: (i, 0)))
        args.append(row_ssq)
    return pl.pallas_call(
        functools.partial(_matmul_kernel, relu2=relu2, row_ssq_dim=k if row_ssq is not None else 0),
        grid=(n // bn, m // bm),
        in_specs=in_specs,
        out_specs=pl.BlockSpec((bm, bn), lambda j, i: (i, j)),
        out_shape=jax.ShapeDtypeStruct((m, n), out_dtype),
        compiler_params=_cparams(("parallel", "parallel")),
        name="matmul_relu2" if relu2 else "matmul",
    )(*args)


def _matmul_ksplit_res_kernel(a_ref, b_ref, x_ref, o_ref):
    @pl.when(pl.program_id(2) == 0)
    def _():
        o_ref[...] = x_ref[...] + _dot(a_ref[...], b_ref[...])

    @pl.when(pl.program_id(2) != 0)
    def _():
        o_ref[...] = o_ref[...] + _dot(a_ref[...], b_ref[...])


def matmul_ksplit_residual(a, b, x, bm=1024, bn=1024, bk=4096):
    m, k = a.shape
    n = b.shape[1]
    return pl.pallas_call(
        _matmul_ksplit_res_kernel,
        grid=(n // bn, m // bm, k // bk),
        in_specs=[pl.BlockSpec((bm, bk), lambda j, i, kk: (i, kk)),
                  pl.BlockSpec((bk, bn), lambda j, i, kk: (kk, j)),
                  pl.BlockSpec((bm, bn), lambda j, i, kk: (i, j))],
        out_specs=pl.BlockSpec((bm, bn), lambda j, i, kk: (i, j)),
        out_shape=jax.ShapeDtypeStruct((m, n), F32),
        compiler_params=_cparams(("parallel", "parallel", "arbitrary"), KSPLIT_VMEM_LIMIT),
        name="matmul_ksplit_residual",
    )(a, b, x)


def _split3(x):
    hi = x.astype(BF16)
    r1 = x - hi.astype(F32)
    mid = r1.astype(BF16)
    r2 = r1 - mid.astype(F32)
    return hi, mid, r2.astype(BF16)


def _gates_kernel(h_ref, w_ref, bias_ref, alog_ref, ssd_ref, ssdT_ref, ml_ref, mlT_ref, *, bm):
    raw = _dot(h_ref[...], w_ref[...]) + bias_ref[...]
    dt = _softplus(raw[:, 0:128])
    a = dt * (-jnp.exp(alog_ref[...]))
    t2 = raw[:, 128:256]
    lane = lax.broadcasted_iota(jnp.int32, (CHUNK, 128), 1)
    row_t = lax.broadcasted_iota(jnp.int32, (CHUNK, 128), 0)
    lsig = -_softplus(-t2)
    row_i = lax.broadcasted_iota(jnp.int32, (CHUNK, CHUNK), 0)
    col_i = lax.broadcasted_iota(jnp.int32, (CHUNK, CHUNK), 1)
    lower = (col_i <= row_i).astype(BF16)
    upper = (col_i >= row_i).astype(BF16)
    ones = jnp.ones((CHUNK, CHUNK), BF16)
    ssd_fwd_lane = lane < 64
    ml_fwd_lane = (lane % 32) < 16

    def cums(x, fwd_lane):
        hi, mid, lo = _split3(x)
        cum_f = _dot(lower, hi) + _dot(lower, mid) + _dot(lower, lo)
        cum_b = _dot(upper, hi) + _dot(upper, mid) + _dot(upper, lo)
        tot = _dot(ones, hi) + _dot(ones, mid) + _dot(ones, lo)
        return jnp.where(fwd_lane, cum_f, cum_b), tot

    for c in range(bm // CHUNK):
        sl = slice(c * CHUNK, (c + 1) * CHUNK)
        dt_c = dt[sl]
        acum, tot = cums(a[sl], ssd_fwd_lane)
        ssd_ref[sl, 0:128] = dt_c
        ssd_ref[sl, 128:256] = acum
        ssd_ref[sl, 256:384] = tot
        ssdT_ref[0:128, sl] = dt_c.T
        ssdT_ref[128:256, sl] = acum.T
        ssdT_ref[256:384, sl] = tot.T
        t2_c = t2[sl]
        mcum, mtot = cums(lsig[sl], ml_fwd_lane)
        y = mcum - pltpu.roll(t2_c, 32, 1)
        y_f, y_b = y, y
        for d in (1, 2, 4, 8, 16, 32, 64):
            y_f = jnp.minimum(y_f, jnp.where(row_t >= d, pltpu.roll(y_f, d, 0), jnp.inf))
            y_b = jnp.minimum(y_b, jnp.where(row_t < CHUNK - d, pltpu.roll(y_b, CHUNK - d, 0), jnp.inf))
        rmax = mcum - jnp.where(ml_fwd_lane, y_f, y_b)
        ml_ref[sl, 0:128] = t2_c
        ml_ref[sl, 128:256] = mcum
        ml_ref[sl, 256:384] = mtot
        ml_ref[sl, 384:512] = rmax
        mlT_ref[0:128, sl] = t2_c.T
        mlT_ref[128:256, sl] = mcum.T
        mlT_ref[256:384, sl] = mtot.T


def gates(h, w_gate, bias, alog, bm=512):
    m, d = h.shape
    nat = pl.BlockSpec((bm, 384), lambda i: (i, 0))
    nat4 = pl.BlockSpec((bm, 512), lambda i: (i, 0))
    tr = pl.BlockSpec((384, bm), lambda i: (0, i))
    return pl.pallas_call(
        functools.partial(_gates_kernel, bm=bm),
        grid=(m // bm,),
        in_specs=[pl.BlockSpec((bm, d), lambda i: (i, 0)),
                  pl.BlockSpec((d, 256), lambda i: (0, 0)),
                  pl.BlockSpec((1, 256), lambda i: (0, 0)),
                  pl.BlockSpec((1, 128), lambda i: (0, 0))],
        out_specs=[nat, tr, nat4, tr],
        out_shape=[jax.ShapeDtypeStruct((m, 384), F32), jax.ShapeDtypeStruct((384, m), F32),
                   jax.ShapeDtypeStruct((m, 512), F32), jax.ShapeDtypeStruct((384, m), F32)],
        compiler_params=_cparams(("parallel",)),
        name="gates",
    )(h, w_gate, bias, alog)


def _matmul_conv_kernel(h_ref, hp_ref, hn_ref, w_ref, cw_ref, cb_ref, o_ref, *, bm, seq_len,
                        transpose_out, sub):
    i = pl.program_id(1)
    w = w_ref[...]
    at_start = (i * bm) % seq_len == 0
    at_end = ((i + 1) * bm) % seq_len == 0
    cw = cw_ref[...]
    cb = cb_ref[...]
    nsub = bm // sub
    row = lax.broadcasted_iota(jnp.int32, (sub, w.shape[1]), 0)

    def conv(cur, prev8, next8):
        m2 = pltpu.roll(cur, 2, 0)
        m2 = jnp.where(row == 0, prev8[6:7], jnp.where(row == 1, prev8[7:8], m2))
        m1 = pltpu.roll(cur, 1, 0)
        m1 = jnp.where(row == 0, prev8[7:8], m1)
        p1 = pltpu.roll(cur, sub - 1, 0)
        p1 = jnp.where(row == sub - 1, next8[0:1], p1)
        p2 = pltpu.roll(cur, sub - 2, 0)
        p2 = jnp.where(row == sub - 2, next8[0:1], jnp.where(row == sub - 1, next8[1:2], p2))
        out = cw[0:1] * m2
        out = out + cw[1:2] * m1
        out = out + cw[2:3] * cur
        out = out + cw[3:4] * p1
        out = out + cw[4:5] * p2
        out = out + cb
        return out * _sigmoid(out)

    def emit(r, res):
        if transpose_out:
            o_ref[:, r * sub:(r + 1) * sub] = res.T.astype(o_ref.dtype)
        else:
            o_ref[r * sub:(r + 1) * sub, :] = res.astype(o_ref.dtype)

    prev8 = jnp.where(at_start, 0.0, _dot(hp_ref[...], w))[8:16]
    blocks = [_dot(h_ref[0:sub, :], w)]
    for r in range(nsub):
        if r + 1 < nsub:
            blocks.append(_dot(h_ref[(r + 1) * sub:(r + 2) * sub, :], w))
            next8 = blocks[r + 1][0:8]
        else:
            next8 = jnp.where(at_end, 0.0, _dot(hn_ref[...], w))[0:8]
        emit(r, conv(blocks[r], prev8, next8))
        prev8 = blocks[r][sub - 8:sub]


def matmul_conv(h, w, col0, n, conv_w, conv_b, seq_len, transpose_out, bm=1024, bn=1024, sub=256):
    m, k = h.shape
    assert seq_len % bm == 0 and bm % sub == 0
    nhb = m // 16
    if transpose_out:
        out_spec = pl.BlockSpec((bn, bm), lambda j, i: (j, i))
        out_shape = jax.ShapeDtypeStruct((n, m), BF16)
    else:
        out_spec = pl.BlockSpec((bm, bn), lambda j, i: (i, j))
        out_shape = jax.ShapeDtypeStruct((m, n), BF16)
    return pl.pallas_call(
        functools.partial(_matmul_conv_kernel, bm=bm, seq_len=seq_len, transpose_out=transpose_out,
                          sub=sub),
        grid=(n // bn, m // bm),
        in_specs=[pl.BlockSpec((bm, k), lambda j, i: (i, 0)),
                  pl.BlockSpec((16, k), lambda j, i: (jnp.maximum(i * (bm // 16) - 1, 0), 0)),
                  pl.BlockSpec((16, k), lambda j, i: (jnp.minimum((i + 1) * (bm // 16), nhb - 1), 0)),
                  pl.BlockSpec((pl.Element(k), pl.Element(bn)),
                               lambda j, i: (0, pl.multiple_of(col0 + j * bn, 128))),
                  pl.BlockSpec((D_CONV, bn), lambda j, i: (0, j)),
                  pl.BlockSpec((1, bn), lambda j, i: (0, j))],
        out_specs=out_spec,
        out_shape=out_shape,
        compiler_params=_cparams(("parallel", "parallel")),
        name="matmul_conv_t" if transpose_out else "matmul_conv",
    )(h, h, h, w, conv_w, conv_b)


def _ssd_kernel(*refs, backward, cb):
    if backward:
        (xsT_ref, b_ref, c_ref, nat_ref, tr_ref, yf_ref, z_ref, nw_ref, o_ref, s_ref) = refs
    else:
        (xsT_ref, b_ref, c_ref, nat_ref, tr_ref, dexp_ref, o_ref, s_ref) = refs
    G, R, P, N = SSD_GROUPS, 8, SSD_HEAD_DIM, SSD_STATE
    gw = R * P
    h0 = G * R if backward else 0

    @pl.when(pl.program_id(1) == 0)
    def _():
        s_ref[...] = jnp.zeros_like(s_ref)

    s_i = lax.broadcasted_iota(jnp.int32, (CHUNK, CHUNK), 0)
    l_i = lax.broadcasted_iota(jnp.int32, (CHUNK, CHUNK), 1)
    mask = (l_i <= s_i) if backward else (l_i >= s_i)
    chunk_order = range(cb - 1, -1, -1) if backward else range(cb)
    for j, g in [(j, g) for j in chunk_order for g in range(G)]:
        ts = slice(j * CHUNK, (j + 1) * CHUNK)
        gs = slice(g * gw, (g + 1) * gw)
        hg = h0 + g * R
        bm = b_ref[ts, g * N:(g + 1) * N]
        cm = c_ref[ts, g * N:(g + 1) * N]
        dt = tr_ref[hg:hg + R, ts]
        acum = tr_ref[128 + hg:128 + hg + R, ts]
        tot = tr_ref[256 + hg:256 + hg + R, ts]
        cbT = _dot_nt(bm, cm)
        s_old = s_ref[gs, :]
        yoffT = _dot_nt(s_old.astype(BF16), cm)
        e_acum = jnp.exp(acum)
        dte = jnp.exp(tot - acum)
        e_tot = jnp.exp(tot)
        y_pieces = []
        xd_pieces = []
        for r in range(R):
            hs = slice(g * gw + r * P, g * gw + (r + 1) * P)
            xr = xsT_ref[hs, ts].astype(F32)
            xdt = xr * dt[r:r + 1, :]
            col = nat_ref[ts, 128 + hg + r:128 + hg + r + 1]
            seg = acum[r:r + 1, :] - col
            dec = jnp.exp(jnp.where(mask, seg, -jnp.inf))
            mt = (cbT * dec).astype(BF16)
            y_r = _dot(xdt.astype(BF16), mt) + yoffT[r * P:(r + 1) * P, :] * e_acum[r:r + 1, :]
            if not backward:
                y_r = y_r + dexp_ref[hs, :] * xr
            y_pieces.append(y_r)
            xd_pieces.append((xdt * dte[r:r + 1, :]).astype(BF16))
        yT = jnp.concatenate(y_pieces, axis=0)
        upd = _dot(jnp.concatenate(xd_pieces, axis=0), bm)
        for r in range(R):
            hs = slice(r * P, (r + 1) * P)
            s_ref[g * gw + r * P:g * gw + (r + 1) * P, :] = s_old[hs, :] * e_tot[r:r + 1, :] + upd[hs, :]
        y = yT.T
        if backward:
            y = y + yf_ref[ts, gs]
            z = z_ref[ts, gs].astype(F32)
            y = y * (z * _sigmoid(z))
            y = y * lax.rsqrt(jnp.mean(y * y, axis=-1, keepdims=True) + EPS)
            o_ref[ts, gs] = (y * nw_ref[:, gs]).astype(o_ref.dtype)
        else:
            o_ref[ts, gs] = y


def ssd_scan(xsT, bc, ssd, ssdT, batch, seq_len, *, backward, dexp=None,
             y_fwd=None, proj=None, norm_w=None):
    m = xsT.shape[1]
    cb = CHUNKS_PER_STEP
    tb = CHUNK * cb
    nc = seq_len // tb
    G = SSD_GROUPS
    d_ssd = xsT.shape[0]

    def cg(b, c):
        return b * nc + ((nc - 1 - c) if backward else c)

    in_specs = [
        pl.BlockSpec((d_ssd, tb), lambda b, c: (0, cg(b, c))),
        pl.BlockSpec((tb, G * SSD_STATE), lambda b, c: (cg(b, c), 0)),
        pl.BlockSpec((tb, G * SSD_STATE), lambda b, c: (cg(b, c), 1)),
        pl.BlockSpec((tb, 384), lambda b, c: (cg(b, c), 0)),
        pl.BlockSpec((384, tb), lambda b, c: (0, cg(b, c))),
    ]
    args = [xsT, bc, bc, ssd, ssdT]
    if backward:
        in_specs += [
            pl.BlockSpec((tb, d_ssd), lambda b, c: (cg(b, c), 0)),
            pl.BlockSpec((tb, d_ssd), lambda b, c: (cg(b, c), 0)),
            pl.BlockSpec((1, d_ssd), lambda b, c: (0, 0)),
        ]
        args += [y_fwd, proj, norm_w]
        out_dtype = BF16
    else:
        in_specs += [pl.BlockSpec((d_ssd, 128), lambda b, c: (0, 0))]
        args += [dexp]
        out_dtype = F32
    return pl.pallas_call(
        functools.partial(_ssd_kernel, backward=backward, cb=cb),
        grid=(batch, nc),
        in_specs=in_specs,
        out_specs=pl.BlockSpec((tb, d_ssd), lambda b, c: (cg(b, c), 0)),
        out_shape=jax.ShapeDtypeStruct((m, d_ssd), out_dtype),
        scratch_shapes=[pltpu.VMEM((d_ssd, SSD_STATE), F32)],
        compiler_params=_cparams(("parallel", "arbitrary")),
        name="ssd_bwd" if backward else "ssd_fwd",
    )(*args)


def _mlstm_kernel(*refs, backward, dk, dv, heads_per_group, cb):
    if backward:
        (q_ref, k_ref, v_ref, nat_ref, tr_ref, hf_ref, og_ref, nw_ref, o_ref, c_ref, m_ref) = refs
    else:
        (q_ref, k_ref, v_ref, nat_ref, tr_ref, o_ref, c_ref, m_ref) = refs
    H = MLSTM_HEADS

    @pl.when(pl.program_id(1) == 0)
    def _():
        c_ref[...] = jnp.zeros_like(c_ref)
        m_ref[...] = jnp.zeros_like(m_ref)

    t_i = lax.broadcasted_iota(jnp.int32, (CHUNK, CHUNK), 0)
    s_i = lax.broadcasted_iota(jnp.int32, (CHUNK, CHUNK), 1)
    mask = (s_i >= t_i) if backward else (s_i <= t_i)
    ones_blk = jnp.ones((CHUNK, 128), BF16)
    scale = dk ** -0.5
    dense = (CHUNK, CHUNK)

    chunk_order = range(cb - 1, -1, -1) if backward else range(cb)
    for j, g0 in [(j, g0) for j in chunk_order for g0 in range(0, H, heads_per_group)]:
        ts = slice(j * CHUNK, (j + 1) * CHUNK)
        hs = list(range(g0, g0 + heads_per_group))
        lane_of = {h: h + (H if backward else 0) for h in hs}
        st = {h: {} for h in hs}
        for h in hs:
            d, hh = st[h], lane_of[h]
            li_row = tr_ref[hh:hh + 1, ts]
            cum_row = tr_ref[160 + hh:161 + hh, ts]
            d["tot"] = tr_ref[288 + hh:289 + hh, ts]
            d["base_row"] = cum_row - li_row
            grow = d["tot"] - d["base_row"]
            d["m_in"] = m_ref[h, 0:1, :]
            m_loc = jnp.broadcast_to(jnp.max(grow, axis=1, keepdims=True), (1, CHUNK))
            d["m_new"] = jnp.maximum(d["tot"] + d["m_in"], m_loc)
            d["w_row"] = jnp.exp(grow - d["m_new"])
            d["cum_d"] = jnp.broadcast_to(nat_ref[ts, 160 + hh:161 + hh], dense)
            d["rmax_d"] = jnp.broadcast_to(nat_ref[ts, 416 + hh:417 + hh], dense)
            d["v_aug"] = jnp.concatenate([v_ref[ts, h * dv:(h + 1) * dv].astype(BF16), ones_blk], axis=1)
        for h in hs:
            d = st[h]
            k = k_ref[ts, h * dk:(h + 1) * dk].astype(F32)
            d["kb"] = k.astype(BF16)
            d["kwT"] = (k.T * d["w_row"]).astype(BF16)
            d["qs"] = (q_ref[ts, h * dk:(h + 1) * dk].astype(F32) * scale).astype(BF16)
        for h in hs:
            d = st[h]
            d["c_loc"] = _dot(d["kwT"], d["v_aug"])
            d["sqk"] = _dot_nt(d["qs"], d["kb"])
            d["c_in"] = c_ref[h]
            d["qc"] = _dot(d["qs"], d["c_in"].astype(BF16))
        for h in hs:
            d = st[h]
            dlog = jnp.where(mask, d["cum_d"] - d["base_row"], -jnp.inf)
            inter = d["cum_d"] + d["m_in"]
            d["m_t"] = jnp.maximum(d["rmax_d"], inter)
            d["pm"] = (jnp.exp(dlog - d["m_t"]) * d["sqk"]).astype(BF16)
            d["a_inter"] = jnp.exp(inter - d["m_t"])
        for h in hs:
            d = st[h]
            a3 = jnp.concatenate([d["a_inter"]] * (dv // 128 + 1), axis=1)
            num = _dot(d["pm"], d["v_aug"]) + d["qc"] * a3
            den = num[:, dv:dv + 128]
            inv = 1.0 / jnp.maximum(jnp.abs(den), jnp.exp(-d["m_t"]))
            d["hout"] = num[:, 0:dv] * jnp.concatenate([inv] * (dv // 128), axis=1)
        for h in hs:
            d = st[h]
            a_prev = jnp.exp(d["tot"] + d["m_in"] - d["m_new"])
            a_prev3 = jnp.concatenate([a_prev] * (dv // 128 + 1), axis=1)
            c_ref[h] = a_prev3 * d["c_in"] + d["c_loc"]
            m_ref[h] = jnp.broadcast_to(d["m_new"], (8, 128))
        for h in hs:
            hout = st[h]["hout"]
            vs = slice(h * dv, (h + 1) * dv)
            if backward:
                hout = hout + hf_ref[ts, vs]
                hout = hout * lax.rsqrt(jnp.mean(hout * hout, axis=-1, keepdims=True) + EPS)
                hout = hout * nw_ref[:, vs]
                o_ref[ts, vs] = (_sigmoid(og_ref[ts, vs].astype(F32)) * hout).astype(o_ref.dtype)
            else:
                o_ref[ts, vs] = hout


def mlstm_scan(proj, ml, mlT, batch, seq_len, cols, *, backward, h_fwd=None, norm_w=None):
    m = proj.shape[0]
    cb = CHUNKS_PER_STEP
    tb = CHUNK * cb
    nc = seq_len // tb
    H = MLSTM_HEADS
    dk, dv = cols["dk"], cols["dv"]
    qw, vw = H * dk, H * dv
    qb, kb, vb, ob = cols["q"] // qw, cols["k"] // qw, cols["v"] // vw, cols["o"] // vw
    assert qb * qw == cols["q"] and kb * qw == cols["k"] and vb * vw == cols["v"] and ob * vw == cols["o"]

    def cg(b, c):
        return b * nc + ((nc - 1 - c) if backward else c)

    in_specs = [
        pl.BlockSpec((tb, qw), lambda b, c: (cg(b, c), qb)),
        pl.BlockSpec((tb, qw), lambda b, c: (cg(b, c), kb)),
        pl.BlockSpec((tb, vw), lambda b, c: (cg(b, c), vb)),
        pl.BlockSpec((tb, 512), lambda b, c: (cg(b, c), 0)),
        pl.BlockSpec((384, tb), lambda b, c: (0, cg(b, c))),
    ]
    args = [proj, proj, proj, ml, mlT]
    if backward:
        in_specs += [
            pl.BlockSpec((tb, vw), lambda b, c: (cg(b, c), 0)),
            pl.BlockSpec((tb, vw), lambda b, c: (cg(b, c), ob)),
            pl.BlockSpec((1, vw), lambda b, c: (0, 0)),
        ]
        args += [h_fwd, proj, norm_w]
        out_dtype = BF16
    else:
        out_dtype = F32
    return pl.pallas_call(
        functools.partial(_mlstm_kernel, backward=backward, dk=dk, dv=dv, heads_per_group=8, cb=cb),
        grid=(batch, nc),
        in_specs=in_specs,
        out_specs=pl.BlockSpec((tb, vw), lambda b, c: (cg(b, c), 0)),
        out_shape=jax.ShapeDtypeStruct((m, vw), out_dtype),
        scratch_shapes=[pltpu.VMEM((H, dk, dv + 128), F32), pltpu.VMEM((H, 8, 128), F32)],
        compiler_params=_cparams(("parallel", "arbitrary")),
        name="mlstm_bwd" if backward else "mlstm_fwd",
    )(*args)


def _outproj_kernel(a1_ref, a2_ref, w1_ref, w2_ref, x_ref, nw_ref, o_ref, xw_ref, ssq_ref):
    acc = _dot(a1_ref[...], w1_ref[...]) + _dot(a2_ref[...], w2_ref[...])
    x1 = x_ref[...] + acc
    o_ref[...] = x1
    xw_ref[...] = (x1 * nw_ref[...]).astype(xw_ref.dtype)
    ssq_ref[...] = jnp.broadcast_to(jnp.sum(x1 * x1, axis=-1, keepdims=True), ssq_ref.shape)


def outproj_residual(a1, a2, w1, w2, x, norm_w, bm=512, bn=512):
    m, k = a1.shape
    n = w1.shape[1]
    return pl.pallas_call(
        _outproj_kernel,
        grid=(n // bn, m // bm),
        in_specs=[pl.BlockSpec((bm, k), lambda j, i: (i, 0)),
                  pl.BlockSpec((bm, k), lambda j, i: (i, 0)),
                  pl.BlockSpec((k, bn), lambda j, i: (0, j)),
                  pl.BlockSpec((k, bn), lambda j, i: (0, j)),
                  pl.BlockSpec((bm, bn), lambda j, i: (i, j)),
                  pl.BlockSpec((1, bn), lambda j, i: (0, j))],
        out_specs=[pl.BlockSpec((bm, bn), lambda j, i: (i, j)),
                   pl.BlockSpec((bm, bn), lambda j, i: (i, j)),
                   pl.BlockSpec((bm, 128), lambda j, i: (i, j))],
        out_shape=[jax.ShapeDtypeStruct((m, n), F32), jax.ShapeDtypeStruct((m, n), BF16),
                   jax.ShapeDtypeStruct((m, (n // bn) * 128), F32)],
        compiler_params=_cparams(("parallel", "parallel")),
        name="outproj_residual",
    )(a1, a2, w1, w2, x, norm_w.reshape(1, n).astype(F32))


def _prep_layer(norm1_w, w_in, conv_w, conv_b, dt_bias, a_log, d_skip, ssd_norm_w, b_i, b_f,
                mlstm_norm_w, w_out, norm2_w, w_up, w_down):
    d_model = w_in.shape[0]
    d_mix = w_out.shape[0]
    d_ssd = d_mix // 2
    d_ml = d_mix - d_ssd
    n_ssd_heads = d_ssd // SSD_HEAD_DIM
    xbc_w = d_ssd + 2 * SSD_GROUPS * SSD_STATE
    dv = d_ml // MLSTM_HEADS
    dk = dv // 2
    widths = (d_ssd, xbc_w, 2 * n_ssd_heads, MLSTM_HEADS * dk, MLSTM_HEADS * dk, d_ml, d_ml,
              2 * MLSTM_HEADS, 2 * MLSTM_HEADS)
    offs = [0]
    for wd in widths:
        offs.append(offs[-1] + wd)
    assert offs[-1] == w_in.shape[1]
    assert 2 * n_ssd_heads == 128 and 2 * MLSTM_HEADS == 32
    seg = lambda i: w_in[:, offs[i]:offs[i + 1]]
    w_bf = w_in.astype(BF16)
    w_gate = jnp.concatenate([seg(2), seg(7), seg(8), jnp.zeros((d_model, 64), w_in.dtype)],
                             axis=1).astype(BF16)
    gate_bias = jnp.concatenate([dt_bias.reshape(-1), b_i.reshape(-1), b_f.reshape(-1),
                                 jnp.zeros((64,), F32)]).astype(F32).reshape(1, 256)
    cols = {"q": 0, "k": MLSTM_HEADS * dk, "v": 2 * MLSTM_HEADS * dk}
    cols["o"] = cols["v"] + d_ml
    cols["dk"], cols["dv"], cols["d_ssd"] = dk, dv, d_ssd
    return dict(
        norm1_w=norm1_w, w_bf=w_bf, offs=offs, w_gate=w_gate, gate_bias=gate_bias,
        alog=a_log.reshape(1, 128).astype(F32),
        conv_w=conv_w.astype(F32), conv_b=conv_b.reshape(1, -1).astype(F32),
        dexp=jnp.broadcast_to(jnp.repeat(d_skip.astype(F32), SSD_HEAD_DIM)[:, None], (d_ssd, 128)),
        ssd_norm_w=ssd_norm_w.reshape(1, -1).astype(F32),
        mlstm_norm_w=mlstm_norm_w.reshape(1, -1).astype(F32),
        w_out1=w_out[:d_ssd].astype(BF16), w_out2=w_out[d_ssd:].astype(BF16),
        norm2_w=norm2_w, w_up=w_up.astype(BF16), w_down=w_down.astype(BF16), cols=cols)


def _layer(x, p, batch, seq_len):
    cols = p["cols"]
    d_ssd = cols["d_ssd"]
    h = rmsnorm_rows(x, p["norm1_w"], BF16)
    offs = p["offs"]
    proj_z = matmul(h, p["w_bf"], F32, col0=offs[0], n=d_ssd)
    proj_b = matmul(h, p["w_bf"], BF16, col0=offs[3], n=offs[7] - offs[3])
    ssd, ssdT, ml, mlT = gates(h, p["w_gate"], p["gate_bias"], p["alog"])
    xsT = matmul_conv(h, p["w_bf"], offs[1], d_ssd, p["conv_w"][:, :d_ssd], p["conv_b"][:, :d_ssd],
                      seq_len, True)
    bc = matmul_conv(h, p["w_bf"], offs[1] + d_ssd, 2 * SSD_GROUPS * SSD_STATE,
                     p["conv_w"][:, d_ssd:], p["conv_b"][:, d_ssd:], seq_len, False)
    y_f = ssd_scan(xsT, bc, ssd, ssdT, batch, seq_len, backward=False, dexp=p["dexp"])
    mix1 = ssd_scan(xsT, bc, ssd, ssdT, batch, seq_len, backward=True, y_fwd=y_f,
                    proj=proj_z, norm_w=p["ssd_norm_w"])
    h_f = mlstm_scan(proj_b, ml, mlT, batch, seq_len, cols, backward=False)
    mix2 = mlstm_scan(proj_b, ml, mlT, batch, seq_len, cols, backward=True, h_fwd=h_f,
                      norm_w=p["mlstm_norm_w"])
    x1, x1w, ssq = outproj_residual(mix1, mix2, p["w_out1"], p["w_out2"], x, p["norm2_w"])
    u = matmul(x1w, p["w_up"], BF16, relu2=True, row_ssq=ssq)
    return matmul_ksplit_residual(u, p["w_down"], x1)


def _trunk(x, layers, final_norm_w):
    batch, seq_len, d = x.shape
    xf = x.reshape(batch * seq_len, d)
    for p in layers:
        xf = _layer(xf, p, batch, seq_len)
    return rmsnorm_rows(xf, final_norm_w, F32).reshape(batch, seq_len, d)


def kernel(x_prompt, x_sample, norm1_w, w_in, conv_w, conv_b, dt_bias, a_log, d_skip, ssd_norm_w,
           b_i, b_f, mlstm_norm_w, w_out, norm2_w, w_up, w_down, final_norm_w):
    depth = w_in.shape[0]
    layers = [_prep_layer(norm1_w[l], w_in[l], conv_w[l], conv_b[l], dt_bias[l], a_log[l], d_skip[l],
                          ssd_norm_w[l], b_i[l], b_f[l], mlstm_norm_w[l], w_out[l], norm2_w[l],
                          w_up[l], w_down[l]) for l in range(depth)]
    y_prompt = _trunk(x_prompt, layers, final_norm_w)
    y_sample = _trunk(x_sample, layers, final_norm_w)
    return (y_prompt, y_sample)
```

```python
import functools

import jax
import jax.numpy as jnp
from jax import lax
from jax.experimental import pallas as pl
from jax.experimental.pallas import tpu as pltpu

F32 = jnp.float32
BF16 = jnp.bfloat16

CHUNK = 128
CHUNKS_PER_STEP = 2
EPS = 1e-5
D_CONV = 5
SSD_GROUPS = 8
SSD_HEAD_DIM = 64
SSD_STATE = 128
MLSTM_HEADS = 16
VMEM_LIMIT = 56 * 1024 * 1024
KSPLIT_VMEM_LIMIT = 60 * 1024 * 1024


def _cparams(sem, vmem_limit=VMEM_LIMIT):
    return pltpu.CompilerParams(dimension_semantics=sem, vmem_limit_bytes=vmem_limit)


def _sigmoid(x):
    return 1.0 / (1.0 + jnp.exp(-x))


def _softplus(x):
    return jnp.maximum(x, 0.0) + jnp.log1p(jnp.exp(-jnp.abs(x)))


def _dot(a, b):
    return jnp.dot(a, b, preferred_element_type=F32)


def _dot_nt(a, b):
    return lax.dot_general(a, b, (((1,), (1,)), ((), ())), preferred_element_type=F32)


def _rmsnorm_kernel(x_ref, w_ref, o_ref):
    x = x_ref[...].astype(F32)
    y = x * lax.rsqrt(jnp.mean(x * x, axis=-1, keepdims=True) + EPS)
    o_ref[...] = (y * w_ref[...]).astype(o_ref.dtype)


def rmsnorm_rows(x, w, out_dtype, bm=512):
    m, d = x.shape
    return pl.pallas_call(
        _rmsnorm_kernel,
        grid=(m // bm,),
        in_specs=[pl.BlockSpec((bm, d), lambda i: (i, 0)),
                  pl.BlockSpec((1, d), lambda i: (0, 0))],
        out_specs=pl.BlockSpec((bm, d), lambda i: (i, 0)),
        out_shape=jax.ShapeDtypeStruct((m, d), out_dtype),
        compiler_params=_cparams(("parallel",)),
        name="rmsnorm_rows",
    )(x, w.reshape(1, d).astype(F32))


def _cast_kernel(x_ref, o_ref):
    o_ref[...] = x_ref[...].astype(o_ref.dtype)


def cast_rows(x, out_dtype, br=128):
    r, c = x.shape
    return pl.pallas_call(
        _cast_kernel,
        grid=(r // br,),
        in_specs=[pl.BlockSpec((br, c), lambda i: (i, 0))],
        out_specs=pl.BlockSpec((br, c), lambda i: (i, 0)),
        out_shape=jax.ShapeDtypeStruct((r, c), out_dtype),
        compiler_params=_cparams(("parallel",)),
        name="cast_rows",
    )(x)


def _matmul_kernel(*refs, relu2, row_ssq_dim):
    if row_ssq_dim:
        a_ref, b_ref, ssq_ref, o_ref = refs
    else:
        a_ref, b_ref, o_ref = refs
    acc = _dot(a_ref[...], b_ref[...])
    if relu2:
        acc = jnp.maximum(acc, 0.0)
        acc = acc * acc
    if row_ssq_dim:
        ssq = ssq_ref[...]
        tot = ssq[:, 0:128]
        for part in range(1, ssq.shape[1] // 128):
            tot = tot + ssq[:, part * 128:(part + 1) * 128]
        r2 = 1.0 / (tot * (1.0 / row_ssq_dim) + EPS)
        acc = acc * jnp.concatenate([r2] * (acc.shape[1] // 128), axis=1)
    o_ref[...] = acc.astype(o_ref.dtype)


def matmul(a, b, out_dtype, bm=1024, bn=1024, relu2=False, col0=0, n=None, row_ssq=None):
    m, k = a.shape
    n = b.shape[1] if n is None else n
    in_specs = [pl.BlockSpec((bm, k), lambda j, i: (i, 0)),
                pl.BlockSpec((pl.Element(k), pl.Element(bn)),
                             lambda j, i: (0, pl.multiple_of(col0 + j * bn, 128)))]
    args = [a, b]
    if row_ssq is not None:
        assert relu2
        in_specs.append(pl.BlockSpec((bm, row_ssq.shape[1]), lambda j, i: (i, 0)))
        args.append(row_ssq)
    return pl.pallas_call(
        functools.partial(_matmul_kernel, relu2=relu2, row_ssq_dim=k if row_ssq is not None else 0),
        grid=(n // bn, m // bm),
        in_specs=in_specs,
        out_specs=pl.BlockSpec((bm, bn), lambda j, i: (i, j)),
        out_shape=jax.ShapeDtypeStruct((m, n), out_dtype),
        compiler_params=_cparams(("parallel", "parallel")),
        name="matmul_relu2" if relu2 else "matmul",
    )(*args)


def _matmul_ksplit_res_kernel(a_ref, b_ref, x_ref, o_ref):
    @pl.when(pl.program_id(2) == 0)
    def _():
        o_ref[...] = x_ref[...] + _dot(a_ref[...], b_ref[...])

    @pl.when(pl.program_id(2) != 0)
    def _():
        o_ref[...] = o_ref[...] + _dot(a_ref[...], b_ref[...])


def matmul_ksplit_residual(a, b, x, bm=1024, bn=1024, bk=4096):
    m, k = a.shape
    n = b.shape[1]
    return pl.pallas_call(
        _matmul_ksplit_res_kernel,
        grid=(n // bn, m // bm, k // bk),
        in_specs=[pl.BlockSpec((bm, bk), lambda j, i, kk: (i, kk)),
                  pl.BlockSpec((bk, bn), lambda j, i, kk: (kk, j)),
                  pl.BlockSpec((bm, bn), lambda j, i, kk: (i, j))],
        out_specs=pl.BlockSpec((bm, bn), lambda j, i, kk: (i, j)),
        out_shape=jax.ShapeDtypeStruct((m, n), F32),
        compiler_params=_cparams(("parallel", "parallel", "arbitrary"), KSPLIT_VMEM_LIMIT),
        name="matmul_ksplit_residual",
    )(a, b, x)


def _split3(x):
    hi = x.astype(BF16)
    r1 = x - hi.astype(F32)
    mid = r1.astype(BF16)
    r2 = r1 - mid.astype(F32)
    return hi, mid, r2.astype(BF16)


def _gates_kernel(h_ref, w_ref, bias_ref, alog_ref, ssd_ref, ssdT_ref, ml_ref, mlT_ref, *, bm):
    raw = _dot(h_ref[...], w_ref[...]) + bias_ref[...]
    dt = _softplus(raw[:, 0:128])
    a = dt * (-jnp.exp(alog_ref[...]))
    t2 = raw[:, 128:256]
    lane = lax.broadcasted_iota(jnp.int32, (CHUNK, 128), 1)
    row_t = lax.broadcasted_iota(jnp.int32, (CHUNK, 128), 0)
    lsig = -_softplus(-t2)
    row_i = lax.broadcasted_iota(jnp.int32, (CHUNK, CHUNK), 0)
    col_i = lax.broadcasted_iota(jnp.int32, (CHUNK, CHUNK), 1)
    lower = (col_i <= row_i).astype(BF16)
    upper = (col_i >= row_i).astype(BF16)
    ones = jnp.ones((CHUNK, CHUNK), BF16)
    ssd_fwd_lane = lane < 64
    ml_fwd_lane = (lane % 32) < 16

    def cums(x, fwd_lane):
        hi, mid, lo = _split3(x)
        cum_f = _dot(lower, hi) + _dot(lower, mid) + _dot(lower, lo)
        cum_b = _dot(upper, hi) + _dot(upper, mid) + _dot(upper, lo)
        tot = _dot(ones, hi) + _dot(ones, mid) + _dot(ones, lo)
        return jnp.where(fwd_lane, cum_f, cum_b), tot

    for c in range(bm // CHUNK):
        sl = slice(c * CHUNK, (c + 1) * CHUNK)
        dt_c = dt[sl]
        acum, tot = cums(a[sl], ssd_fwd_lane)
        ssd_ref[sl, 0:128] = dt_c
        ssd_ref[sl, 128:256] = acum
        ssd_ref[sl, 256:384] = tot
        ssdT_ref[0:128, sl] = dt_c.T
        ssdT_ref[128:256, sl] = acum.T
        ssdT_ref[256:384, sl] = tot.T
        t2_c = t2[sl]
        mcum, mtot = cums(lsig[sl], ml_fwd_lane)
        y = mcum - pltpu.roll(t2_c, 32, 1)
        y_f, y_b = y, y
        for d in (1, 2, 4, 8, 16, 32, 64):
            y_f = jnp.minimum(y_f, jnp.where(row_t >= d, pltpu.roll(y_f, d, 0), jnp.inf))
            y_b = jnp.minimum(y_b, jnp.where(row_t < CHUNK - d, pltpu.roll(y_b, CHUNK - d, 0), jnp.inf))
        rmax = mcum - jnp.where(ml_fwd_lane, y_f, y_b)
        ml_ref[sl, 0:128] = t2_c
        ml_ref[sl, 128:256] = mcum
        ml_ref[sl, 256:384] = mtot
        ml_ref[sl, 384:512] = rmax
        mlT_ref[0:128, sl] = t2_c.T
        mlT_ref[128:256, sl] = mcum.T
        mlT_ref[256:384, sl] = mtot.T


def gates(h, w_gate, bias, alog, bm=512):
    m, d = h.shape
    nat = pl.BlockSpec((bm, 384), lambda i: (i, 0))
    nat4 = pl.BlockSpec((bm, 512), lambda i: (i, 0))
    tr = pl.BlockSpec((384, bm), lambda i: (0, i))
    return pl.pallas_call(
        functools.partial(_gates_kernel, bm=bm),
        grid=(m // bm,),
        in_specs=[pl.BlockSpec((bm, d), lambda i: (i, 0)),
                  pl.BlockSpec((d, 256), lambda i: (0, 0)),
                  pl.BlockSpec((1, 256), lambda i: (0, 0)),
                  pl.BlockSpec((1, 128), lambda i: (0, 0))],
        out_specs=[nat, tr, nat4, tr],
        out_shape=[jax.ShapeDtypeStruct((m, 384), F32), jax.ShapeDtypeStruct((384, m), F32),
                   jax.ShapeDtypeStruct((m, 512), F32), jax.ShapeDtypeStruct((384, m), F32)],
        compiler_params=_cparams(("parallel",)),
        name="gates",
    )(h, w_gate, bias, alog)


def _matmul_conv_kernel(h_ref, hp_ref, hn_ref, w_ref, cw_ref, cb_ref, o_ref, *, bm, seq_len,
                        transpose_out, sub):
    i = pl.program_id(1)
    w = w_ref[...]
    at_start = (i * bm) % seq_len == 0
    at_end = ((i + 1) * bm) % seq_len == 0
    cw = cw_ref[...]
    cb = cb_ref[...]
    nsub = bm // sub
    row = lax.broadcasted_iota(jnp.int32, (sub, w.shape[1]), 0)

    def conv(cur, prev8, next8):
        m2 = pltpu.roll(cur, 2, 0)
        m2 = jnp.where(row == 0, prev8[6:7], jnp.where(row == 1, prev8[7:8], m2))
        m1 = pltpu.roll(cur, 1, 0)
        m1 = jnp.where(row == 0, prev8[7:8], m1)
        p1 = pltpu.roll(cur, sub - 1, 0)
        p1 = jnp.where(row == sub - 1, next8[0:1], p1)
        p2 = pltpu.roll(cur, sub - 2, 0)
        p2 = jnp.where(row == sub - 2, next8[0:1], jnp.where(row == sub - 1, next8[1:2], p2))
        out = cw[0:1] * m2
        out = out + cw[1:2] * m1
        out = out + cw[2:3] * cur
        out = out + cw[3:4] * p1
        out = out + cw[4:5] * p2
        out = out + cb
        return out * _sigmoid(out)

    def emit(r, res):
        if transpose_out:
            o_ref[:, r * sub:(r + 1) * sub] = res.T.astype(o_ref.dtype)
        else:
            o_ref[r * sub:(r + 1) * sub, :] = res.astype(o_ref.dtype)

    prev8 = jnp.where(at_start, 0.0, _dot(hp_ref[...], w))[8:16]
    blocks = [_dot(h_ref[0:sub, :], w)]
    for r in range(nsub):
        if r + 1 < nsub:
            blocks.append(_dot(h_ref[(r + 1) * sub:(r + 2) * sub, :], w))
            next8 = blocks[r + 1][0:8]
        else:
            next8 = jnp.where(at_end, 0.0, _dot(hn_ref[...], w))[0:8]
        emit(r, conv(blocks[r], prev8, next8))
        prev8 = blocks[r][sub - 8:sub]


def matmul_conv(h, w, col0, n, conv_w, conv_b, seq_len, transpose_out, bm=1024, bn=1024, sub=256):
    m, k = h.shape
    assert seq_len % bm == 0 and bm % sub == 0
    nhb = m // 16
    if transpose_out:
        out_spec = pl.BlockSpec((bn, bm), lambda j, i: (j, i))
        out_shape = jax.ShapeDtypeStruct((n, m), BF16)
    else:
        out_spec = pl.BlockSpec((bm, bn), lambda j, i: (i, j))
        out_shape = jax.ShapeDtypeStruct((m, n), BF16)
    return pl.pallas_call(
        functools.partial(_matmul_conv_kernel, bm=bm, seq_len=seq_len, transpose_out=transpose_out,
                          sub=sub),
        grid=(n // bn, m // bm),
        in_specs=[pl.BlockSpec((bm, k), lambda j, i: (i, 0)),
                  pl.BlockSpec((16, k), lambda j, i: (jnp.maximum(i * (bm // 16) - 1, 0), 0)),
                  pl.BlockSpec((16, k), lambda j, i: (jnp.minimum((i + 1) * (bm // 16), nhb - 1), 0)),
                  pl.BlockSpec((pl.Element(k), pl.Element(bn)),
                               lambda j, i: (0, pl.multiple_of(col0 + j * bn, 128))),
                  pl.BlockSpec((D_CONV, bn), lambda j, i: (0, j)),
                  pl.BlockSpec((1, bn), lambda j, i: (0, j))],
        out_specs=out_spec,
        out_shape=out_shape,
        compiler_params=_cparams(("parallel", "parallel")),
        name="matmul_conv_t" if transpose_out else "matmul_conv",
    )(h, h, h, w, conv_w, conv_b)


def _ssd_kernel(*refs, backward, cb):
    if backward:
        (xsT_ref, b_ref, c_ref, nat_ref, tr_ref, yf_ref, z_ref, nw_ref, o_ref, s_ref) = refs
    else:
        (xsT_ref, b_ref, c_ref, nat_ref, tr_ref, dexp_ref, o_ref, s_ref) = refs
    G, R, P, N = SSD_GROUPS, 8, SSD_HEAD_DIM, SSD_STATE
    gw = R * P
    h0 = G * R if backward else 0

    @pl.when(pl.program_id(1) == 0)
    def _():
        s_ref[...] = jnp.zeros_like(s_ref)

    s_i = lax.broadcasted_iota(jnp.int32, (CHUNK, CHUNK), 0)
    l_i = lax.broadcasted_iota(jnp.int32, (CHUNK, CHUNK), 1)
    mask = (l_i <= s_i) if backward else (l_i >= s_i)
    chunk_order = range(cb - 1, -1, -1) if backward else range(cb)
    for j, g in [(j, g) for j in chunk_order for g in range(G)]:
        ts = slice(j * CHUNK, (j + 1) * CHUNK)
        gs = slice(g * gw, (g + 1) * gw)
        hg = h0 + g * R
        bm = b_ref[ts, g * N:(g + 1) * N]
        cm = c_ref[ts, g * N:(g + 1) * N]
        dt = tr_ref[hg:hg + R, ts]
        acum = tr_ref[128 + hg:128 + hg + R, ts]
        tot = tr_ref[256 + hg:256 + hg + R, ts]
        cbT = _dot_nt(bm, cm)
        s_old = s_ref[gs, :]
        yoffT = _dot_nt(s_old.astype(BF16), cm)
        e_acum = jnp.exp(acum)
        dte = jnp.exp(tot - acum)
        e_tot = jnp.exp(tot)
        y_pieces = []
        xd_pieces = []
        for r in range(R):
            hs = slice(g * gw + r * P, g * gw + (r + 1) * P)
            xr = xsT_ref[hs, ts].astype(F32)
            xdt = xr * dt[r:r + 1, :]
            col = nat_ref[ts, 128 + hg + r:128 + hg + r + 1]
            seg = acum[r:r + 1, :] - col
            dec = jnp.exp(jnp.where(mask, seg, -jnp.inf))
            mt = (cbT * dec).astype(BF16)
            y_r = _dot(xdt.astype(BF16), mt) + yoffT[r * P:(r + 1) * P, :] * e_acum[r:r + 1, :]
            if not backward:
                y_r = y_r + dexp_ref[hs, :] * xr
            y_pieces.append(y_r)
            xd_pieces.append((xdt * dte[r:r + 1, :]).astype(BF16))
        yT = jnp.concatenate(y_pieces, axis=0)
        upd = _dot(jnp.concatenate(xd_pieces, axis=0), bm)
        for r in range(R):
            hs = slice(r * P, (r + 1) * P)
            s_ref[g * gw + r * P:g * gw + (r + 1) * P, :] = s_old[hs, :] * e_tot[r:r + 1, :] + upd[hs, :]
        y = yT.T
        if backward:
            y = y + yf_ref[ts, gs]
            z = z_ref[ts, gs].astype(F32)
            y = y * (z * _sigmoid(z))
            y = y * lax.rsqrt(jnp.mean(y * y, axis=-1, keepdims=True) + EPS)
            o_ref[ts, gs] = (y * nw_ref[:, gs]).astype(o_ref.dtype)
        else:
            o_ref[ts, gs] = y.astype(o_ref.dtype)


def ssd_scan(xsT, bc, ssd, ssdT, batch, seq_len, *, backward, dexp=None,
             y_fwd=None, proj=None, norm_w=None):
    m = xsT.shape[1]
    cb = CHUNKS_PER_STEP
    tb = CHUNK * cb
    nc = seq_len // tb
    G = SSD_GROUPS
    d_ssd = xsT.shape[0]

    def cg(b, c):
        return b * nc + ((nc - 1 - c) if backward else c)

    in_specs = [
        pl.BlockSpec((d_ssd, tb), lambda b, c: (0, cg(b, c))),
        pl.BlockSpec((tb, G * SSD_STATE), lambda b, c: (cg(b, c), 0)),
        pl.BlockSpec((tb, G * SSD_STATE), lambda b, c: (cg(b, c), 1)),
        pl.BlockSpec((tb, 384), lambda b, c: (cg(b, c), 0)),
        pl.BlockSpec((384, tb), lambda b, c: (0, cg(b, c))),
    ]
    args = [xsT, bc, bc, ssd, ssdT]
    if backward:
        in_specs += [
            pl.BlockSpec((tb, d_ssd), lambda b, c: (cg(b, c), 0)),
            pl.BlockSpec((tb, d_ssd), lambda b, c: (cg(b, c), 0)),
            pl.BlockSpec((1, d_ssd), lambda b, c: (0, 0)),
        ]
        args += [y_fwd, proj, norm_w]
    else:
        in_specs += [pl.BlockSpec((d_ssd, 128), lambda b, c: (0, 0))]
        args += [dexp]
    out_dtype = BF16
    return pl.pallas_call(
        functools.partial(_ssd_kernel, backward=backward, cb=cb),
        grid=(batch, nc),
        in_specs=in_specs,
        out_specs=pl.BlockSpec((tb, d_ssd), lambda b, c: (cg(b, c), 0)),
        out_shape=jax.ShapeDtypeStruct((m, d_ssd), out_dtype),
        scratch_shapes=[pltpu.VMEM((d_ssd, SSD_STATE), F32)],
        compiler_params=_cparams(("parallel", "arbitrary")),
        name="ssd_bwd" if backward else "ssd_fwd",
    )(*args)


def _mlstm_kernel(*refs, backward, dk, dv, heads_per_group, cb):
    if backward:
        (q_ref, k_ref, v_ref, nat_ref, tr_ref, hf_ref, og_ref, nw_ref, o_ref, c_ref, m_ref) = refs
    else:
        (q_ref, k_ref, v_ref, nat_ref, tr_ref, o_ref, c_ref, m_ref) = refs
    H = MLSTM_HEADS

    @pl.when(pl.program_id(1) == 0)
    def _():
        c_ref[...] = jnp.zeros_like(c_ref)
        m_ref[...] = jnp.zeros_like(m_ref)

    t_i = lax.broadcasted_iota(jnp.int32, (CHUNK, CHUNK), 0)
    s_i = lax.broadcasted_iota(jnp.int32, (CHUNK, CHUNK), 1)
    mask = (s_i >= t_i) if backward else (s_i <= t_i)
    ones_blk = jnp.ones((CHUNK, 128), BF16)
    scale = dk ** -0.5
    dense = (CHUNK, CHUNK)

    chunk_order = range(cb - 1, -1, -1) if backward else range(cb)
    for j, g0 in [(j, g0) for j in chunk_order for g0 in range(0, H, heads_per_group)]:
        ts = slice(j * CHUNK, (j + 1) * CHUNK)
        hs = list(range(g0, g0 + heads_per_group))
        lane_of = {h: h + (H if backward else 0) for h in hs}
        st = {h: {} for h in hs}
        for h in hs:
            d, hh = st[h], lane_of[h]
            li_row = tr_ref[hh:hh + 1, ts]
            cum_row = tr_ref[160 + hh:161 + hh, ts]
            d["tot"] = tr_ref[288 + hh:289 + hh, ts]
            d["base_row"] = cum_row - li_row
            grow = d["tot"] - d["base_row"]
            d["m_in"] = m_ref[h, 0:1, :]
            m_loc = jnp.broadcast_to(jnp.max(grow, axis=1, keepdims=True), (1, CHUNK))
            d["m_new"] = jnp.maximum(d["tot"] + d["m_in"], m_loc)
            d["w_row"] = jnp.exp(grow - d["m_new"])
            d["cum_d"] = jnp.broadcast_to(nat_ref[ts, 160 + hh:161 + hh], dense)
            d["rmax_d"] = jnp.broadcast_to(nat_ref[ts, 416 + hh:417 + hh], dense)
            d["v_aug"] = jnp.concatenate([v_ref[ts, h * dv:(h + 1) * dv].astype(BF16), ones_blk], axis=1)
        for h in hs:
            d = st[h]
            k = k_ref[ts, h * dk:(h + 1) * dk].astype(F32)
            d["kb"] = k.astype(BF16)
            d["kwT"] = (k.T * d["w_row"]).astype(BF16)
            d["qs"] = (q_ref[ts, h * dk:(h + 1) * dk].astype(F32) * scale).astype(BF16)
        for h in hs:
            d = st[h]
            d["c_loc"] = _dot(d["kwT"], d["v_aug"])
            d["sqk"] = _dot_nt(d["qs"], d["kb"])
            d["c_in"] = c_ref[h]
            d["qc"] = _dot(d["qs"], d["c_in"].astype(BF16))
        for h in hs:
            d = st[h]
            dlog = jnp.where(mask, d["cum_d"] - d["base_row"], -jnp.inf)
            inter = d["cum_d"] + d["m_in"]
            d["m_t"] = jnp.maximum(d["rmax_d"], inter)
            d["pm"] = (jnp.exp(dlog - d["m_t"]) * d["sqk"]).astype(BF16)
            d["a_inter"] = jnp.exp(inter - d["m_t"])
        for h in hs:
            d = st[h]
            a3 = jnp.concatenate([d["a_inter"]] * (dv // 128 + 1), axis=1)
            num = _dot(d["pm"], d["v_aug"]) + d["qc"] * a3
            den = num[:, dv:dv + 128]
            inv = 1.0 / jnp.maximum(jnp.abs(den), jnp.exp(-d["m_t"]))
            d["hout"] = num[:, 0:dv] * jnp.concatenate([inv] * (dv // 128), axis=1)
        for h in hs:
            d = st[h]
            a_prev = jnp.exp(d["tot"] + d["m_in"] - d["m_new"])
            a_prev3 = jnp.concatenate([a_prev] * (dv // 128 + 1), axis=1)
            c_ref[h] = a_prev3 * d["c_in"] + d["c_loc"]
            m_ref[h] = jnp.broadcast_to(d["m_new"], (8, 128))
        for h in hs:
            hout = st[h]["hout"]
            vs = slice(h * dv, (h + 1) * dv)
            if backward:
                hout = hout + hf_ref[ts, vs]
                hout = hout * lax.rsqrt(jnp.mean(hout * hout, axis=-1, keepdims=True) + EPS)
                hout = hout * nw_ref[:, vs]
                o_ref[ts, vs] = (_sigmoid(og_ref[ts, vs].astype(F32)) * hout).astype(o_ref.dtype)
            else:
                o_ref[ts, vs] = hout.astype(o_ref.dtype)


def mlstm_scan(proj, ml, mlT, batch, seq_len, cols, *, backward, h_fwd=None, norm_w=None):
    m = proj.shape[0]
    cb = CHUNKS_PER_STEP
    tb = CHUNK * cb
    nc = seq_len // tb
    H = MLSTM_HEADS
    dk, dv = cols["dk"], cols["dv"]
    qw, vw = H * dk, H * dv
    qb, kb, vb, ob = cols["q"] // qw, cols["k"] // qw, cols["v"] // vw, cols["o"] // vw
    assert qb * qw == cols["q"] and kb * qw == cols["k"] and vb * vw == cols["v"] and ob * vw == cols["o"]

    def cg(b, c):
        return b * nc + ((nc - 1 - c) if backward else c)

    in_specs = [
        pl.BlockSpec((tb, qw), lambda b, c: (cg(b, c), qb)),
        pl.BlockSpec((tb, qw), lambda b, c: (cg(b, c), kb)),
        pl.BlockSpec((tb, vw), lambda b, c: (cg(b, c), vb)),
        pl.BlockSpec((tb, 512), lambda b, c: (cg(b, c), 0)),
        pl.BlockSpec((384, tb), lambda b, c: (0, cg(b, c))),
    ]
    args = [proj, proj, proj, ml, mlT]
    if backward:
        in_specs += [
            pl.BlockSpec((tb, vw), lambda b, c: (cg(b, c), 0)),
            pl.BlockSpec((tb, vw), lambda b, c: (cg(b, c), ob)),
            pl.BlockSpec((1, vw), lambda b, c: (0, 0)),
        ]
        args += [h_fwd, proj, norm_w]
    out_dtype = BF16
    heads_per_group = 2 if backward else 4
    return pl.pallas_call(
        functools.partial(_mlstm_kernel, backward=backward, dk=dk, dv=dv,
                          heads_per_group=heads_per_group, cb=cb),
        grid=(batch, nc),
        in_specs=in_specs,
        out_specs=pl.BlockSpec((tb, vw), lambda b, c: (cg(b, c), 0)),
        out_shape=jax.ShapeDtypeStruct((m, vw), out_dtype),
        scratch_shapes=[pltpu.VMEM((H, dk, dv + 128), F32), pltpu.VMEM((H, 8, 128), F32)],
        compiler_params=_cparams(("parallel", "arbitrary")),
        name="mlstm_bwd" if backward else "mlstm_fwd",
    )(*args)


def _outproj_kernel(a1_ref, a2_ref, w1_ref, w2_ref, x_ref, nw_ref, o_ref, xw_ref, ssq_ref):
    acc = _dot(a1_ref[...], w1_ref[...]) + _dot(a2_ref[...], w2_ref[...])
    x1 = x_ref[...] + acc
    o_ref[...] = x1
    xw_ref[...] = (x1 * nw_ref[...]).astype(xw_ref.dtype)
    ssq_ref[...] = jnp.broadcast_to(jnp.sum(x1 * x1, axis=-1, keepdims=True), ssq_ref.shape)


def outproj_residual(a1, a2, w1, w2, x, norm_w, bm=512, bn=512):
    m, k = a1.shape
    n = w1.shape[1]
    return pl.pallas_call(
        _outproj_kernel,
        grid=(n // bn, m // bm),
        in_specs=[pl.BlockSpec((bm, k), lambda j, i: (i, 0)),
                  pl.BlockSpec((bm, k), lambda j, i: (i, 0)),
                  pl.BlockSpec((k, bn), lambda j, i: (0, j)),
                  pl.BlockSpec((k, bn), lambda j, i: (0, j)),
                  pl.BlockSpec((bm, bn), lambda j, i: (i, j)),
                  pl.BlockSpec((1, bn), lambda j, i: (0, j))],
        out_specs=[pl.BlockSpec((bm, bn), lambda j, i: (i, j)),
                   pl.BlockSpec((bm, bn), lambda j, i: (i, j)),
                   pl.BlockSpec((bm, 128), lambda j, i: (i, j))],
        out_shape=[jax.ShapeDtypeStruct((m, n), F32), jax.ShapeDtypeStruct((m, n), BF16),
                   jax.ShapeDtypeStruct((m, (n // bn) * 128), F32)],
        compiler_params=_cparams(("parallel", "parallel")),
        name="outproj_residual",
    )(a1, a2, w1, w2, x, norm_w.reshape(1, n).astype(F32))


def _prep_layer(norm1_w, w_in, conv_w, conv_b, dt_bias, a_log, d_skip, ssd_norm_w, b_i, b_f,
                mlstm_norm_w, w_out, norm2_w, w_up, w_down):
    d_model = w_in.shape[0]
    d_mix = w_out.shape[0]
    d_ssd = d_mix // 2
    d_ml = d_mix - d_ssd
    n_ssd_heads = d_ssd // SSD_HEAD_DIM
    xbc_w = d_ssd + 2 * SSD_GROUPS * SSD_STATE
    dv = d_ml // MLSTM_HEADS
    dk = dv // 2
    widths = (d_ssd, xbc_w, 2 * n_ssd_heads, MLSTM_HEADS * dk, MLSTM_HEADS * dk, d_ml, d_ml,
              2 * MLSTM_HEADS, 2 * MLSTM_HEADS)
    offs = [0]
    for wd in widths:
        offs.append(offs[-1] + wd)
    assert offs[-1] == w_in.shape[1]
    assert 2 * n_ssd_heads == 128 and 2 * MLSTM_HEADS == 32
    w_bf = cast_rows(w_in, BF16)
    seg = lambda i: w_bf[:, offs[i]:offs[i + 1]]
    w_gate = jnp.concatenate([seg(2), seg(7), seg(8), jnp.zeros((d_model, 64), BF16)], axis=1)
    gate_bias = jnp.concatenate([dt_bias.reshape(-1), b_i.reshape(-1), b_f.reshape(-1),
                                 jnp.zeros((64,), F32)]).astype(F32).reshape(1, 256)
    cols = {"q": 0, "k": MLSTM_HEADS * dk, "v": 2 * MLSTM_HEADS * dk}
    cols["o"] = cols["v"] + d_ml
    cols["dk"], cols["dv"], cols["d_ssd"] = dk, dv, d_ssd
    return dict(
        norm1_w=norm1_w, w_bf=w_bf, offs=offs, w_gate=w_gate, gate_bias=gate_bias,
        alog=a_log.reshape(1, 128).astype(F32),
        conv_w=conv_w.astype(F32), conv_b=conv_b.reshape(1, -1).astype(F32),
        dexp=jnp.broadcast_to(jnp.repeat(d_skip.astype(F32), SSD_HEAD_DIM)[:, None], (d_ssd, 128)),
        ssd_norm_w=ssd_norm_w.reshape(1, -1).astype(F32),
        mlstm_norm_w=mlstm_norm_w.reshape(1, -1).astype(F32),
        w_out1=w_out[:d_ssd].astype(BF16), w_out2=w_out[d_ssd:].astype(BF16),
        norm2_w=norm2_w, w_up=w_up.astype(BF16), w_down=w_down.astype(BF16), cols=cols)


def _layer(x, p, batch, seq_len):
    cols = p["cols"]
    d_ssd = cols["d_ssd"]
    h = rmsnorm_rows(x, p["norm1_w"], BF16)
    offs = p["offs"]
    proj_z = matmul(h, p["w_bf"], BF16, col0=offs[0], n=d_ssd)
    proj_b = matmul(h, p["w_bf"], BF16, col0=offs[3], n=offs[7] - offs[3])
    ssd, ssdT, ml, mlT = gates(h, p["w_gate"], p["gate_bias"], p["alog"])
    xsT = matmul_conv(h, p["w_bf"], offs[1], d_ssd, p["conv_w"][:, :d_ssd], p["conv_b"][:, :d_ssd],
                      seq_len, True)
    bc = matmul_conv(h, p["w_bf"], offs[1] + d_ssd, 2 * SSD_GROUPS * SSD_STATE,
                     p["conv_w"][:, d_ssd:], p["conv_b"][:, d_ssd:], seq_len, False)
    y_f = ssd_scan(xsT, bc, ssd, ssdT, batch, seq_len, backward=False, dexp=p["dexp"])
    mix1 = ssd_scan(xsT, bc, ssd, ssdT, batch, seq_len, backward=True, y_fwd=y_f,
                    proj=proj_z, norm_w=p["ssd_norm_w"])
    h_f = mlstm_scan(proj_b, ml, mlT, batch, seq_len, cols, backward=False)
    mix2 = mlstm_scan(proj_b, ml, mlT, batch, seq_len, cols, backward=True, h_fwd=h_f,
                      norm_w=p["mlstm_norm_w"])
    x1, x1w, ssq = outproj_residual(mix1, mix2, p["w_out1"], p["w_out2"], x, p["norm2_w"])
    u = matmul(x1w, p["w_up"], BF16, relu2=True, row_ssq=ssq)
    return matmul_ksplit_residual(u, p["w_down"], x1)


def _trunk(x, layers, final_norm_w):
    batch, seq_len, d = x.shape
    xf = x.reshape(batch * seq_len, d)
    for p in layers:
        xf = _layer(xf, p, batch, seq_len)
    return rmsnorm_rows(xf, final_norm_w, F32).reshape(batch, seq_len, d)


def kernel(x_prompt, x_sample, norm1_w, w_in, conv_w, conv_b, dt_bias, a_log, d_skip, ssd_norm_w,
           b_i, b_f, mlstm_norm_w, w_out, norm2_w, w_up, w_down, final_norm_w):
    depth = w_in.shape[0]
    layers = [_prep_layer(norm1_w[l], w_in[l], conv_w[l], conv_b[l], dt_bias[l], a_log[l], d_skip[l],
                          ssd_norm_w[l], b_i[l], b_f[l], mlstm_norm_w[l], w_out[l], norm2_w[l],
                          w_up[l], w_down[l]) for l in range(depth)]
    y_prompt = _trunk(x_prompt, layers, final_norm_w)
    y_sample = _trunk(x_sample, layers, final_norm_w)
    return (y_prompt, y_sample)
```

```python
import functools

import jax
import jax.numpy as jnp
from jax import lax
from jax.experimental import pallas as pl
from jax.experimental.pallas import tpu as pltpu

F32 = jnp.float32
BF16 = jnp.bfloat16

CHUNK = 128
CHUNKS_PER_STEP = 2
EPS = 1e-5
D_CONV = 5
SSD_GROUPS = 8
SSD_HEAD_DIM = 64
SSD_STATE = 128
MLSTM_HEADS = 16
VMEM_LIMIT = 56 * 1024 * 1024
KSPLIT_VMEM_LIMIT = 60 * 1024 * 1024


def _cparams(sem, vmem_limit=VMEM_LIMIT):
    return pltpu.CompilerParams(dimension_semantics=sem, vmem_limit_bytes=vmem_limit)


def _sigmoid(x):
    return 1.0 / (1.0 + jnp.exp(-x))


def _softplus(x):
    return jnp.maximum(x, 0.0) + jnp.log1p(jnp.exp(-jnp.abs(x)))


def _dot(a, b):
    return jnp.dot(a, b, preferred_element_type=F32)


def _dot_nt(a, b):
    return lax.dot_general(a, b, (((1,), (1,)), ((), ())), preferred_element_type=F32)


def _rmsnorm_kernel(x_ref, w_ref, o_ref):
    x = x_ref[...].astype(F32)
    y = x * lax.rsqrt(jnp.mean(x * x, axis=-1, keepdims=True) + EPS)
    o_ref[...] = (y * w_ref[...]).astype(o_ref.dtype)


def rmsnorm_rows(x, w, out_dtype, bm=512):
    m, d = x.shape
    return pl.pallas_call(
        _rmsnorm_kernel,
        grid=(m // bm,),
        in_specs=[pl.BlockSpec((bm, d), lambda i: (i, 0)),
                  pl.BlockSpec((1, d), lambda i: (0, 0))],
        out_specs=pl.BlockSpec((bm, d), lambda i: (i, 0)),
        out_shape=jax.ShapeDtypeStruct((m, d), out_dtype),
        compiler_params=_cparams(("parallel",)),
        name="rmsnorm_rows",
    )(x, w.reshape(1, d).astype(F32))


def _matmul_kernel(*refs, relu2, row_ssq_dim):
    if row_ssq_dim:
        a_ref, b_ref, ssq_ref, o_ref = refs
    else:
        a_ref, b_ref, o_ref = refs
    acc = _dot(a_ref[...], b_ref[...])
    if relu2:
        acc = jnp.maximum(acc, 0.0)
        acc = acc * acc
    if row_ssq_dim:
        ssq = ssq_ref[...]
        tot = ssq[:, 0:128]
        for part in range(1, ssq.shape[1] // 128):
            tot = tot + ssq[:, part * 128:(part + 1) * 128]
        r2 = 1.0 / (tot * (1.0 / row_ssq_dim) + EPS)
        acc = acc * jnp.concatenate([r2] * (acc.shape[1] // 128), axis=1)
    o_ref[...] = acc.astype(o_ref.dtype)


def matmul(a, b, out_dtype, bm=1024, bn=1024, relu2=False, col0=0, n=None, row_ssq=None):
    m, k = a.shape
    n = b.shape[1] if n is None else n
    in_specs = [pl.BlockSpec((bm, k), lambda j, i: (i, 0)),
                pl.BlockSpec((pl.Element(k), pl.Element(bn)),
                             lambda j, i: (0, pl.multiple_of(col0 + j * bn, 128)))]
    args = [a, b]
    if row_ssq is not None:
        assert relu2
        in_specs.append(pl.BlockSpec((bm, row_ssq.shape[1]), lambda j, i: (i, 0)))
        args.append(row_ssq)
    return pl.pallas_call(
        functools.partial(_matmul_kernel, relu2=relu2, row_ssq_dim=k if row_ssq is not None else 0),
        grid=(n // bn, m // bm),
        in_specs=in_specs,
        out_specs=pl.BlockSpec((bm, bn), lambda j, i: (i, j)),
        out_shape=jax.ShapeDtypeStruct((m, n), out_dtype),
        compiler_params=_cparams(("parallel", "parallel")),
        name="matmul_relu2" if relu2 else "matmul",
    )(*args)


def _matmul_ksplit_res_kernel(a_ref, b_ref, x_ref, o_ref):
    @pl.when(pl.program_id(2) == 0)
    def _():
        o_ref[...] = x_ref[...] + _dot(a_ref[...], b_ref[...])

    @pl.when(pl.program_id(2) != 0)
    def _():
        o_ref[...] = o_ref[...] + _dot(a_ref[...], b_ref[...])


def matmul_ksplit_residual(a, b, x, bm=1024, bn=1024, bk=4096):
    m, k = a.shape
    n = b.shape[1]
    return pl.pallas_call(
        _matmul_ksplit_res_kernel,
        grid=(n // bn, m // bm, k // bk),
        in_specs=[pl.BlockSpec((bm, bk), lambda j, i, kk: (i, kk)),
                  pl.BlockSpec((bk, bn), lambda j, i, kk: (kk, j)),
                  pl.BlockSpec((bm, bn), lambda j, i, kk: (i, j))],
        out_specs=pl.BlockSpec((bm, bn), lambda j, i, kk: (i, j)),
        out_shape=jax.ShapeDtypeStruct((m, n), F32),
        compiler_params=_cparams(("parallel", "parallel", "arbitrary"), KSPLIT_VMEM_LIMIT),
        name="matmul_ksplit_residual",
    )(a, b, x)


def _split3(x):
    hi = x.astype(BF16)
    r1 = x - hi.astype(F32)
    mid = r1.astype(BF16)
    r2 = r1 - mid.astype(F32)
    return hi, mid, r2.astype(BF16)


def _gates_kernel(h_ref, w_ref, bias_ref, alog_ref, ssd_ref, ssdT_ref, ml_ref, mlT_ref, *, bm):
    raw = _dot(h_ref[...], w_ref[...]) + bias_ref[...]
    dt = _softplus(raw[:, 0:128])
    a = dt * (-jnp.exp(alog_ref[...]))
    t2 = raw[:, 128:256]
    lane = lax.broadcasted_iota(jnp.int32, (CHUNK, 128), 1)
    row_t = lax.broadcasted_iota(jnp.int32, (CHUNK, 128), 0)
    lsig = -_softplus(-t2)
    row_i = lax.broadcasted_iota(jnp.int32, (CHUNK, CHUNK), 0)
    col_i = lax.broadcasted_iota(jnp.int32, (CHUNK, CHUNK), 1)
    lower = (col_i <= row_i).astype(BF16)
    upper = (col_i >= row_i).astype(BF16)
    ones = jnp.ones((CHUNK, CHUNK), BF16)
    ssd_fwd_lane = lane < 64
    ml_fwd_lane = (lane % 32) < 16

    def cums(x, fwd_lane):
        hi, mid, lo = _split3(x)
        cum_f = _dot(lower, hi) + _dot(lower, mid) + _dot(lower, lo)
        cum_b = _dot(upper, hi) + _dot(upper, mid) + _dot(upper, lo)
        tot = _dot(ones, hi) + _dot(ones, mid) + _dot(ones, lo)
        return jnp.where(fwd_lane, cum_f, cum_b), tot

    for c in range(bm // CHUNK):
        sl = slice(c * CHUNK, (c + 1) * CHUNK)
        dt_c = dt[sl]
        acum, tot = cums(a[sl], ssd_fwd_lane)
        ssd_ref[sl, 0:128] = dt_c
        ssd_ref[sl, 128:256] = acum
        ssd_ref[sl, 256:384] = tot
        ssdT_ref[0:128, sl] = dt_c.T
        ssdT_ref[128:256, sl] = acum.T
        ssdT_ref[256:384, sl] = tot.T
        t2_c = t2[sl]
        mcum, mtot = cums(lsig[sl], ml_fwd_lane)
        y = mcum - pltpu.roll(t2_c, 32, 1)
        y_f, y_b = y, y
        for d in (1, 2, 4, 8, 16, 32, 64):
            y_f = jnp.minimum(y_f, jnp.where(row_t >= d, pltpu.roll(y_f, d, 0), jnp.inf))
            y_b = jnp.minimum(y_b, jnp.where(row_t < CHUNK - d, pltpu.roll(y_b, CHUNK - d, 0), jnp.inf))
        rmax = mcum - jnp.where(ml_fwd_lane, y_f, y_b)
        ml_ref[sl, 0:128] = t2_c
        ml_ref[sl, 128:256] = mcum
        ml_ref[sl, 256:384] = mtot
        ml_ref[sl, 384:512] = rmax
        mlT_ref[0:128, sl] = t2_c.T
        mlT_ref[128:256, sl] = mcum.T
        mlT_ref[256:384, sl] = mtot.T


def gates(h, w_gate, bias, alog, bm=512):
    m, d = h.shape
    nat = pl.BlockSpec((bm, 384), lambda i: (i, 0))
    nat4 = pl.BlockSpec((bm, 512), lambda i: (i, 0))
    tr = pl.BlockSpec((384, bm), lambda i: (0, i))
    return pl.pallas_call(
        functools.partial(_gates_kernel, bm=bm),
        grid=(m // bm,),
        in_specs=[pl.BlockSpec((bm, d), lambda i: (i, 0)),
                  pl.BlockSpec((d, 256), lambda i: (0, 0)),
                  pl.BlockSpec((1, 256), lambda i: (0, 0)),
                  pl.BlockSpec((1, 128), lambda i: (0, 0))],
        out_specs=[nat, tr, nat4, tr],
        out_shape=[jax.ShapeDtypeStruct((m, 384), F32), jax.ShapeDtypeStruct((384, m), F32),
                   jax.ShapeDtypeStruct((m, 512), F32), jax.ShapeDtypeStruct((384, m), F32)],
        compiler_params=_cparams(("parallel",)),
        name="gates",
    )(h, w_gate, bias, alog)


def _matmul_conv_kernel(h_ref, hp_ref, hn_ref, w_ref, cw_ref, cb_ref, o_ref, *, bm, seq_len,
                        transpose_out, sub):
    i = pl.program_id(1)
    w = w_ref[...]
    at_start = (i * bm) % seq_len == 0
    at_end = ((i + 1) * bm) % seq_len == 0
    cw = cw_ref[...]
    cb = cb_ref[...]
    nsub = bm // sub
    row = lax.broadcasted_iota(jnp.int32, (sub, w.shape[1]), 0)

    def conv(cur, prev8, next8):
        m2 = pltpu.roll(cur, 2, 0)
        m2 = jnp.where(row == 0, prev8[6:7], jnp.where(row == 1, prev8[7:8], m2))
        m1 = pltpu.roll(cur, 1, 0)
        m1 = jnp.where(row == 0, prev8[7:8], m1)
        p1 = pltpu.roll(cur, sub - 1, 0)
        p1 = jnp.where(row == sub - 1, next8[0:1], p1)
        p2 = pltpu.roll(cur, sub - 2, 0)
        p2 = jnp.where(row == sub - 2, next8[0:1], jnp.where(row == sub - 1, next8[1:2], p2))
        out = cw[0:1] * m2
        out = out + cw[1:2] * m1
        out = out + cw[2:3] * cur
        out = out + cw[3:4] * p1
        out = out + cw[4:5] * p2
        out = out + cb
        return out * _sigmoid(out)

    def emit(r, res):
        if transpose_out:
            o_ref[:, r * sub:(r + 1) * sub] = res.T.astype(o_ref.dtype)
        else:
            o_ref[r * sub:(r + 1) * sub, :] = res.astype(o_ref.dtype)

    prev8 = jnp.where(at_start, 0.0, _dot(hp_ref[...], w))[8:16]
    blocks = [_dot(h_ref[0:sub, :], w)]
    for r in range(nsub):
        if r + 1 < nsub:
            blocks.append(_dot(h_ref[(r + 1) * sub:(r + 2) * sub, :], w))
            next8 = blocks[r + 1][0:8]
        else:
            next8 = jnp.where(at_end, 0.0, _dot(hn_ref[...], w))[0:8]
        emit(r, conv(blocks[r], prev8, next8))
        prev8 = blocks[r][sub - 8:sub]


def matmul_conv(h, w, col0, n, conv_w, conv_b, seq_len, transpose_out, bm=1024, bn=1024, sub=256):
    m, k = h.shape
    assert seq_len % bm == 0 and bm % sub == 0
    nhb = m // 16
    if transpose_out:
        out_spec = pl.BlockSpec((bn, bm), lambda j, i: (j, i))
        out_shape = jax.ShapeDtypeStruct((n, m), BF16)
    else:
        out_spec = pl.BlockSpec((bm, bn), lambda j, i: (i, j))
        out_shape = jax.ShapeDtypeStruct((m, n), BF16)
    return pl.pallas_call(
        functools.partial(_matmul_conv_kernel, bm=bm, seq_len=seq_len, transpose_out=transpose_out,
                          sub=sub),
        grid=(n // bn, m // bm),
        in_specs=[pl.BlockSpec((bm, k), lambda j, i: (i, 0)),
                  pl.BlockSpec((16, k), lambda j, i: (jnp.maximum(i * (bm // 16) - 1, 0), 0)),
                  pl.BlockSpec((16, k), lambda j, i: (jnp.minimum((i + 1) * (bm // 16), nhb - 1), 0)),
                  pl.BlockSpec((pl.Element(k), pl.Element(bn)),
                               lambda j, i: (0, pl.multiple_of(col0 + j * bn, 128))),
                  pl.BlockSpec((D_CONV, bn), lambda j, i: (0, j)),
                  pl.BlockSpec((1, bn), lambda j, i: (0, j))],
        out_specs=out_spec,
        out_shape=out_shape,
        compiler_params=_cparams(("parallel", "parallel")),
        name="matmul_conv_t" if transpose_out else "matmul_conv",
    )(h, h, h, w, conv_w, conv_b)


def _ssd_kernel(*refs, backward, cb):
    if backward:
        (xsT_ref, b_ref, c_ref, nat_ref, tr_ref, yf_ref, z_ref, nw_ref, o_ref, s_ref) = refs
    else:
        (xsT_ref, b_ref, c_ref, nat_ref, tr_ref, dexp_ref, o_ref, s_ref) = refs
    G, R, P, N = SSD_GROUPS, 8, SSD_HEAD_DIM, SSD_STATE
    gw = R * P
    h0 = G * R if backward else 0

    @pl.when(pl.program_id(1) == 0)
    def _():
        s_ref[...] = jnp.zeros_like(s_ref)

    s_i = lax.broadcasted_iota(jnp.int32, (CHUNK, CHUNK), 0)
    l_i = lax.broadcasted_iota(jnp.int32, (CHUNK, CHUNK), 1)
    mask = (l_i <= s_i) if backward else (l_i >= s_i)
    chunk_order = range(cb - 1, -1, -1) if backward else range(cb)
    for j, g in [(j, g) for j in chunk_order for g in range(G)]:
        ts = slice(j * CHUNK, (j + 1) * CHUNK)
        gs = slice(g * gw, (g + 1) * gw)
        hg = h0 + g * R
        bm = b_ref[ts, g * N:(g + 1) * N]
        cm = c_ref[ts, g * N:(g + 1) * N]
        dt = tr_ref[hg:hg + R, ts]
        acum = tr_ref[128 + hg:128 + hg + R, ts]
        tot = tr_ref[256 + hg:256 + hg + R, ts]
        cbT = _dot_nt(bm, cm)
        s_old = s_ref[gs, :]
        yoffT = _dot_nt(s_old.astype(BF16), cm)
        e_acum = jnp.exp(acum)
        dte = jnp.exp(tot - acum)
        e_tot = jnp.exp(tot)
        y_pieces = []
        xd_pieces = []
        for r in range(R):
            hs = slice(g * gw + r * P, g * gw + (r + 1) * P)
            xr = xsT_ref[hs, ts].astype(F32)
            xdt = xr * dt[r:r + 1, :]
            col = nat_ref[ts, 128 + hg + r:128 + hg + r + 1]
            seg = acum[r:r + 1, :] - col
            dec = jnp.exp(jnp.where(mask, seg, -jnp.inf))
            mt = (cbT * dec).astype(BF16)
            y_r = _dot(xdt.astype(BF16), mt) + yoffT[r * P:(r + 1) * P, :] * e_acum[r:r + 1, :]
            if not backward:
                y_r = y_r + dexp_ref[hs, :] * xr
            y_pieces.append(y_r)
            xd_pieces.append((xdt * dte[r:r + 1, :]).astype(BF16))
        yT = jnp.concatenate(y_pieces, axis=0)
        upd = _dot(jnp.concatenate(xd_pieces, axis=0), bm)
        for r in range(R):
            hs = slice(r * P, (r + 1) * P)
            s_ref[g * gw + r * P:g * gw + (r + 1) * P, :] = s_old[hs, :] * e_tot[r:r + 1, :] + upd[hs, :]
        y = yT.T
        if backward:
            y = y + yf_ref[ts, gs]
            z = z_ref[ts, gs].astype(F32)
            y = y * (z * _sigmoid(z))
            y = y * lax.rsqrt(jnp.mean(y * y, axis=-1, keepdims=True) + EPS)
            o_ref[ts, gs] = (y * nw_ref[:, gs]).astype(o_ref.dtype)
        else:
            o_ref[ts, gs] = y


def ssd_scan(xsT, bc, ssd, ssdT, batch, seq_len, *, backward, dexp=None,
             y_fwd=None, proj=None, norm_w=None):
    m = xsT.shape[1]
    cb = CHUNKS_PER_STEP
    tb = CHUNK * cb
    nc = seq_len // tb
    G = SSD_GROUPS
    d_ssd = xsT.shape[0]

    def cg(b, c):
        return b * nc + ((nc - 1 - c) if backward else c)

    in_specs = [
        pl.BlockSpec((d_ssd, tb), lambda b, c: (0, cg(b, c))),
        pl.BlockSpec((tb, G * SSD_STATE), lambda b, c: (cg(b, c), 0)),
        pl.BlockSpec((tb, G * SSD_STATE), lambda b, c: (cg(b, c), 1)),
        pl.BlockSpec((tb, 384), lambda b, c: (cg(b, c), 0)),
        pl.BlockSpec((384, tb), lambda b, c: (0, cg(b, c))),
    ]
    args = [xsT, bc, bc, ssd, ssdT]
    if backward:
        in_specs += [
            pl.BlockSpec((tb, d_ssd), lambda b, c: (cg(b, c), 0)),
            pl.BlockSpec((tb, d_ssd), lambda b, c: (cg(b, c), 0)),
            pl.BlockSpec((1, d_ssd), lambda b, c: (0, 0)),
        ]
        args += [y_fwd, proj, norm_w]
        out_dtype = BF16
    else:
        in_specs += [pl.BlockSpec((d_ssd, 128), lambda b, c: (0, 0))]
        args += [dexp]
        out_dtype = F32
    return pl.pallas_call(
        functools.partial(_ssd_kernel, backward=backward, cb=cb),
        grid=(batch, nc),
        in_specs=in_specs,
        out_specs=pl.BlockSpec((tb, d_ssd), lambda b, c: (cg(b, c), 0)),
        out_shape=jax.ShapeDtypeStruct((m, d_ssd), out_dtype),
        scratch_shapes=[pltpu.VMEM((d_ssd, SSD_STATE), F32)],
        compiler_params=_cparams(("parallel", "arbitrary")),
        name="ssd_bwd" if backward else "ssd_fwd",
    )(*args)


def _mlstm_kernel(*refs, backward, dk, dv, heads_per_group, cb):
    if backward:
        (q_ref, k_ref, v_ref, nat_ref, tr_ref, hf_ref, og_ref, nw_ref, o_ref, c_ref, m_ref) = refs
    else:
        (q_ref, k_ref, v_ref, nat_ref, tr_ref, o_ref, c_ref, m_ref) = refs
    H = MLSTM_HEADS

    @pl.when(pl.program_id(1) == 0)
    def _():
        c_ref[...] = jnp.zeros_like(c_ref)
        m_ref[...] = jnp.zeros_like(m_ref)

    t_i = lax.broadcasted_iota(jnp.int32, (CHUNK, CHUNK), 0)
    s_i = lax.broadcasted_iota(jnp.int32, (CHUNK, CHUNK), 1)
    mask = (s_i >= t_i) if backward else (s_i <= t_i)
    ones_blk = jnp.ones((CHUNK, 128), BF16)
    scale = dk ** -0.5
    dense = (CHUNK, CHUNK)

    chunk_order = range(cb - 1, -1, -1) if backward else range(cb)
    for j, g0 in [(j, g0) for j in chunk_order for g0 in range(0, H, heads_per_group)]:
        ts = slice(j * CHUNK, (j + 1) * CHUNK)
        hs = list(range(g0, g0 + heads_per_group))
        lane_of = {h: h + (H if backward else 0) for h in hs}
        st = {h: {} for h in hs}
        for h in hs:
            d, hh = st[h], lane_of[h]
            li_row = tr_ref[hh:hh + 1, ts]
            cum_row = tr_ref[160 + hh:161 + hh, ts]
            d["tot"] = tr_ref[288 + hh:289 + hh, ts]
            d["base_row"] = cum_row - li_row
            grow = d["tot"] - d["base_row"]
            d["m_in"] = m_ref[h, 0:1, :]
            m_loc = jnp.broadcast_to(jnp.max(grow, axis=1, keepdims=True), (1, CHUNK))
            d["m_new"] = jnp.maximum(d["tot"] + d["m_in"], m_loc)
            d["w_row"] = jnp.exp(grow - d["m_new"])
            d["cum_d"] = jnp.broadcast_to(nat_ref[ts, 160 + hh:161 + hh], dense)
            d["rmax_d"] = jnp.broadcast_to(nat_ref[ts, 416 + hh:417 + hh], dense)
            d["v_aug"] = jnp.concatenate([v_ref[ts, h * dv:(h + 1) * dv].astype(BF16), ones_blk], axis=1)
        for h in hs:
            d = st[h]
            k = k_ref[ts, h * dk:(h + 1) * dk].astype(F32)
            d["kb"] = k.astype(BF16)
            d["kwT"] = (k.T * d["w_row"]).astype(BF16)
            d["qs"] = (q_ref[ts, h * dk:(h + 1) * dk].astype(F32) * scale).astype(BF16)
        for h in hs:
            d = st[h]
            d["c_loc"] = _dot(d["kwT"], d["v_aug"])
            d["sqk"] = _dot_nt(d["qs"], d["kb"])
            d["c_in"] = c_ref[h]
            d["qc"] = _dot(d["qs"], d["c_in"].astype(BF16))
        for h in hs:
            d = st[h]
            dlog = jnp.where(mask, d["cum_d"] - d["base_row"], -jnp.inf)
            inter = d["cum_d"] + d["m_in"]
            d["m_t"] = jnp.maximum(d["rmax_d"], inter)
            d["pm"] = (jnp.exp(dlog - d["m_t"]) * d["sqk"]).astype(BF16)
            d["a_inter"] = jnp.exp(inter - d["m_t"])
        for h in hs:
            d = st[h]
            a3 = jnp.concatenate([d["a_inter"]] * (dv // 128 + 1), axis=1)
            num = _dot(d["pm"], d["v_aug"]) + d["qc"] * a3
            den = num[:, dv:dv + 128]
            inv = 1.0 / jnp.maximum(jnp.abs(den), jnp.exp(-d["m_t"]))
            d["hout"] = num[:, 0:dv] * jnp.concatenate([inv] * (dv // 128), axis=1)
        for h in hs:
            d = st[h]
            a_prev = jnp.exp(d["tot"] + d["m_in"] - d["m_new"])
            a_prev3 = jnp.concatenate([a_prev] * (dv // 128 + 1), axis=1)
            c_ref[h] = a_prev3 * d["c_in"] + d["c_loc"]
            m_ref[h] = jnp.broadcast_to(d["m_new"], (8, 128))
        for h in hs:
            hout = st[h]["hout"]
            vs = slice(h * dv, (h + 1) * dv)
            if backward:
                hout = hout + hf_ref[ts, vs]
                hout = hout * lax.rsqrt(jnp.mean(hout * hout, axis=-1, keepdims=True) + EPS)
                hout = hout * nw_ref[:, vs]
                o_ref[ts, vs] = (_sigmoid(og_ref[ts, vs].astype(F32)) * hout).astype(o_ref.dtype)
            else:
                o_ref[ts, vs] = hout


def mlstm_scan(proj, ml, mlT, batch, seq_len, cols, *, backward, h_fwd=None, norm_w=None):
    m = proj.shape[0]
    cb = CHUNKS_PER_STEP
    tb = CHUNK * cb
    nc = seq_len // tb
    H = MLSTM_HEADS
    dk, dv = cols["dk"], cols["dv"]
    qw, vw = H * dk, H * dv
    qb, kb, vb, ob = cols["q"] // qw, cols["k"] // qw, cols["v"] // vw, cols["o"] // vw
    assert qb * qw == cols["q"] and kb * qw == cols["k"] and vb * vw == cols["v"] and ob * vw == cols["o"]

    def cg(b, c):
        return b * nc + ((nc - 1 - c) if backward else c)

    in_specs = [
        pl.BlockSpec((tb, qw), lambda b, c: (cg(b, c), qb)),
        pl.BlockSpec((tb, qw), lambda b, c: (cg(b, c), kb)),
        pl.BlockSpec((tb, vw), lambda b, c: (cg(b, c), vb)),
        pl.BlockSpec((tb, 512), lambda b, c: (cg(b, c), 0)),
        pl.BlockSpec((384, tb), lambda b, c: (0, cg(b, c))),
    ]
    args = [proj, proj, proj, ml, mlT]
    if backward:
        in_specs += [
            pl.BlockSpec((tb, vw), lambda b, c: (cg(b, c), 0)),
            pl.BlockSpec((tb, vw), lambda b, c: (cg(b, c), ob)),
            pl.BlockSpec((1, vw), lambda b, c: (0, 0)),
        ]
        args += [h_fwd, proj, norm_w]
        out_dtype = BF16
    else:
        out_dtype = F32
    return pl.pallas_call(
        functools.partial(_mlstm_kernel, backward=backward, dk=dk, dv=dv,
                          heads_per_group=8 if backward else 4, cb=cb),
        grid=(batch, nc),
        in_specs=in_specs,
        out_specs=pl.BlockSpec((tb, vw), lambda b, c: (cg(b, c), 0)),
        out_shape=jax.ShapeDtypeStruct((m, vw), out_dtype),
        scratch_shapes=[pltpu.VMEM((H, dk, dv + 128), F32), pltpu.VMEM((H, 8, 128), F32)],
        compiler_params=_cparams(("parallel", "arbitrary")),
        name="mlstm_bwd" if backward else "mlstm_fwd",
    )(*args)


def _outproj_kernel(a1_ref, a2_ref, w1_ref, w2_ref, x_ref, nw_ref, o_ref, xw_ref, ssq_ref):
    acc = _dot(a1_ref[...], w1_ref[...]) + _dot(a2_ref[...], w2_ref[...])
    x1 = x_ref[...] + acc
    o_ref[...] = x1
    xw_ref[...] = (x1 * nw_ref[...]).astype(xw_ref.dtype)
    ssq_ref[...] = jnp.broadcast_to(jnp.sum(x1 * x1, axis=-1, keepdims=True), ssq_ref.shape)


def outproj_residual(a1, a2, w1, w2, x, norm_w, bm=1024, bn=512):
    m, k = a1.shape
    n = w1.shape[1]
    once = pl.Buffered(1)
    return pl.pallas_call(
        _outproj_kernel,
        grid=(n // bn, m // bm),
        in_specs=[pl.BlockSpec((bm, k), lambda j, i: (i, 0)),
                  pl.BlockSpec((bm, k), lambda j, i: (i, 0)),
                  pl.BlockSpec((k, bn), lambda j, i: (0, j), pipeline_mode=once),
                  pl.BlockSpec((k, bn), lambda j, i: (0, j), pipeline_mode=once),
                  pl.BlockSpec((bm, bn), lambda j, i: (i, j)),
                  pl.BlockSpec((1, bn), lambda j, i: (0, j))],
        out_specs=[pl.BlockSpec((bm, bn), lambda j, i: (i, j)),
                   pl.BlockSpec((bm, bn), lambda j, i: (i, j)),
                   pl.BlockSpec((bm, 128), lambda j, i: (i, j))],
        out_shape=[jax.ShapeDtypeStruct((m, n), F32), jax.ShapeDtypeStruct((m, n), BF16),
                   jax.ShapeDtypeStruct((m, (n // bn) * 128), F32)],
        compiler_params=_cparams(("parallel", "parallel")),
        name="outproj_residual",
    )(a1, a2, w1, w2, x, norm_w.reshape(1, n).astype(F32))


def _prep_layer(norm1_w, w_in, conv_w, conv_b, dt_bias, a_log, d_skip, ssd_norm_w, b_i, b_f,
                mlstm_norm_w, w_out, norm2_w, w_up, w_down):
    d_model = w_in.shape[0]
    d_mix = w_out.shape[0]
    d_ssd = d_mix // 2
    d_ml = d_mix - d_ssd
    n_ssd_heads = d_ssd // SSD_HEAD_DIM
    xbc_w = d_ssd + 2 * SSD_GROUPS * SSD_STATE
    dv = d_ml // MLSTM_HEADS
    dk = dv // 2
    widths = (d_ssd, xbc_w, 2 * n_ssd_heads, MLSTM_HEADS * dk, MLSTM_HEADS * dk, d_ml, d_ml,
              2 * MLSTM_HEADS, 2 * MLSTM_HEADS)
    offs = [0]
    for wd in widths:
        offs.append(offs[-1] + wd)
    assert offs[-1] == w_in.shape[1]
    assert 2 * n_ssd_heads == 128 and 2 * MLSTM_HEADS == 32
    seg = lambda i: w_in[:, offs[i]:offs[i + 1]]
    w_bf = w_in.astype(BF16)
    w_gate = jnp.concatenate([seg(2), seg(7), seg(8), jnp.zeros((d_model, 64), w_in.dtype)],
                             axis=1).astype(BF16)
    gate_bias = jnp.concatenate([dt_bias.reshape(-1), b_i.reshape(-1), b_f.reshape(-1),
                                 jnp.zeros((64,), F32)]).astype(F32).reshape(1, 256)
    cols = {"q": 0, "k": MLSTM_HEADS * dk, "v": 2 * MLSTM_HEADS * dk}
    cols["o"] = cols["v"] + d_ml
    cols["dk"], cols["dv"], cols["d_ssd"] = dk, dv, d_ssd
    return dict(
        norm1_w=norm1_w, w_bf=w_bf, offs=offs, w_gate=w_gate, gate_bias=gate_bias,
        alog=a_log.reshape(1, 128).astype(F32),
        conv_w=conv_w.astype(F32), conv_b=conv_b.reshape(1, -1).astype(F32),
        dexp=jnp.broadcast_to(jnp.repeat(d_skip.astype(F32), SSD_HEAD_DIM)[:, None], (d_ssd, 128)),
        ssd_norm_w=ssd_norm_w.reshape(1, -1).astype(F32),
        mlstm_norm_w=mlstm_norm_w.reshape(1, -1).astype(F32),
        w_out1=w_out[:d_ssd].astype(BF16), w_out2=w_out[d_ssd:].astype(BF16),
        norm2_w=norm2_w, w_up=w_up.astype(BF16), w_down=w_down.astype(BF16), cols=cols)


def _layer(x, p, batch, seq_len):
    cols = p["cols"]
    d_ssd = cols["d_ssd"]
    h = rmsnorm_rows(x, p["norm1_w"], BF16)
    offs = p["offs"]
    proj_z = matmul(h, p["w_bf"], F32, col0=offs[0], n=d_ssd)
    proj_b = matmul(h, p["w_bf"], BF16, col0=offs[3], n=offs[7] - offs[3])
    ssd, ssdT, ml, mlT = gates(h, p["w_gate"], p["gate_bias"], p["alog"])
    xsT = matmul_conv(h, p["w_bf"], offs[1], d_ssd, p["conv_w"][:, :d_ssd], p["conv_b"][:, :d_ssd],
                      seq_len, True)
    bc = matmul_conv(h, p["w_bf"], offs[1] + d_ssd, 2 * SSD_GROUPS * SSD_STATE,
                     p["conv_w"][:, d_ssd:], p["conv_b"][:, d_ssd:], seq_len, False)
    y_f = ssd_scan(xsT, bc, ssd, ssdT, batch, seq_len, backward=False, dexp=p["dexp"])
    mix1 = ssd_scan(xsT, bc, ssd, ssdT, batch, seq_len, backward=True, y_fwd=y_f,
                    proj=proj_z, norm_w=p["ssd_norm_w"])
    h_f = mlstm_scan(proj_b, ml, mlT, batch, seq_len, cols, backward=False)
    mix2 = mlstm_scan(proj_b, ml, mlT, batch, seq_len, cols, backward=True, h_fwd=h_f,
                      norm_w=p["mlstm_norm_w"])
    x1, x1w, ssq = outproj_residual(mix1, mix2, p["w_out1"], p["w_out2"], x, p["norm2_w"])
    u = matmul(x1w, p["w_up"], BF16, relu2=True, row_ssq=ssq)
    return matmul_ksplit_residual(u, p["w_down"], x1)


def _trunk(x, layers, final_norm_w):
    batch, seq_len, d = x.shape
    xf = x.reshape(batch * seq_len, d)
    for p in layers:
        xf = _layer(xf, p, batch, seq_len)
    return rmsnorm_rows(xf, final_norm_w, F32).reshape(batch, seq_len, d)


def kernel(x_prompt, x_sample, norm1_w, w_in, conv_w, conv_b, dt_bias, a_log, d_skip, ssd_norm_w,
           b_i, b_f, mlstm_norm_w, w_out, norm2_w, w_up, w_down, final_norm_w):
    depth = w_in.shape[0]
    layers = [_prep_layer(norm1_w[l], w_in[l], conv_w[l], conv_b[l], dt_bias[l], a_log[l], d_skip[l],
                          ssd_norm_w[l], b_i[l], b_f[l], mlstm_norm_w[l], w_out[l], norm2_w[l],
                          w_up[l], w_down[l]) for l in range(depth)]
    y_prompt = _trunk(x_prompt, layers, final_norm_w)
    y_sample = _trunk(x_sample, layers, final_norm_w)
    return (y_prompt, y_sample)
```

```python
import functools

import jax
import jax.numpy as jnp
from jax import lax
from jax.experimental import pallas as pl
from jax.experimental.pallas import tpu as pltpu

F32 = jnp.float32
BF16 = jnp.bfloat16

CHUNK = 128
CHUNKS_PER_STEP = 2
EPS = 1e-5
D_CONV = 5
SSD_GROUPS = 8
SSD_HEAD_DIM = 64
SSD_STATE = 128
MLSTM_HEADS = 16
VMEM_LIMIT = 56 * 1024 * 1024
KSPLIT_VMEM_LIMIT = 60 * 1024 * 1024


def _cparams(sem, vmem_limit=VMEM_LIMIT):
    return pltpu.CompilerParams(dimension_semantics=sem, vmem_limit_bytes=vmem_limit)


def _sigmoid(x):
    return 1.0 / (1.0 + jnp.exp(-x))


def _softplus(x):
    return jnp.maximum(x, 0.0) + jnp.log1p(jnp.exp(-jnp.abs(x)))


def _dot(a, b):
    return jnp.dot(a, b, preferred_element_type=F32)


def _dot_nt(a, b):
    return lax.dot_general(a, b, (((1,), (1,)), ((), ())), preferred_element_type=F32)


def _rmsnorm_kernel(x_ref, w_ref, o_ref):
    x = x_ref[...].astype(F32)
    y = x * lax.rsqrt(jnp.mean(x * x, axis=-1, keepdims=True) + EPS)
    o_ref[...] = (y * w_ref[...]).astype(o_ref.dtype)


def rmsnorm_rows(x, w, out_dtype, bm=512):
    m, d = x.shape
    return pl.pallas_call(
        _rmsnorm_kernel,
        grid=(m // bm,),
        in_specs=[pl.BlockSpec((bm, d), lambda i: (i, 0)),
                  pl.BlockSpec((1, d), lambda i: (0, 0))],
        out_specs=pl.BlockSpec((bm, d), lambda i: (i, 0)),
        out_shape=jax.ShapeDtypeStruct((m, d), out_dtype),
        compiler_params=_cparams(("parallel",)),
        name="rmsnorm_rows",
    )(x, w.reshape(1, d).astype(F32))


def _matmul_kernel(*refs, relu2, row_ssq_dim):
    if row_ssq_dim:
        a_ref, b_ref, ssq_ref, o_ref = refs
    else:
        a_ref, b_ref, o_ref = refs
    acc = _dot(a_ref[...], b_ref[...])
    if relu2:
        acc = jnp.maximum(acc, 0.0)
        acc = acc * acc
    if row_ssq_dim:
        ssq = ssq_ref[...]
        tot = ssq[:, 0:128]
        for part in range(1, ssq.shape[1] // 128):
            tot = tot + ssq[:, part * 128:(part + 1) * 128]
        r2 = 1.0 / (tot * (1.0 / row_ssq_dim) + EPS)
        acc = acc * jnp.concatenate([r2] * (acc.shape[1] // 128), axis=1)
    o_ref[...] = acc.astype(o_ref.dtype)


def matmul(a, b, out_dtype, bm=1024, bn=1024, relu2=False, col0=0, n=None, row_ssq=None):
    m, k = a.shape
    n = b.shape[1] if n is None else n
    in_specs = [pl.BlockSpec((bm, k), lambda j, i: (i, 0)),
                pl.BlockSpec((pl.Element(k), pl.Element(bn)),
                             lambda j, i: (0, pl.multiple_of(col0 + j * bn, 128)))]
    args = [a, b]
    if row_ssq is not None:
        assert relu2
        in_specs.append(pl.BlockSpec((bm, row_ssq.shape[1]), lambda j, i: (i, 0)))
        args.append(row_ssq)
    return pl.pallas_call(
        functools.partial(_matmul_kernel, relu2=relu2, row_ssq_dim=k if row_ssq is not None else 0),
        grid=(n // bn, m // bm),
        in_specs=in_specs,
        out_specs=pl.BlockSpec((bm, bn), lambda j, i: (i, j)),
        out_shape=jax.ShapeDtypeStruct((m, n), out_dtype),
        compiler_params=_cparams(("parallel", "parallel")),
        name="matmul_relu2" if relu2 else "matmul",
    )(*args)


def _matmul_ksplit_res_kernel(a_ref, b_ref, x_ref, o_ref):
    @pl.when(pl.program_id(2) == 0)
    def _():
        o_ref[...] = x_ref[...] + _dot(a_ref[...], b_ref[...])

    @pl.when(pl.program_id(2) != 0)
    def _():
        o_ref[...] = o_ref[...] + _dot(a_ref[...], b_ref[...])


def matmul_ksplit_residual(a, b, x, bm=1024, bn=1024, bk=4096):
    m, k = a.shape
    n = b.shape[1]
    return pl.pallas_call(
        _matmul_ksplit_res_kernel,
        grid=(n // bn, m // bm, k // bk),
        in_specs=[pl.BlockSpec((bm, bk), lambda j, i, kk: (i, kk)),
                  pl.BlockSpec((bk, bn), lambda j, i, kk: (kk, j)),
                  pl.BlockSpec((bm, bn), lambda j, i, kk: (i, j))],
        out_specs=pl.BlockSpec((bm, bn), lambda j, i, kk: (i, j)),
        out_shape=jax.ShapeDtypeStruct((m, n), F32),
        compiler_params=_cparams(("parallel", "parallel", "arbitrary"), KSPLIT_VMEM_LIMIT),
        name="matmul_ksplit_residual",
    )(a, b, x)


def _split3(x):
    hi = x.astype(BF16)
    r1 = x - hi.astype(F32)
    mid = r1.astype(BF16)
    r2 = r1 - mid.astype(F32)
    return hi, mid, r2.astype(BF16)


def _gates_kernel(x_ref, nw_ref, w_ref, bias_ref, alog_ref,
                  h_ref, ssd_ref, ssdT_ref, ml_ref, mlT_ref, *, bm):
    x = x_ref[...]
    h = (x * lax.rsqrt(jnp.mean(x * x, axis=-1, keepdims=True) + EPS) * nw_ref[...]).astype(BF16)
    h_ref[...] = h
    raw = _dot(h, w_ref[...]) + bias_ref[...]
    dt = _softplus(raw[:, 0:128])
    a = dt * (-jnp.exp(alog_ref[...]))
    t2 = raw[:, 128:256]
    lane = lax.broadcasted_iota(jnp.int32, (CHUNK, 128), 1)
    row_t = lax.broadcasted_iota(jnp.int32, (CHUNK, 128), 0)
    lsig = -_softplus(-t2)
    row_i = lax.broadcasted_iota(jnp.int32, (CHUNK, CHUNK), 0)
    col_i = lax.broadcasted_iota(jnp.int32, (CHUNK, CHUNK), 1)
    lower = (col_i <= row_i).astype(BF16)
    upper = (col_i >= row_i).astype(BF16)
    ones = jnp.ones((CHUNK, CHUNK), BF16)
    ssd_fwd_lane = lane < 64
    ml_fwd_lane = (lane % 32) < 16

    def cums(x, fwd_lane):
        hi, mid, lo = _split3(x)
        cum_f = _dot(lower, hi) + _dot(lower, mid) + _dot(lower, lo)
        cum_b = _dot(upper, hi) + _dot(upper, mid) + _dot(upper, lo)
        tot = _dot(ones, hi) + _dot(ones, mid) + _dot(ones, lo)
        return jnp.where(fwd_lane, cum_f, cum_b), tot

    for c in range(bm // CHUNK):
        sl = slice(c * CHUNK, (c + 1) * CHUNK)
        dt_c = dt[sl]
        acum, tot = cums(a[sl], ssd_fwd_lane)
        ssd_ref[sl, 0:128] = dt_c
        ssd_ref[sl, 128:256] = acum
        ssd_ref[sl, 256:384] = tot
        ssdT_ref[0:128, sl] = dt_c.T
        ssdT_ref[128:256, sl] = acum.T
        ssdT_ref[256:384, sl] = tot.T
        t2_c = t2[sl]
        mcum, mtot = cums(lsig[sl], ml_fwd_lane)
        y = mcum - pltpu.roll(t2_c, 32, 1)
        y_f, y_b = y, y
        for d in (1, 2, 4, 8, 16, 32, 64):
            y_f = jnp.minimum(y_f, jnp.where(row_t >= d, pltpu.roll(y_f, d, 0), jnp.inf))
            y_b = jnp.minimum(y_b, jnp.where(row_t < CHUNK - d, pltpu.roll(y_b, CHUNK - d, 0), jnp.inf))
        rmax = mcum - jnp.where(ml_fwd_lane, y_f, y_b)
        ml_ref[sl, 0:128] = t2_c
        ml_ref[sl, 128:256] = mcum
        ml_ref[sl, 256:384] = mtot
        ml_ref[sl, 384:512] = rmax
        mlT_ref[0:128, sl] = t2_c.T
        mlT_ref[128:256, sl] = mcum.T
        mlT_ref[256:384, sl] = mtot.T


def gates(x, norm_w, w_gate, bias, alog, bm=512):
    m, d = x.shape
    nat = pl.BlockSpec((bm, 384), lambda i: (i, 0))
    nat4 = pl.BlockSpec((bm, 512), lambda i: (i, 0))
    tr = pl.BlockSpec((384, bm), lambda i: (0, i))
    return pl.pallas_call(
        functools.partial(_gates_kernel, bm=bm),
        grid=(m // bm,),
        in_specs=[pl.BlockSpec((bm, d), lambda i: (i, 0)),
                  pl.BlockSpec((1, d), lambda i: (0, 0)),
                  pl.BlockSpec((d, 256), lambda i: (0, 0)),
                  pl.BlockSpec((1, 256), lambda i: (0, 0)),
                  pl.BlockSpec((1, 128), lambda i: (0, 0))],
        out_specs=[pl.BlockSpec((bm, d), lambda i: (i, 0)), nat, tr, nat4, tr],
        out_shape=[jax.ShapeDtypeStruct((m, d), BF16),
                   jax.ShapeDtypeStruct((m, 384), F32), jax.ShapeDtypeStruct((384, m), F32),
                   jax.ShapeDtypeStruct((m, 512), F32), jax.ShapeDtypeStruct((384, m), F32)],
        compiler_params=_cparams(("parallel",)),
        name="gates",
    )(x, norm_w.reshape(1, d).astype(F32), w_gate, bias, alog)


def _matmul_conv_kernel(h_ref, hp_ref, hn_ref, w_ref, cw_ref, cb_ref, o_ref, *, bm, seq_len,
                        transpose_out, sub):
    i = pl.program_id(1)
    w = w_ref[...]
    at_start = (i * bm) % seq_len == 0
    at_end = ((i + 1) * bm) % seq_len == 0
    cw = cw_ref[...]
    cb = cb_ref[...]
    nsub = bm // sub
    row = lax.broadcasted_iota(jnp.int32, (sub, w.shape[1]), 0)

    def conv(cur, prev8, next8):
        m2 = pltpu.roll(cur, 2, 0)
        m2 = jnp.where(row == 0, prev8[6:7], jnp.where(row == 1, prev8[7:8], m2))
        m1 = pltpu.roll(cur, 1, 0)
        m1 = jnp.where(row == 0, prev8[7:8], m1)
        p1 = pltpu.roll(cur, sub - 1, 0)
        p1 = jnp.where(row == sub - 1, next8[0:1], p1)
        p2 = pltpu.roll(cur, sub - 2, 0)
        p2 = jnp.where(row == sub - 2, next8[0:1], jnp.where(row == sub - 1, next8[1:2], p2))
        out = cw[0:1] * m2
        out = out + cw[1:2] * m1
        out = out + cw[2:3] * cur
        out = out + cw[3:4] * p1
        out = out + cw[4:5] * p2
        out = out + cb
        return out * _sigmoid(out)

    def emit(r, res):
        if transpose_out:
            o_ref[:, r * sub:(r + 1) * sub] = res.T.astype(o_ref.dtype)
        else:
            o_ref[r * sub:(r + 1) * sub, :] = res.astype(o_ref.dtype)

    prev8 = jnp.where(at_start, 0.0, _dot(hp_ref[...], w))[8:16]
    blocks = [_dot(h_ref[0:sub, :], w)]
    for r in range(nsub):
        if r + 1 < nsub:
            blocks.append(_dot(h_ref[(r + 1) * sub:(r + 2) * sub, :], w))
            next8 = blocks[r + 1][0:8]
        else:
            next8 = jnp.where(at_end, 0.0, _dot(hn_ref[...], w))[0:8]
        emit(r, conv(blocks[r], prev8, next8))
        prev8 = blocks[r][sub - 8:sub]


def matmul_conv(h, w, col0, n, conv_w, conv_b, seq_len, transpose_out, bm=1024, bn=1024, sub=256):
    m, k = h.shape
    assert seq_len % bm == 0 and bm % sub == 0
    nhb = m // 16
    if transpose_out:
        out_spec = pl.BlockSpec((bn, bm), lambda j, i: (j, i))
        out_shape = jax.ShapeDtypeStruct((n, m), BF16)
    else:
        out_spec = pl.BlockSpec((bm, bn), lambda j, i: (i, j))
        out_shape = jax.ShapeDtypeStruct((m, n), BF16)
    return pl.pallas_call(
        functools.partial(_matmul_conv_kernel, bm=bm, seq_len=seq_len, transpose_out=transpose_out,
                          sub=sub),
        grid=(n // bn, m // bm),
        in_specs=[pl.BlockSpec((bm, k), lambda j, i: (i, 0)),
                  pl.BlockSpec((16, k), lambda j, i: (jnp.maximum(i * (bm // 16) - 1, 0), 0)),
                  pl.BlockSpec((16, k), lambda j, i: (jnp.minimum((i + 1) * (bm // 16), nhb - 1), 0)),
                  pl.BlockSpec((pl.Element(k), pl.Element(bn)),
                               lambda j, i: (0, pl.multiple_of(col0 + j * bn, 128))),
                  pl.BlockSpec((D_CONV, bn), lambda j, i: (0, j)),
                  pl.BlockSpec((1, bn), lambda j, i: (0, j))],
        out_specs=out_spec,
        out_shape=out_shape,
        compiler_params=_cparams(("parallel", "parallel")),
        name="matmul_conv_t" if transpose_out else "matmul_conv",
    )(h, h, h, w, conv_w, conv_b)


def _ssd_kernel(*refs, backward, cb):
    if backward:
        (xsT_ref, b_ref, c_ref, nat_ref, tr_ref, yf_ref, z_ref, nw_ref, o_ref, s_ref) = refs
    else:
        (xsT_ref, b_ref, c_ref, nat_ref, tr_ref, dexp_ref, o_ref, s_ref) = refs
    G, R, P, N = SSD_GROUPS, 8, SSD_HEAD_DIM, SSD_STATE
    gw = R * P
    h0 = G * R if backward else 0

    @pl.when(pl.program_id(1) == 0)
    def _():
        s_ref[...] = jnp.zeros_like(s_ref)

    s_i = lax.broadcasted_iota(jnp.int32, (CHUNK, CHUNK), 0)
    l_i = lax.broadcasted_iota(jnp.int32, (CHUNK, CHUNK), 1)
    mask = (l_i <= s_i) if backward else (l_i >= s_i)
    chunk_order = range(cb - 1, -1, -1) if backward else range(cb)
    for j, g in [(j, g) for j in chunk_order for g in range(G)]:
        ts = slice(j * CHUNK, (j + 1) * CHUNK)
        gs = slice(g * gw, (g + 1) * gw)
        hg = h0 + g * R
        bm = b_ref[ts, g * N:(g + 1) * N]
        cm = c_ref[ts, g * N:(g + 1) * N]
        dt = tr_ref[hg:hg + R, ts]
        acum = tr_ref[128 + hg:128 + hg + R, ts]
        tot = tr_ref[256 + hg:256 + hg + R, ts]
        cbT = _dot_nt(bm, cm)
        s_old = s_ref[gs, :]
        yoffT = _dot_nt(s_old.astype(BF16), cm)
        e_acum = jnp.exp(acum)
        dte = jnp.exp(tot - acum)
        e_tot = jnp.exp(tot)
        y_pieces = []
        xd_pieces = []
        for r in range(R):
            hs = slice(g * gw + r * P, g * gw + (r + 1) * P)
            xr = xsT_ref[hs, ts].astype(F32)
            xdt = xr * dt[r:r + 1, :]
            col = nat_ref[ts, 128 + hg + r:128 + hg + r + 1]
            seg = acum[r:r + 1, :] - col
            dec = jnp.exp(jnp.where(mask, seg, -jnp.inf))
            mt = (cbT * dec).astype(BF16)
            y_r = _dot(xdt.astype(BF16), mt) + yoffT[r * P:(r + 1) * P, :] * e_acum[r:r + 1, :]
            if not backward:
                y_r = y_r + dexp_ref[hs, :] * xr
            y_pieces.append(y_r)
            xd_pieces.append((xdt * dte[r:r + 1, :]).astype(BF16))
        yT = jnp.concatenate(y_pieces, axis=0)
        upd = _dot(jnp.concatenate(xd_pieces, axis=0), bm)
        for r in range(R):
            hs = slice(r * P, (r + 1) * P)
            s_ref[g * gw + r * P:g * gw + (r + 1) * P, :] = s_old[hs, :] * e_tot[r:r + 1, :] + upd[hs, :]
        y = yT.T
        if backward:
            y = y + yf_ref[ts, gs]
            z = z_ref[ts, gs].astype(F32)
            y = y * (z * _sigmoid(z))
            y = y * lax.rsqrt(jnp.mean(y * y, axis=-1, keepdims=True) + EPS)
            o_ref[ts, gs] = (y * nw_ref[:, gs]).astype(o_ref.dtype)
        else:
            o_ref[ts, gs] = y


def ssd_scan(xsT, bc, ssd, ssdT, batch, seq_len, *, backward, dexp=None,
             y_fwd=None, proj=None, norm_w=None):
    m = xsT.shape[1]
    cb = CHUNKS_PER_STEP
    tb = CHUNK * cb
    nc = seq_len // tb
    G = SSD_GROUPS
    d_ssd = xsT.shape[0]

    def cg(b, c):
        return b * nc + ((nc - 1 - c) if backward else c)

    in_specs = [
        pl.BlockSpec((d_ssd, tb), lambda b, c: (0, cg(b, c))),
        pl.BlockSpec((tb, G * SSD_STATE), lambda b, c: (cg(b, c), 0)),
        pl.BlockSpec((tb, G * SSD_STATE), lambda b, c: (cg(b, c), 1)),
        pl.BlockSpec((tb, 384), lambda b, c: (cg(b, c), 0)),
        pl.BlockSpec((384, tb), lambda b, c: (0, cg(b, c))),
    ]
    args = [xsT, bc, bc, ssd, ssdT]
    if backward:
        in_specs += [
            pl.BlockSpec((tb, d_ssd), lambda b, c: (cg(b, c), 0)),
            pl.BlockSpec((tb, d_ssd), lambda b, c: (cg(b, c), 0)),
            pl.BlockSpec((1, d_ssd), lambda b, c: (0, 0)),
        ]
        args += [y_fwd, proj, norm_w]
        out_dtype = BF16
    else:
        in_specs += [pl.BlockSpec((d_ssd, 128), lambda b, c: (0, 0))]
        args += [dexp]
        out_dtype = F32
    return pl.pallas_call(
        functools.partial(_ssd_kernel, backward=backward, cb=cb),
        grid=(batch, nc),
        in_specs=in_specs,
        out_specs=pl.BlockSpec((tb, d_ssd), lambda b, c: (cg(b, c), 0)),
        out_shape=jax.ShapeDtypeStruct((m, d_ssd), out_dtype),
        scratch_shapes=[pltpu.VMEM((d_ssd, SSD_STATE), F32)],
        compiler_params=_cparams(("parallel", "arbitrary")),
        name="ssd_bwd" if backward else "ssd_fwd",
    )(*args)


def _mlstm_kernel(*refs, backward, dk, dv, heads_per_group, cb):
    if backward:
        (q_ref, k_ref, v_ref, nat_ref, tr_ref, hf_ref, og_ref, nw_ref, o_ref, c_ref, m_ref) = refs
    else:
        (q_ref, k_ref, v_ref, nat_ref, tr_ref, o_ref, c_ref, m_ref) = refs
    H = MLSTM_HEADS

    @pl.when(pl.program_id(1) == 0)
    def _():
        c_ref[...] = jnp.zeros_like(c_ref)
        m_ref[...] = jnp.zeros_like(m_ref)

    t_i = lax.broadcasted_iota(jnp.int32, (CHUNK, CHUNK), 0)
    s_i = lax.broadcasted_iota(jnp.int32, (CHUNK, CHUNK), 1)
    mask = (s_i >= t_i) if backward else (s_i <= t_i)
    ones_blk = jnp.ones((CHUNK, 128), BF16)
    scale = dk ** -0.5
    dense = (CHUNK, CHUNK)

    chunk_order = range(cb - 1, -1, -1) if backward else range(cb)
    for j, g0 in [(j, g0) for j in chunk_order for g0 in range(0, H, heads_per_group)]:
        ts = slice(j * CHUNK, (j + 1) * CHUNK)
        hs = list(range(g0, g0 + heads_per_group))
        lane_of = {h: h + (H if backward else 0) for h in hs}
        st = {h: {} for h in hs}
        for h in hs:
            d, hh = st[h], lane_of[h]
            li_row = tr_ref[hh:hh + 1, ts]
            cum_row = tr_ref[160 + hh:161 + hh, ts]
            d["tot"] = tr_ref[288 + hh:289 + hh, ts]
            d["base_row"] = cum_row - li_row
            grow = d["tot"] - d["base_row"]
            d["m_in"] = m_ref[h, 0:1, :]
            m_loc = jnp.broadcast_to(jnp.max(grow, axis=1, keepdims=True), (1, CHUNK))
            d["m_new"] = jnp.maximum(d["tot"] + d["m_in"], m_loc)
            d["w_row"] = jnp.exp(grow - d["m_new"])
            d["cum_d"] = jnp.broadcast_to(nat_ref[ts, 160 + hh:161 + hh], dense)
            d["rmax_d"] = jnp.broadcast_to(nat_ref[ts, 416 + hh:417 + hh], dense)
            d["v_aug"] = jnp.concatenate([v_ref[ts, h * dv:(h + 1) * dv].astype(BF16), ones_blk], axis=1)
        for h in hs:
            d = st[h]
            k = k_ref[ts, h * dk:(h + 1) * dk].astype(F32)
            d["kb"] = k.astype(BF16)
            d["kwT"] = (k.T * d["w_row"]).astype(BF16)
            d["qs"] = (q_ref[ts, h * dk:(h + 1) * dk].astype(F32) * scale).astype(BF16)
        for h in hs:
            d = st[h]
            d["c_loc"] = _dot(d["kwT"], d["v_aug"])
            d["sqk"] = _dot_nt(d["qs"], d["kb"])
            d["c_in"] = c_ref[h]
            d["qc"] = _dot(d["qs"], d["c_in"].astype(BF16))
        for h in hs:
            d = st[h]
            dlog = jnp.where(mask, d["cum_d"] - d["base_row"], -jnp.inf)
            inter = d["cum_d"] + d["m_in"]
            d["m_t"] = jnp.maximum(d["rmax_d"], inter)
            d["pm"] = (jnp.exp(dlog - d["m_t"]) * d["sqk"]).astype(BF16)
            d["a_inter"] = jnp.exp(inter - d["m_t"])
        for h in hs:
            d = st[h]
            a3 = jnp.concatenate([d["a_inter"]] * (dv // 128 + 1), axis=1)
            num = _dot(d["pm"], d["v_aug"]) + d["qc"] * a3
            den = num[:, dv:dv + 128]
            inv = 1.0 / jnp.maximum(jnp.abs(den), jnp.exp(-d["m_t"]))
            d["hout"] = num[:, 0:dv] * jnp.concatenate([inv] * (dv // 128), axis=1)
        for h in hs:
            d = st[h]
            a_prev = jnp.exp(d["tot"] + d["m_in"] - d["m_new"])
            a_prev3 = jnp.concatenate([a_prev] * (dv // 128 + 1), axis=1)
            c_ref[h] = a_prev3 * d["c_in"] + d["c_loc"]
            m_ref[h] = jnp.broadcast_to(d["m_new"], (8, 128))
        for h in hs:
            hout = st[h]["hout"]
            vs = slice(h * dv, (h + 1) * dv)
            if backward:
                hout = hout + hf_ref[ts, vs]
                hout = hout * lax.rsqrt(jnp.mean(hout * hout, axis=-1, keepdims=True) + EPS)
                hout = hout * nw_ref[:, vs]
                o_ref[ts, vs] = (_sigmoid(og_ref[ts, vs].astype(F32)) * hout).astype(o_ref.dtype)
            else:
                o_ref[ts, vs] = hout


def mlstm_scan(proj, ml, mlT, batch, seq_len, cols, *, backward, h_fwd=None, norm_w=None):
    m = proj.shape[0]
    cb = CHUNKS_PER_STEP
    tb = CHUNK * cb
    nc = seq_len // tb
    H = MLSTM_HEADS
    dk, dv = cols["dk"], cols["dv"]
    qw, vw = H * dk, H * dv
    qb, kb, vb, ob = cols["q"] // qw, cols["k"] // qw, cols["v"] // vw, cols["o"] // vw
    assert qb * qw == cols["q"] and kb * qw == cols["k"] and vb * vw == cols["v"] and ob * vw == cols["o"]

    def cg(b, c):
        return b * nc + ((nc - 1 - c) if backward else c)

    in_specs = [
        pl.BlockSpec((tb, qw), lambda b, c: (cg(b, c), qb)),
        pl.BlockSpec((tb, qw), lambda b, c: (cg(b, c), kb)),
        pl.BlockSpec((tb, vw), lambda b, c: (cg(b, c), vb)),
        pl.BlockSpec((tb, 512), lambda b, c: (cg(b, c), 0)),
        pl.BlockSpec((384, tb), lambda b, c: (0, cg(b, c))),
    ]
    args = [proj, proj, proj, ml, mlT]
    if backward:
        in_specs += [
            pl.BlockSpec((tb, vw), lambda b, c: (cg(b, c), 0)),
            pl.BlockSpec((tb, vw), lambda b, c: (cg(b, c), ob)),
            pl.BlockSpec((1, vw), lambda b, c: (0, 0)),
        ]
        args += [h_fwd, proj, norm_w]
        out_dtype = BF16
    else:
        out_dtype = F32
    return pl.pallas_call(
        functools.partial(_mlstm_kernel, backward=backward, dk=dk, dv=dv,
                          heads_per_group=8 if backward else 4, cb=cb),
        grid=(batch, nc),
        in_specs=in_specs,
        out_specs=pl.BlockSpec((tb, vw), lambda b, c: (cg(b, c), 0)),
        out_shape=jax.ShapeDtypeStruct((m, vw), out_dtype),
        scratch_shapes=[pltpu.VMEM((H, dk, dv + 128), F32), pltpu.VMEM((H, 8, 128), F32)],
        compiler_params=_cparams(("parallel", "arbitrary")),
        name="mlstm_bwd" if backward else "mlstm_fwd",
    )(*args)


def _outproj_kernel(a1_ref, a2_ref, w1_ref, w2_ref, x_ref, nw_ref, o_ref, xw_ref, ssq_ref):
    acc = _dot(a1_ref[...], w1_ref[...]) + _dot(a2_ref[...], w2_ref[...])
    x1 = x_ref[...] + acc
    o_ref[...] = x1
    xw_ref[...] = (x1 * nw_ref[...]).astype(xw_ref.dtype)
    ssq_ref[...] = jnp.broadcast_to(jnp.sum(x1 * x1, axis=-1, keepdims=True), ssq_ref.shape)


def outproj_residual(a1, a2, w1, w2, x, norm_w, bm=512, bn=512):
    m, k = a1.shape
    n = w1.shape[1]
    return pl.pallas_call(
        _outproj_kernel,
        grid=(n // bn, m // bm),
        in_specs=[pl.BlockSpec((bm, k), lambda j, i: (i, 0)),
                  pl.BlockSpec((bm, k), lambda j, i: (i, 0)),
                  pl.BlockSpec((k, bn), lambda j, i: (0, j)),
                  pl.BlockSpec((k, bn), lambda j, i: (0, j)),
                  pl.BlockSpec((bm, bn), lambda j, i: (i, j)),
                  pl.BlockSpec((1, bn), lambda j, i: (0, j))],
        out_specs=[pl.BlockSpec((bm, bn), lambda j, i: (i, j)),
                   pl.BlockSpec((bm, bn), lambda j, i: (i, j)),
                   pl.BlockSpec((bm, 128), lambda j, i: (i, j))],
        out_shape=[jax.ShapeDtypeStruct((m, n), F32), jax.ShapeDtypeStruct((m, n), BF16),
                   jax.ShapeDtypeStruct((m, (n // bn) * 128), F32)],
        compiler_params=_cparams(("parallel", "parallel")),
        name="outproj_residual",
    )(a1, a2, w1, w2, x, norm_w.reshape(1, n).astype(F32))


def _prep_layer(norm1_w, w_in, conv_w, conv_b, dt_bias, a_log, d_skip, ssd_norm_w, b_i, b_f,
                mlstm_norm_w, w_out, norm2_w, w_up, w_down):
    d_model = w_in.shape[0]
    d_mix = w_out.shape[0]
    d_ssd = d_mix // 2
    d_ml = d_mix - d_ssd
    n_ssd_heads = d_ssd // SSD_HEAD_DIM
    xbc_w = d_ssd + 2 * SSD_GROUPS * SSD_STATE
    dv = d_ml // MLSTM_HEADS
    dk = dv // 2
    widths = (d_ssd, xbc_w, 2 * n_ssd_heads, MLSTM_HEADS * dk, MLSTM_HEADS * dk, d_ml, d_ml,
              2 * MLSTM_HEADS, 2 * MLSTM_HEADS)
    offs = [0]
    for wd in widths:
        offs.append(offs[-1] + wd)
    assert offs[-1] == w_in.shape[1]
    assert 2 * n_ssd_heads == 128 and 2 * MLSTM_HEADS == 32
    seg = lambda i: w_in[:, offs[i]:offs[i + 1]]
    w_bf = w_in.astype(BF16)
    w_gate = jnp.concatenate([seg(2), seg(7), seg(8), jnp.zeros((d_model, 64), w_in.dtype)],
                             axis=1).astype(BF16)
    gate_bias = jnp.concatenate([dt_bias.reshape(-1), b_i.reshape(-1), b_f.reshape(-1),
                                 jnp.zeros((64,), F32)]).astype(F32).reshape(1, 256)
    cols = {"q": 0, "k": MLSTM_HEADS * dk, "v": 2 * MLSTM_HEADS * dk}
    cols["o"] = cols["v"] + d_ml
    cols["dk"], cols["dv"], cols["d_ssd"] = dk, dv, d_ssd
    return dict(
        norm1_w=norm1_w, w_bf=w_bf, offs=offs, w_gate=w_gate, gate_bias=gate_bias,
        alog=a_log.reshape(1, 128).astype(F32),
        conv_w=conv_w.astype(F32), conv_b=conv_b.reshape(1, -1).astype(F32),
        dexp=jnp.broadcast_to(jnp.repeat(d_skip.astype(F32), SSD_HEAD_DIM)[:, None], (d_ssd, 128)),
        ssd_norm_w=ssd_norm_w.reshape(1, -1).astype(F32),
        mlstm_norm_w=mlstm_norm_w.reshape(1, -1).astype(F32),
        w_out1=w_out[:d_ssd].astype(BF16), w_out2=w_out[d_ssd:].astype(BF16),
        norm2_w=norm2_w, w_up=w_up.astype(BF16), w_down=w_down.astype(BF16), cols=cols)


def _layer(x, p, batch, seq_len):
    cols = p["cols"]
    d_ssd = cols["d_ssd"]
    h, ssd, ssdT, ml, mlT = gates(x, p["norm1_w"], p["w_gate"], p["gate_bias"], p["alog"])
    offs = p["offs"]
    proj_z = matmul(h, p["w_bf"], F32, col0=offs[0], n=d_ssd)
    proj_b = matmul(h, p["w_bf"], BF16, col0=offs[3], n=offs[7] - offs[3])
    xsT = matmul_conv(h, p["w_bf"], offs[1], d_ssd, p["conv_w"][:, :d_ssd], p["conv_b"][:, :d_ssd],
                      seq_len, True)
    bc = matmul_conv(h, p["w_bf"], offs[1] + d_ssd, 2 * SSD_GROUPS * SSD_STATE,
                     p["conv_w"][:, d_ssd:], p["conv_b"][:, d_ssd:], seq_len, False)
    y_f = ssd_scan(xsT, bc, ssd, ssdT, batch, seq_len, backward=False, dexp=p["dexp"])
    mix1 = ssd_scan(xsT, bc, ssd, ssdT, batch, seq_len, backward=True, y_fwd=y_f,
                    proj=proj_z, norm_w=p["ssd_norm_w"])
    h_f = mlstm_scan(proj_b, ml, mlT, batch, seq_len, cols, backward=False)
    mix2 = mlstm_scan(proj_b, ml, mlT, batch, seq_len, cols, backward=True, h_fwd=h_f,
                      norm_w=p["mlstm_norm_w"])
    x1, x1w, ssq = outproj_residual(mix1, mix2, p["w_out1"], p["w_out2"], x, p["norm2_w"])
    u = matmul(x1w, p["w_up"], BF16, relu2=True, row_ssq=ssq)
    return matmul_ksplit_residual(u, p["w_down"], x1)


def _trunk(x, layers, final_norm_w):
    batch, seq_len, d = x.shape
    xf = x.reshape(batch * seq_len, d)
    for p in layers:
        xf = _layer(xf, p, batch, seq_len)
    return rmsnorm_rows(xf, final_norm_w, F32).reshape(batch, seq_len, d)


def kernel(x_prompt, x_sample, norm1_w, w_in, conv_w, conv_b, dt_bias, a_log, d_skip, ssd_norm_w,
           b_i, b_f, mlstm_norm_w, w_out, norm2_w, w_up, w_down, final_norm_w):
    depth = w_in.shape[0]
    layers = [_prep_layer(norm1_w[l], w_in[l], conv_w[l], conv_b[l], dt_bias[l], a_log[l], d_skip[l],
                          ssd_norm_w[l], b_i[l], b_f[l], mlstm_norm_w[l], w_out[l], norm2_w[l],
                          w_up[l], w_down[l]) for l in range(depth)]
    y_prompt = _trunk(x_prompt, layers, final_norm_w)
    y_sample = _trunk(x_sample, layers, final_norm_w)
    return (y_prompt, y_sample)
```

```python
import functools

import jax
import jax.numpy as jnp
from jax import lax
from jax.experimental import pallas as pl
from jax.experimental.pallas import tpu as pltpu

F32 = jnp.float32
BF16 = jnp.bfloat16

CHUNK = 128
CHUNKS_PER_STEP = 2
EPS = 1e-5
D_CONV = 5
SSD_GROUPS = 8
SSD_HEAD_DIM = 64
SSD_STATE = 128
MLSTM_HEADS = 16
VMEM_LIMIT = 56 * 1024 * 1024
KSPLIT_VMEM_LIMIT = 60 * 1024 * 1024


def _cparams(sem, vmem_limit=VMEM_LIMIT):
    return pltpu.CompilerParams(dimension_semantics=sem, vmem_limit_bytes=vmem_limit)


def _sigmoid(x):
    return 1.0 / (1.0 + jnp.exp(-x))


def _softplus(x):
    return jnp.maximum(x, 0.0) + jnp.log1p(jnp.exp(-jnp.abs(x)))


def _dot(a, b):
    return jnp.dot(a, b, preferred_element_type=F32)


def _dot_nt(a, b):
    return lax.dot_general(a, b, (((1,), (1,)), ((), ())), preferred_element_type=F32)


def _rmsnorm_kernel(x_ref, w_ref, o_ref):
    x = x_ref[...].astype(F32)
    y = x * lax.rsqrt(jnp.mean(x * x, axis=-1, keepdims=True) + EPS)
    o_ref[...] = (y * w_ref[...]).astype(o_ref.dtype)


def rmsnorm_rows(x, w, out_dtype, bm=512):
    m, d = x.shape
    return pl.pallas_call(
        _rmsnorm_kernel,
        grid=(m // bm,),
        in_specs=[pl.BlockSpec((bm, d), lambda i: (i, 0)),
                  pl.BlockSpec((1, d), lambda i: (0, 0))],
        out_specs=pl.BlockSpec((bm, d), lambda i: (i, 0)),
        out_shape=jax.ShapeDtypeStruct((m, d), out_dtype),
        compiler_params=_cparams(("parallel",)),
        name="rmsnorm_rows",
    )(x, w.reshape(1, d).astype(F32))


def _matmul_kernel(*refs, relu2, row_ssq_dim):
    if row_ssq_dim:
        a_ref, b_ref, ssq_ref, o_ref = refs
    else:
        a_ref, b_ref, o_ref = refs
    acc = _dot(a_ref[...], b_ref[...])
    if relu2:
        acc = jnp.maximum(acc, 0.0)
        acc = acc * acc
    if row_ssq_dim:
        ssq = ssq_ref[...]
        tot = ssq[:, 0:128]
        for part in range(1, ssq.shape[1] // 128):
            tot = tot + ssq[:, part * 128:(part + 1) * 128]
        r2 = 1.0 / (tot * (1.0 / row_ssq_dim) + EPS)
        acc = acc * jnp.concatenate([r2] * (acc.shape[1] // 128), axis=1)
    o_ref[...] = acc.astype(o_ref.dtype)


def matmul(a, b, out_dtype, bm=1024, bn=1024, relu2=False, col0=0, n=None, row_ssq=None):
    m, k = a.shape
    n = b.shape[1] if n is None else n
    in_specs = [pl.BlockSpec((bm, k), lambda j, i: (i, 0)),
                pl.BlockSpec((pl.Element(k), pl.Element(bn)),
                             lambda j, i: (0, pl.multiple_of(col0 + j * bn, 128)))]
    args = [a, b]
    if row_ssq is not None:
        assert relu2
        in_specs.append(pl.BlockSpec((bm, row_ssq.shape[1]), lambda j, i: (i, 0)))
        args.append(row_ssq)
    return pl.pallas_call(
        functools.partial(_matmul_kernel, relu2=relu2, row_ssq_dim=k if row_ssq is not None else 0),
        grid=(n // bn, m // bm),
        in_specs=in_specs,
        out_specs=pl.BlockSpec((bm, bn), lambda j, i: (i, j)),
        out_shape=jax.ShapeDtypeStruct((m, n), out_dtype),
        compiler_params=_cparams(("parallel", "parallel")),
        name="matmul_relu2" if relu2 else "matmul",
    )(*args)


def _matmul_ksplit_res_kernel(a_ref, b_ref, x_ref, o_ref):
    @pl.when(pl.program_id(2) == 0)
    def _():
        o_ref[...] = x_ref[...] + _dot(a_ref[...], b_ref[...])

    @pl.when(pl.program_id(2) != 0)
    def _():
        o_ref[...] = o_ref[...] + _dot(a_ref[...], b_ref[...])


def matmul_ksplit_residual(a, b, x, bm=1024, bn=1024, bk=4096):
    m, k = a.shape
    n = b.shape[1]
    return pl.pallas_call(
        _matmul_ksplit_res_kernel,
        grid=(n // bn, m // bm, k // bk),
        in_specs=[pl.BlockSpec((bm, bk), lambda j, i, kk: (i, kk)),
                  pl.BlockSpec((bk, bn), lambda j, i, kk: (kk, j)),
                  pl.BlockSpec((bm, bn), lambda j, i, kk: (i, j))],
        out_specs=pl.BlockSpec((bm, bn), lambda j, i, kk: (i, j)),
        out_shape=jax.ShapeDtypeStruct((m, n), F32),
        compiler_params=_cparams(("parallel", "parallel", "arbitrary"), KSPLIT_VMEM_LIMIT),
        name="matmul_ksplit_residual",
    )(a, b, x)


def _split3(x):
    hi = x.astype(BF16)
    r1 = x - hi.astype(F32)
    mid = r1.astype(BF16)
    r2 = r1 - mid.astype(F32)
    return hi, mid, r2.astype(BF16)


def _gates_kernel(x_ref, nw_ref, w_ref, bias_ref, alog_ref,
                  h_ref, ssd_ref, ssdT_ref, ml_ref, mlT_ref, *, bm):
    x = x_ref[...]
    h = (x * lax.rsqrt(jnp.mean(x * x, axis=-1, keepdims=True) + EPS) * nw_ref[...]).astype(BF16)
    h_ref[...] = h
    raw = _dot(h, w_ref[...]) + bias_ref[...]
    dt = _softplus(raw[:, 0:128])
    a = dt * (-jnp.exp(alog_ref[...]))
    t2 = raw[:, 128:256]
    lane = lax.broadcasted_iota(jnp.int32, (CHUNK, 128), 1)
    row_t = lax.broadcasted_iota(jnp.int32, (CHUNK, 128), 0)
    lsig = -_softplus(-t2)
    row_i = lax.broadcasted_iota(jnp.int32, (CHUNK, CHUNK), 0)
    col_i = lax.broadcasted_iota(jnp.int32, (CHUNK, CHUNK), 1)
    lower = (col_i <= row_i).astype(BF16)
    upper = (col_i >= row_i).astype(BF16)
    ones = jnp.ones((CHUNK, CHUNK), BF16)
    ssd_fwd_lane = lane < 64
    ml_fwd_lane = (lane % 32) < 16

    def cums(x, fwd_lane):
        hi, mid, lo = _split3(x)
        cum_f = _dot(lower, hi) + _dot(lower, mid) + _dot(lower, lo)
        cum_b = _dot(upper, hi) + _dot(upper, mid) + _dot(upper, lo)
        tot = _dot(ones, hi) + _dot(ones, mid) + _dot(ones, lo)
        return jnp.where(fwd_lane, cum_f, cum_b), tot

    for c in range(bm // CHUNK):
        sl = slice(c * CHUNK, (c + 1) * CHUNK)
        dt_c = dt[sl]
        acum, tot = cums(a[sl], ssd_fwd_lane)
        ssd_ref[sl, 0:128] = dt_c
        ssd_ref[sl, 128:256] = acum
        ssd_ref[sl, 256:384] = tot
        ssdT_ref[0:128, sl] = dt_c.T
        ssdT_ref[128:256, sl] = acum.T
        ssdT_ref[256:384, sl] = tot.T
        t2_c = t2[sl]
        mcum, mtot = cums(lsig[sl], ml_fwd_lane)
        y = mcum - pltpu.roll(t2_c, 32, 1)
        y_f, y_b = y, y
        for d in (1, 2, 4, 8, 16, 32, 64):
            y_f = jnp.minimum(y_f, jnp.where(row_t >= d, pltpu.roll(y_f, d, 0), jnp.inf))
            y_b = jnp.minimum(y_b, jnp.where(row_t < CHUNK - d, pltpu.roll(y_b, CHUNK - d, 0), jnp.inf))
        rmax = mcum - jnp.where(ml_fwd_lane, y_f, y_b)
        ml_ref[sl, 0:128] = t2_c
        ml_ref[sl, 128:256] = mcum
        ml_ref[sl, 256:384] = mtot
        ml_ref[sl, 384:512] = rmax
        mlT_ref[0:128, sl] = t2_c.T
        mlT_ref[128:256, sl] = mcum.T
        mlT_ref[256:384, sl] = mtot.T


def gates(x, norm_w, w_gate, bias, alog, bm=512):
    m, d = x.shape
    nat = pl.BlockSpec((bm, 384), lambda i: (i, 0))
    nat4 = pl.BlockSpec((bm, 512), lambda i: (i, 0))
    tr = pl.BlockSpec((384, bm), lambda i: (0, i))
    return pl.pallas_call(
        functools.partial(_gates_kernel, bm=bm),
        grid=(m // bm,),
        in_specs=[pl.BlockSpec((bm, d), lambda i: (i, 0)),
                  pl.BlockSpec((1, d), lambda i: (0, 0)),
                  pl.BlockSpec((d, 256), lambda i: (0, 0)),
                  pl.BlockSpec((1, 256), lambda i: (0, 0)),
                  pl.BlockSpec((1, 128), lambda i: (0, 0))],
        out_specs=[pl.BlockSpec((bm, d), lambda i: (i, 0)), nat, tr, nat4, tr],
        out_shape=[jax.ShapeDtypeStruct((m, d), BF16),
                   jax.ShapeDtypeStruct((m, 384), F32), jax.ShapeDtypeStruct((384, m), F32),
                   jax.ShapeDtypeStruct((m, 512), F32), jax.ShapeDtypeStruct((384, m), F32)],
        compiler_params=_cparams(("parallel",)),
        name="gates",
    )(x, norm_w.reshape(1, d).astype(F32), w_gate, bias, alog)


def _matmul_conv_kernel(h_ref, hp_ref, hn_ref, w_ref, cw_ref, cb_ref, o_ref, *, bm, seq_len,
                        transpose_out, sub):
    i = pl.program_id(1)
    w = w_ref[...]
    at_start = (i * bm) % seq_len == 0
    at_end = ((i + 1) * bm) % seq_len == 0
    cw = cw_ref[...]
    cb = cb_ref[...]
    nsub = bm // sub
    row = lax.broadcasted_iota(jnp.int32, (sub, w.shape[1]), 0)

    def conv(cur, prev8, next8):
        m2 = pltpu.roll(cur, 2, 0)
        m2 = jnp.where(row == 0, prev8[6:7], jnp.where(row == 1, prev8[7:8], m2))
        m1 = pltpu.roll(cur, 1, 0)
        m1 = jnp.where(row == 0, prev8[7:8], m1)
        p1 = pltpu.roll(cur, sub - 1, 0)
        p1 = jnp.where(row == sub - 1, next8[0:1], p1)
        p2 = pltpu.roll(cur, sub - 2, 0)
        p2 = jnp.where(row == sub - 2, next8[0:1], jnp.where(row == sub - 1, next8[1:2], p2))
        out = cw[0:1] * m2
        out = out + cw[1:2] * m1
        out = out + cw[2:3] * cur
        out = out + cw[3:4] * p1
        out = out + cw[4:5] * p2
        out = out + cb
        return out * _sigmoid(out)

    def emit(r, res):
        if transpose_out:
            o_ref[:, r * sub:(r + 1) * sub] = res.T.astype(o_ref.dtype)
        else:
            o_ref[r * sub:(r + 1) * sub, :] = res.astype(o_ref.dtype)

    prev8 = jnp.where(at_start, 0.0, _dot(hp_ref[...], w))[8:16]
    blocks = [_dot(h_ref[0:sub, :], w)]
    for r in range(nsub):
        if r + 1 < nsub:
            blocks.append(_dot(h_ref[(r + 1) * sub:(r + 2) * sub, :], w))
            next8 = blocks[r + 1][0:8]
        else:
            next8 = jnp.where(at_end, 0.0, _dot(hn_ref[...], w))[0:8]
        emit(r, conv(blocks[r], prev8, next8))
        prev8 = blocks[r][sub - 8:sub]


def matmul_conv(h, w, col0, n, conv_w, conv_b, seq_len, transpose_out, bm=1024, bn=1024, sub=256):
    m, k = h.shape
    assert seq_len % bm == 0 and bm % sub == 0
    nhb = m // 16
    if transpose_out:
        out_spec = pl.BlockSpec((bn, bm), lambda j, i: (j, i))
        out_shape = jax.ShapeDtypeStruct((n, m), BF16)
    else:
        out_spec = pl.BlockSpec((bm, bn), lambda j, i: (i, j))
        out_shape = jax.ShapeDtypeStruct((m, n), BF16)
    return pl.pallas_call(
        functools.partial(_matmul_conv_kernel, bm=bm, seq_len=seq_len, transpose_out=transpose_out,
                          sub=sub),
        grid=(n // bn, m // bm),
        in_specs=[pl.BlockSpec((bm, k), lambda j, i: (i, 0)),
                  pl.BlockSpec((16, k), lambda j, i: (jnp.maximum(i * (bm // 16) - 1, 0), 0)),
                  pl.BlockSpec((16, k), lambda j, i: (jnp.minimum((i + 1) * (bm // 16), nhb - 1), 0)),
                  pl.BlockSpec((pl.Element(k), pl.Element(bn)),
                               lambda j, i: (0, pl.multiple_of(col0 + j * bn, 128))),
                  pl.BlockSpec((D_CONV, bn), lambda j, i: (0, j)),
                  pl.BlockSpec((1, bn), lambda j, i: (0, j))],
        out_specs=out_spec,
        out_shape=out_shape,
        compiler_params=_cparams(("parallel", "parallel")),
        name="matmul_conv_t" if transpose_out else "matmul_conv",
    )(h, h, h, w, conv_w, conv_b)


def _ssd_kernel(*refs, backward, cb, mxu_cols):
    if backward:
        (xsT_ref, b_ref, c_ref, nat_ref, tr_ref, yf_ref, z_ref, nw_ref, o_ref, s_ref) = refs
    else:
        (xsT_ref, b_ref, c_ref, nat_ref, tr_ref, dexp_ref, o_ref, s_ref) = refs
    G, R, P, N = SSD_GROUPS, 8, SSD_HEAD_DIM, SSD_STATE
    gw = R * P
    h0 = G * R if backward else 0

    @pl.when(pl.program_id(1) == 0)
    def _():
        s_ref[...] = jnp.zeros_like(s_ref)

    s_i = lax.broadcasted_iota(jnp.int32, (CHUNK, CHUNK), 0)
    l_i = lax.broadcasted_iota(jnp.int32, (CHUNK, CHUNK), 1)
    mask = (l_i <= s_i) if backward else (l_i >= s_i)
    k_i = lax.broadcasted_iota(jnp.int32, (128, 2 * CHUNK), 0)
    c_i = lax.broadcasted_iota(jnp.int32, (128, 2 * CHUNK), 1)
    acum_split = {}
    chunk_order = range(cb - 1, -1, -1) if backward else range(cb)
    for j, g in [(j, g) for j in chunk_order for g in range(G)]:
        ts = slice(j * CHUNK, (j + 1) * CHUNK)
        gs = slice(g * gw, (g + 1) * gw)
        hg = h0 + g * R
        col_pairs = []
        if mxu_cols:
            if j not in acum_split:
                acum_split[j] = _split3(nat_ref[ts, 128:256])
            hi, mid, lo = acum_split[j]
            for pr in range(mxu_cols // 2):
                sel = (k_i == hg + 2 * pr + c_i // CHUNK).astype(BF16)
                col_pairs.append(_dot(hi, sel) + _dot(mid, sel) + _dot(lo, sel))
        bm = b_ref[ts, g * N:(g + 1) * N]
        cm = c_ref[ts, g * N:(g + 1) * N]
        dt = tr_ref[hg:hg + R, ts]
        acum = tr_ref[128 + hg:128 + hg + R, ts]
        tot = tr_ref[256 + hg:256 + hg + R, ts]
        cbT = _dot_nt(bm, cm)
        s_old = s_ref[gs, :]
        yoffT = _dot_nt(s_old.astype(BF16), cm)
        e_acum = jnp.exp(acum)
        dte = jnp.exp(tot - acum)
        e_tot = jnp.exp(tot)
        y_pieces = []
        xd_pieces = []
        for r in range(R):
            hs = slice(g * gw + r * P, g * gw + (r + 1) * P)
            xr = xsT_ref[hs, ts].astype(F32)
            xdt = xr * dt[r:r + 1, :]
            if r < mxu_cols:
                col = col_pairs[r // 2][:, (r % 2) * CHUNK:(r % 2 + 1) * CHUNK]
            else:
                col = nat_ref[ts, 128 + hg + r:128 + hg + r + 1]
            seg = acum[r:r + 1, :] - col
            dec = jnp.exp(jnp.where(mask, seg, -jnp.inf))
            mt = (cbT * dec).astype(BF16)
            y_r = _dot(xdt.astype(BF16), mt) + yoffT[r * P:(r + 1) * P, :] * e_acum[r:r + 1, :]
            if not backward:
                y_r = y_r + dexp_ref[hs, :] * xr
            y_pieces.append(y_r)
            xd_pieces.append((xdt * dte[r:r + 1, :]).astype(BF16))
        yT = jnp.concatenate(y_pieces, axis=0)
        upd = _dot(jnp.concatenate(xd_pieces, axis=0), bm)
        for r in range(R):
            hs = slice(r * P, (r + 1) * P)
            s_ref[g * gw + r * P:g * gw + (r + 1) * P, :] = s_old[hs, :] * e_tot[r:r + 1, :] + upd[hs, :]
        y = yT.T
        if backward:
            y = y + yf_ref[ts, gs]
            z = z_ref[ts, gs].astype(F32)
            y = y * (z * _sigmoid(z))
            y = y * lax.rsqrt(jnp.mean(y * y, axis=-1, keepdims=True) + EPS)
            o_ref[ts, gs] = (y * nw_ref[:, gs]).astype(o_ref.dtype)
        else:
            o_ref[ts, gs] = y


def ssd_scan(xsT, bc, ssd, ssdT, batch, seq_len, *, backward, dexp=None,
             y_fwd=None, proj=None, norm_w=None):
    m = xsT.shape[1]
    cb = CHUNKS_PER_STEP
    tb = CHUNK * cb
    nc = seq_len // tb
    G = SSD_GROUPS
    d_ssd = xsT.shape[0]

    def cg(b, c):
        return b * nc + ((nc - 1 - c) if backward else c)

    in_specs = [
        pl.BlockSpec((d_ssd, tb), lambda b, c: (0, cg(b, c))),
        pl.BlockSpec((tb, G * SSD_STATE), lambda b, c: (cg(b, c), 0)),
        pl.BlockSpec((tb, G * SSD_STATE), lambda b, c: (cg(b, c), 1)),
        pl.BlockSpec((tb, 384), lambda b, c: (cg(b, c), 0)),
        pl.BlockSpec((384, tb), lambda b, c: (0, cg(b, c))),
    ]
    args = [xsT, bc, bc, ssd, ssdT]
    if backward:
        in_specs += [
            pl.BlockSpec((tb, d_ssd), lambda b, c: (cg(b, c), 0)),
            pl.BlockSpec((tb, d_ssd), lambda b, c: (cg(b, c), 0)),
            pl.BlockSpec((1, d_ssd), lambda b, c: (0, 0)),
        ]
        args += [y_fwd, proj, norm_w]
        out_dtype = BF16
    else:
        in_specs += [pl.BlockSpec((d_ssd, 128), lambda b, c: (0, 0))]
        args += [dexp]
        out_dtype = F32
    return pl.pallas_call(
        functools.partial(_ssd_kernel, backward=backward, cb=cb, mxu_cols=2 if backward else 0),
        grid=(batch, nc),
        in_specs=in_specs,
        out_specs=pl.BlockSpec((tb, d_ssd), lambda b, c: (cg(b, c), 0)),
        out_shape=jax.ShapeDtypeStruct((m, d_ssd), out_dtype),
        scratch_shapes=[pltpu.VMEM((d_ssd, SSD_STATE), F32)],
        compiler_params=_cparams(("parallel", "arbitrary")),
        name="ssd_bwd" if backward else "ssd_fwd",
    )(*args)


def _mlstm_kernel(*refs, backward, dk, dv, heads_per_group, cb):
    if backward:
        (q_ref, k_ref, v_ref, nat_ref, tr_ref, hf_ref, og_ref, nw_ref, o_ref, c_ref, m_ref) = refs
    else:
        (q_ref, k_ref, v_ref, nat_ref, tr_ref, o_ref, c_ref, m_ref) = refs
    H = MLSTM_HEADS

    @pl.when(pl.program_id(1) == 0)
    def _():
        c_ref[...] = jnp.zeros_like(c_ref)
        m_ref[...] = jnp.zeros_like(m_ref)

    t_i = lax.broadcasted_iota(jnp.int32, (CHUNK, CHUNK), 0)
    s_i = lax.broadcasted_iota(jnp.int32, (CHUNK, CHUNK), 1)
    mask = (s_i >= t_i) if backward else (s_i <= t_i)
    ones_blk = jnp.ones((CHUNK, 128), BF16)
    scale = dk ** -0.5
    dense = (CHUNK, CHUNK)

    chunk_order = range(cb - 1, -1, -1) if backward else range(cb)
    for j, g0 in [(j, g0) for j in chunk_order for g0 in range(0, H, heads_per_group)]:
        ts = slice(j * CHUNK, (j + 1) * CHUNK)
        hs = list(range(g0, g0 + heads_per_group))
        lane_of = {h: h + (H if backward else 0) for h in hs}
        st = {h: {} for h in hs}
        for h in hs:
            d, hh = st[h], lane_of[h]
            li_row = tr_ref[hh:hh + 1, ts]
            cum_row = tr_ref[160 + hh:161 + hh, ts]
            d["tot"] = tr_ref[288 + hh:289 + hh, ts]
            d["base_row"] = cum_row - li_row
            grow = d["tot"] - d["base_row"]
            d["m_in"] = m_ref[h, 0:1, :]
            m_loc = jnp.broadcast_to(jnp.max(grow, axis=1, keepdims=True), (1, CHUNK))
            d["m_new"] = jnp.maximum(d["tot"] + d["m_in"], m_loc)
            d["w_row"] = jnp.exp(grow - d["m_new"])
            d["cum_d"] = jnp.broadcast_to(nat_ref[ts, 160 + hh:161 + hh], dense)
            d["rmax_d"] = jnp.broadcast_to(nat_ref[ts, 416 + hh:417 + hh], dense)
            d["v_aug"] = jnp.concatenate([v_ref[ts, h * dv:(h + 1) * dv].astype(BF16), ones_blk], axis=1)
        for h in hs:
            d = st[h]
            k = k_ref[ts, h * dk:(h + 1) * dk].astype(F32)
            d["kb"] = k.astype(BF16)
            d["kwT"] = (k.T * d["w_row"]).astype(BF16)
            d["qs"] = (q_ref[ts, h * dk:(h + 1) * dk].astype(F32) * scale).astype(BF16)
        for h in hs:
            d = st[h]
            d["c_loc"] = _dot(d["kwT"], d["v_aug"])
            d["sqk"] = _dot_nt(d["qs"], d["kb"])
            d["c_in"] = c_ref[h]
            d["qc"] = _dot(d["qs"], d["c_in"].astype(BF16))
        for h in hs:
            d = st[h]
            dlog = jnp.where(mask, d["cum_d"] - d["base_row"], -jnp.inf)
            inter = d["cum_d"] + d["m_in"]
            d["m_t"] = jnp.maximum(d["rmax_d"], inter)
            d["pm"] = (jnp.exp(dlog - d["m_t"]) * d["sqk"]).astype(BF16)
            d["a_inter"] = jnp.exp(inter - d["m_t"])
        for h in hs:
            d = st[h]
            a3 = jnp.concatenate([d["a_inter"]] * (dv // 128 + 1), axis=1)
            num = _dot(d["pm"], d["v_aug"]) + d["qc"] * a3
            den = num[:, dv:dv + 128]
            inv = 1.0 / jnp.maximum(jnp.abs(den), jnp.exp(-d["m_t"]))
            d["hout"] = num[:, 0:dv] * jnp.concatenate([inv] * (dv // 128), axis=1)
        for h in hs:
            d = st[h]
            a_prev = jnp.exp(d["tot"] + d["m_in"] - d["m_new"])
            a_prev3 = jnp.concatenate([a_prev] * (dv // 128 + 1), axis=1)
            c_ref[h] = a_prev3 * d["c_in"] + d["c_loc"]
            m_ref[h] = jnp.broadcast_to(d["m_new"], (8, 128))
        for h in hs:
            hout = st[h]["hout"]
            vs = slice(h * dv, (h + 1) * dv)
            if backward:
                hout = hout + hf_ref[ts, vs]
                hout = hout * lax.rsqrt(jnp.mean(hout * hout, axis=-1, keepdims=True) + EPS)
                hout = hout * nw_ref[:, vs]
                o_ref[ts, vs] = (_sigmoid(og_ref[ts, vs].astype(F32)) * hout).astype(o_ref.dtype)
            else:
                o_ref[ts, vs] = hout


def mlstm_scan(proj, ml, mlT, batch, seq_len, cols, *, backward, h_fwd=None, norm_w=None):
    m = proj.shape[0]
    cb = CHUNKS_PER_STEP
    tb = CHUNK * cb
    nc = seq_len // tb
    H = MLSTM_HEADS
    dk, dv = cols["dk"], cols["dv"]
    qw, vw = H * dk, H * dv
    qb, kb, vb, ob = cols["q"] // qw, cols["k"] // qw, cols["v"] // vw, cols["o"] // vw
    assert qb * qw == cols["q"] and kb * qw == cols["k"] and vb * vw == cols["v"] and ob * vw == cols["o"]

    def cg(b, c):
        return b * nc + ((nc - 1 - c) if backward else c)

    in_specs = [
        pl.BlockSpec((tb, qw), lambda b, c: (cg(b, c), qb)),
        pl.BlockSpec((tb, qw), lambda b, c: (cg(b, c), kb)),
        pl.BlockSpec((tb, vw), lambda b, c: (cg(b, c), vb)),
        pl.BlockSpec((tb, 512), lambda b, c: (cg(b, c), 0)),
        pl.BlockSpec((384, tb), lambda b, c: (0, cg(b, c))),
    ]
    args = [proj, proj, proj, ml, mlT]
    if backward:
        in_specs += [
            pl.BlockSpec((tb, vw), lambda b, c: (cg(b, c), 0)),
            pl.BlockSpec((tb, vw), lambda b, c: (cg(b, c), ob)),
            pl.BlockSpec((1, vw), lambda b, c: (0, 0)),
        ]
        args += [h_fwd, proj, norm_w]
        out_dtype = BF16
    else:
        out_dtype = F32
    return pl.pallas_call(
        functools.partial(_mlstm_kernel, backward=backward, dk=dk, dv=dv,
                          heads_per_group=2 if backward else 4, cb=cb),
        grid=(batch, nc),
        in_specs=in_specs,
        out_specs=pl.BlockSpec((tb, vw), lambda b, c: (cg(b, c), 0)),
        out_shape=jax.ShapeDtypeStruct((m, vw), out_dtype),
        scratch_shapes=[pltpu.VMEM((H, dk, dv + 128), F32), pltpu.VMEM((H, 8, 128), F32)],
        compiler_params=_cparams(("parallel", "arbitrary")),
        name="mlstm_bwd" if backward else "mlstm_fwd",
    )(*args)


def _outproj_kernel(a1_ref, a2_ref, w1_ref, w2_ref, x_ref, nw_ref, o_ref, xw_ref, ssq_ref):
    acc = _dot(a1_ref[...], w1_ref[...]) + _dot(a2_ref[...], w2_ref[...])
    x1 = x_ref[...] + acc
    o_ref[...] = x1
    xw_ref[...] = (x1 * nw_ref[...]).astype(xw_ref.dtype)
    ssq_ref[...] = jnp.broadcast_to(jnp.sum(x1 * x1, axis=-1, keepdims=True), ssq_ref.shape)


def outproj_residual(a1, a2, w1, w2, x, norm_w, bm=512, bn=512):
    m, k = a1.shape
    n = w1.shape[1]
    return pl.pallas_call(
        _outproj_kernel,
        grid=(n // bn, m // bm),
        in_specs=[pl.BlockSpec((bm, k), lambda j, i: (i, 0)),
                  pl.BlockSpec((bm, k), lambda j, i: (i, 0)),
                  pl.BlockSpec((k, bn), lambda j, i: (0, j)),
                  pl.BlockSpec((k, bn), lambda j, i: (0, j)),
                  pl.BlockSpec((bm, bn), lambda j, i: (i, j)),
                  pl.BlockSpec((1, bn), lambda j, i: (0, j))],
        out_specs=[pl.BlockSpec((bm, bn), lambda j, i: (i, j)),
                   pl.BlockSpec((bm, bn), lambda j, i: (i, j)),
                   pl.BlockSpec((bm, 128), lambda j, i: (i, j))],
        out_shape=[jax.ShapeDtypeStruct((m, n), F32), jax.ShapeDtypeStruct((m, n), BF16),
                   jax.ShapeDtypeStruct((m, (n // bn) * 128), F32)],
        compiler_params=_cparams(("parallel", "parallel")),
        name="outproj_residual",
    )(a1, a2, w1, w2, x, norm_w.reshape(1, n).astype(F32))


def _prep_layer(norm1_w, w_in, conv_w, conv_b, dt_bias, a_log, d_skip, ssd_norm_w, b_i, b_f,
                mlstm_norm_w, w_out, norm2_w, w_up, w_down):
    d_model = w_in.shape[0]
    d_mix = w_out.shape[0]
    d_ssd = d_mix // 2
    d_ml = d_mix - d_ssd
    n_ssd_heads = d_ssd // SSD_HEAD_DIM
    xbc_w = d_ssd + 2 * SSD_GROUPS * SSD_STATE
    dv = d_ml // MLSTM_HEADS
    dk = dv // 2
    widths = (d_ssd, xbc_w, 2 * n_ssd_heads, MLSTM_HEADS * dk, MLSTM_HEADS * dk, d_ml, d_ml,
              2 * MLSTM_HEADS, 2 * MLSTM_HEADS)
    offs = [0]
    for wd in widths:
        offs.append(offs[-1] + wd)
    assert offs[-1] == w_in.shape[1]
    assert 2 * n_ssd_heads == 128 and 2 * MLSTM_HEADS == 32
    seg = lambda i: w_in[:, offs[i]:offs[i + 1]]
    w_bf = w_in.astype(BF16)
    w_gate = jnp.concatenate([seg(2), seg(7), seg(8), jnp.zeros((d_model, 64), w_in.dtype)],
                             axis=1).astype(BF16)
    gate_bias = jnp.concatenate([dt_bias.reshape(-1), b_i.reshape(-1), b_f.reshape(-1),
                                 jnp.zeros((64,), F32)]).astype(F32).reshape(1, 256)
    cols = {"q": 0, "k": MLSTM_HEADS * dk, "v": 2 * MLSTM_HEADS * dk}
    cols["o"] = cols["v"] + d_ml
    cols["dk"], cols["dv"], cols["d_ssd"] = dk, dv, d_ssd
    return dict(
        norm1_w=norm1_w, w_bf=w_bf, offs=offs, w_gate=w_gate, gate_bias=gate_bias,
        alog=a_log.reshape(1, 128).astype(F32),
        conv_w=conv_w.astype(F32), conv_b=conv_b.reshape(1, -1).astype(F32),
        dexp=jnp.broadcast_to(jnp.repeat(d_skip.astype(F32), SSD_HEAD_DIM)[:, None], (d_ssd, 128)),
        ssd_norm_w=ssd_norm_w.reshape(1, -1).astype(F32),
        mlstm_norm_w=mlstm_norm_w.reshape(1, -1).astype(F32),
        w_out1=w_out[:d_ssd].astype(BF16), w_out2=w_out[d_ssd:].astype(BF16),
        norm2_w=norm2_w, w_up=w_up.astype(BF16), w_down=w_down.astype(BF16), cols=cols)


def _layer(x, p, batch, seq_len):
    cols = p["cols"]
    d_ssd = cols["d_ssd"]
    h, ssd, ssdT, ml, mlT = gates(x, p["norm1_w"], p["w_gate"], p["gate_bias"], p["alog"])
    offs = p["offs"]
    proj_z = matmul(h, p["w_bf"], F32, col0=offs[0], n=d_ssd)
    proj_b = matmul(h, p["w_bf"], BF16, col0=offs[3], n=offs[7] - offs[3])
    xsT = matmul_conv(h, p["w_bf"], offs[1], d_ssd, p["conv_w"][:, :d_ssd], p["conv_b"][:, :d_ssd],
                      seq_len, True)
    bc = matmul_conv(h, p["w_bf"], offs[1] + d_ssd, 2 * SSD_GROUPS * SSD_STATE,
                     p["conv_w"][:, d_ssd:], p["conv_b"][:, d_ssd:], seq_len, False)
    y_f = ssd_scan(xsT, bc, ssd, ssdT, batch, seq_len, backward=False, dexp=p["dexp"])
    mix1 = ssd_scan(xsT, bc, ssd, ssdT, batch, seq_len, backward=True, y_fwd=y_f,
                    proj=proj_z, norm_w=p["ssd_norm_w"])
    h_f = mlstm_scan(proj_b, ml, mlT, batch, seq_len, cols, backward=False)
    mix2 = mlstm_scan(proj_b, ml, mlT, batch, seq_len, cols, backward=True, h_fwd=h_f,
                      norm_w=p["mlstm_norm_w"])
    x1, x1w, ssq = outproj_residual(mix1, mix2, p["w_out1"], p["w_out2"], x, p["norm2_w"])
    u = matmul(x1w, p["w_up"], BF16, relu2=True, row_ssq=ssq)
    return matmul_ksplit_residual(u, p["w_down"], x1)


def _trunk(x, layers, final_norm_w):
    batch, seq_len, d = x.shape
    xf = x.reshape(batch * seq_len, d)
    for p in layers:
        xf = _layer(xf, p, batch, seq_len)
    return rmsnorm_rows(xf, final_norm_w, F32).reshape(batch, seq_len, d)


def kernel(x_prompt, x_sample, norm1_w, w_in, conv_w, conv_b, dt_bias, a_log, d_skip, ssd_norm_w,
           b_i, b_f, mlstm_norm_w, w_out, norm2_w, w_up, w_down, final_norm_w):
    depth = w_in.shape[0]
    layers = [_prep_layer(norm1_w[l], w_in[l], conv_w[l], conv_b[l], dt_bias[l], a_log[l], d_skip[l],
                          ssd_norm_w[l], b_i[l], b_f[l], mlstm_norm_w[l], w_out[l], norm2_w[l],
                          w_up[l], w_down[l]) for l in range(depth)]
    y_prompt = _trunk(x_prompt, layers, final_norm_w)
    y_sample = _trunk(x_sample, layers, final_norm_w)
    return (y_prompt, y_sample)
```

```python
import functools

import jax
import jax.numpy as jnp
from jax import lax
from jax.experimental import pallas as pl
from jax.experimental.pallas import tpu as pltpu

F32 = jnp.float32
BF16 = jnp.bfloat16

CHUNK = 128
CHUNKS_PER_STEP = 2
EPS = 1e-5
D_CONV = 5
SSD_GROUPS = 8
SSD_HEAD_DIM = 64
SSD_STATE = 128
MLSTM_HEADS = 16
VMEM_LIMIT = 56 * 1024 * 1024
KSPLIT_VMEM_LIMIT = 60 * 1024 * 1024


def _cparams(sem, vmem_limit=VMEM_LIMIT):
    return pltpu.CompilerParams(dimension_semantics=sem, vmem_limit_bytes=vmem_limit)


def _sigmoid(x):
    return 1.0 / (1.0 + jnp.exp(-x))


def _softplus(x):
    return jnp.maximum(x, 0.0) + jnp.log1p(jnp.exp(-jnp.abs(x)))


def _dot(a, b):
    return jnp.dot(a, b, preferred_element_type=F32)


def _dot_nt(a, b):
    return lax.dot_general(a, b, (((1,), (1,)), ((), ())), preferred_element_type=F32)


def _rmsnorm_kernel(x_ref, w_ref, o_ref):
    x = x_ref[...].astype(F32)
    y = x * lax.rsqrt(jnp.mean(x * x, axis=-1, keepdims=True) + EPS)
    o_ref[...] = (y * w_ref[...]).astype(o_ref.dtype)


def rmsnorm_rows(x, w, out_dtype, bm=512):
    m, d = x.shape
    return pl.pallas_call(
        _rmsnorm_kernel,
        grid=(m // bm,),
        in_specs=[pl.BlockSpec((bm, d), lambda i: (i, 0)),
                  pl.BlockSpec((1, d), lambda i: (0, 0))],
        out_specs=pl.BlockSpec((bm, d), lambda i: (i, 0)),
        out_shape=jax.ShapeDtypeStruct((m, d), out_dtype),
        compiler_params=_cparams(("parallel",)),
        name="rmsnorm_rows",
    )(x, w.reshape(1, d).astype(F32))


def _matmul_kernel(*refs, relu2, row_ssq_dim):
    if row_ssq_dim:
        a_ref, b_ref, ssq_ref, o_ref = refs
    else:
        a_ref, b_ref, o_ref = refs
    acc = _dot(a_ref[...], b_ref[...])
    if relu2:
        acc = jnp.maximum(acc, 0.0)
        acc = acc * acc
    if row_ssq_dim:
        ssq = ssq_ref[...]
        tot = ssq[:, 0:128]
        for part in range(1, ssq.shape[1] // 128):
            tot = tot + ssq[:, part * 128:(part + 1) * 128]
        r2 = 1.0 / (tot * (1.0 / row_ssq_dim) + EPS)
        acc = acc * jnp.concatenate([r2] * (acc.shape[1] // 128), axis=1)
    o_ref[...] = acc.astype(o_ref.dtype)


def matmul(a, b, out_dtype, bm=1024, bn=1024, relu2=False, col0=0, n=None, row_ssq=None):
    m, k = a.shape
    n = b.shape[1] if n is None else n
    in_specs = [pl.BlockSpec((bm, k), lambda j, i: (i, 0)),
                pl.BlockSpec((pl.Element(k), pl.Element(bn)),
                             lambda j, i: (0, pl.multiple_of(col0 + j * bn, 128)))]
    args = [a, b]
    if row_ssq is not None:
        assert relu2
        in_specs.append(pl.BlockSpec((bm, row_ssq.shape[1]), lambda j, i: (i, 0)))
        args.append(row_ssq)
    return pl.pallas_call(
        functools.partial(_matmul_kernel, relu2=relu2, row_ssq_dim=k if row_ssq is not None else 0),
        grid=(n // bn, m // bm),
        in_specs=in_specs,
        out_specs=pl.BlockSpec((bm, bn), lambda j, i: (i, j)),
        out_shape=jax.ShapeDtypeStruct((m, n), out_dtype),
        compiler_params=_cparams(("parallel", "parallel")),
        name="matmul_relu2" if relu2 else "matmul",
    )(*args)


def _matmul_ksplit_res_kernel(a_ref, b_ref, x_ref, o_ref):
    @pl.when(pl.program_id(2) == 0)
    def _():
        o_ref[...] = x_ref[...] + _dot(a_ref[...], b_ref[...])

    @pl.when(pl.program_id(2) != 0)
    def _():
        o_ref[...] = o_ref[...] + _dot(a_ref[...], b_ref[...])


def matmul_ksplit_residual(a, b, x, bm=1024, bn=1024, bk=4096):
    m, k = a.shape
    n = b.shape[1]
    return pl.pallas_call(
        _matmul_ksplit_res_kernel,
        grid=(n // bn, m // bm, k // bk),
        in_specs=[pl.BlockSpec((bm, bk), lambda j, i, kk: (i, kk)),
                  pl.BlockSpec((bk, bn), lambda j, i, kk: (kk, j)),
                  pl.BlockSpec((bm, bn), lambda j, i, kk: (i, j))],
        out_specs=pl.BlockSpec((bm, bn), lambda j, i, kk: (i, j)),
        out_shape=jax.ShapeDtypeStruct((m, n), F32),
        compiler_params=_cparams(("parallel", "parallel", "arbitrary"), KSPLIT_VMEM_LIMIT),
        name="matmul_ksplit_residual",
    )(a, b, x)


def _split3(x):
    hi = x.astype(BF16)
    r1 = x - hi.astype(F32)
    mid = r1.astype(BF16)
    r2 = r1 - mid.astype(F32)
    return hi, mid, r2.astype(BF16)


def _gates_kernel(x_ref, nw_ref, w_ref, bias_ref, alog_ref,
                  h_ref, ssd_ref, ssdT_ref, ml_ref, mlT_ref, *, bm):
    x = x_ref[...]
    h = (x * lax.rsqrt(jnp.mean(x * x, axis=-1, keepdims=True) + EPS) * nw_ref[...]).astype(BF16)
    h_ref[...] = h
    raw = _dot(h, w_ref[...]) + bias_ref[...]
    dt = _softplus(raw[:, 0:128])
    a = dt * (-jnp.exp(alog_ref[...]))
    t2 = raw[:, 128:256]
    lane = lax.broadcasted_iota(jnp.int32, (CHUNK, 128), 1)
    row_t = lax.broadcasted_iota(jnp.int32, (CHUNK, 128), 0)
    lsig = -_softplus(-t2)
    row_i = lax.broadcasted_iota(jnp.int32, (CHUNK, CHUNK), 0)
    col_i = lax.broadcasted_iota(jnp.int32, (CHUNK, CHUNK), 1)
    lower = (col_i <= row_i).astype(BF16)
    upper = (col_i >= row_i).astype(BF16)
    ones = jnp.ones((CHUNK, CHUNK), BF16)
    ssd_fwd_lane = lane < 64
    ml_fwd_lane = (lane % 32) < 16

    def cums(x, fwd_lane):
        hi, mid, lo = _split3(x)
        cum_f = _dot(lower, hi) + _dot(lower, mid) + _dot(lower, lo)
        cum_b = _dot(upper, hi) + _dot(upper, mid) + _dot(upper, lo)
        tot = _dot(ones, hi) + _dot(ones, mid) + _dot(ones, lo)
        return jnp.where(fwd_lane, cum_f, cum_b), tot

    for c in range(bm // CHUNK):
        sl = slice(c * CHUNK, (c + 1) * CHUNK)
        dt_c = dt[sl]
        acum, tot = cums(a[sl], ssd_fwd_lane)
        ssd_ref[sl, 0:128] = dt_c
        ssd_ref[sl, 128:256] = acum
        ssd_ref[sl, 256:384] = tot
        ssdT_ref[0:128, sl] = dt_c.T
        ssdT_ref[128:256, sl] = acum.T
        ssdT_ref[256:384, sl] = tot.T
        t2_c = t2[sl]
        mcum, mtot = cums(lsig[sl], ml_fwd_lane)
        y = mcum - pltpu.roll(t2_c, 32, 1)
        y_f, y_b = y, y
        for d in (1, 2, 4, 8, 16, 32, 64):
            y_f = jnp.minimum(y_f, jnp.where(row_t >= d, pltpu.roll(y_f, d, 0), jnp.inf))
            y_b = jnp.minimum(y_b, jnp.where(row_t < CHUNK - d, pltpu.roll(y_b, CHUNK - d, 0), jnp.inf))
        rmax = mcum - jnp.where(ml_fwd_lane, y_f, y_b)
        ml_ref[sl, 0:128] = t2_c
        ml_ref[sl, 128:256] = mcum
        ml_ref[sl, 256:384] = mtot
        ml_ref[sl, 384:512] = rmax
        mlT_ref[0:128, sl] = t2_c.T
        mlT_ref[128:256, sl] = mcum.T
        mlT_ref[256:384, sl] = mtot.T


def gates(x, norm_w, w_gate, bias, alog, bm=512):
    m, d = x.shape
    nat = pl.BlockSpec((bm, 384), lambda i: (i, 0))
    nat4 = pl.BlockSpec((bm, 512), lambda i: (i, 0))
    tr = pl.BlockSpec((384, bm), lambda i: (0, i))
    return pl.pallas_call(
        functools.partial(_gates_kernel, bm=bm),
        grid=(m // bm,),
        in_specs=[pl.BlockSpec((bm, d), lambda i: (i, 0)),
                  pl.BlockSpec((1, d), lambda i: (0, 0)),
                  pl.BlockSpec((d, 256), lambda i: (0, 0)),
                  pl.BlockSpec((1, 256), lambda i: (0, 0)),
                  pl.BlockSpec((1, 128), lambda i: (0, 0))],
        out_specs=[pl.BlockSpec((bm, d), lambda i: (i, 0)), nat, tr, nat4, tr],
        out_shape=[jax.ShapeDtypeStruct((m, d), BF16),
                   jax.ShapeDtypeStruct((m, 384), F32), jax.ShapeDtypeStruct((384, m), F32),
                   jax.ShapeDtypeStruct((m, 512), F32), jax.ShapeDtypeStruct((384, m), F32)],
        compiler_params=_cparams(("parallel",)),
        name="gates",
    )(x, norm_w.reshape(1, d).astype(F32), w_gate, bias, alog)


def _matmul_conv_kernel(h_ref, hp_ref, hn_ref, w_ref, cw_ref, cb_ref, o_ref, *, bm, seq_len,
                        transpose_out, sub):
    i = pl.program_id(1)
    w = w_ref[...]
    at_start = (i * bm) % seq_len == 0
    at_end = ((i + 1) * bm) % seq_len == 0
    cw = cw_ref[...]
    cb = cb_ref[...]
    nsub = bm // sub
    row = lax.broadcasted_iota(jnp.int32, (sub, w.shape[1]), 0)

    def conv(cur, prev8, next8):
        m2 = pltpu.roll(cur, 2, 0)
        m2 = jnp.where(row == 0, prev8[6:7], jnp.where(row == 1, prev8[7:8], m2))
        m1 = pltpu.roll(cur, 1, 0)
        m1 = jnp.where(row == 0, prev8[7:8], m1)
        p1 = pltpu.roll(cur, sub - 1, 0)
        p1 = jnp.where(row == sub - 1, next8[0:1], p1)
        p2 = pltpu.roll(cur, sub - 2, 0)
        p2 = jnp.where(row == sub - 2, next8[0:1], jnp.where(row == sub - 1, next8[1:2], p2))
        out = cw[0:1] * m2
        out = out + cw[1:2] * m1
        out = out + cw[2:3] * cur
        out = out + cw[3:4] * p1
        out = out + cw[4:5] * p2
        out = out + cb
        return out * _sigmoid(out)

    def emit(r, res):
        if transpose_out:
            o_ref[:, r * sub:(r + 1) * sub] = res.T.astype(o_ref.dtype)
        else:
            o_ref[r * sub:(r + 1) * sub, :] = res.astype(o_ref.dtype)

    prev8 = jnp.where(at_start, 0.0, _dot(hp_ref[...], w))[8:16]
    blocks = [_dot(h_ref[0:sub, :], w)]
    for r in range(nsub):
        if r + 1 < nsub:
            blocks.append(_dot(h_ref[(r + 1) * sub:(r + 2) * sub, :], w))
            next8 = blocks[r + 1][0:8]
        else:
            next8 = jnp.where(at_end, 0.0, _dot(hn_ref[...], w))[0:8]
        emit(r, conv(blocks[r], prev8, next8))
        prev8 = blocks[r][sub - 8:sub]


def matmul_conv(h, w, col0, n, conv_w, conv_b, seq_len, transpose_out, bm=1024, bn=1024, sub=256):
    m, k = h.shape
    assert seq_len % bm == 0 and bm % sub == 0
    nhb = m // 16
    if transpose_out:
        out_spec = pl.BlockSpec((bn, bm), lambda j, i: (j, i))
        out_shape = jax.ShapeDtypeStruct((n, m), BF16)
    else:
        out_spec = pl.BlockSpec((bm, bn), lambda j, i: (i, j))
        out_shape = jax.ShapeDtypeStruct((m, n), BF16)
    return pl.pallas_call(
        functools.partial(_matmul_conv_kernel, bm=bm, seq_len=seq_len, transpose_out=transpose_out,
                          sub=sub),
        grid=(n // bn, m // bm),
        in_specs=[pl.BlockSpec((bm, k), lambda j, i: (i, 0)),
                  pl.BlockSpec((16, k), lambda j, i: (jnp.maximum(i * (bm // 16) - 1, 0), 0)),
                  pl.BlockSpec((16, k), lambda j, i: (jnp.minimum((i + 1) * (bm // 16), nhb - 1), 0)),
                  pl.BlockSpec((pl.Element(k), pl.Element(bn)),
                               lambda j, i: (0, pl.multiple_of(col0 + j * bn, 128))),
                  pl.BlockSpec((D_CONV, bn), lambda j, i: (0, j)),
                  pl.BlockSpec((1, bn), lambda j, i: (0, j))],
        out_specs=out_spec,
        out_shape=out_shape,
        compiler_params=_cparams(("parallel", "parallel")),
        name="matmul_conv_t" if transpose_out else "matmul_conv",
    )(h, h, h, w, conv_w, conv_b)


def _ssd_kernel(*refs, backward, cb, mxu_cols):
    if backward:
        (xsT_ref, b_ref, c_ref, nat_ref, tr_ref, yf_ref, z_ref, nw_ref, o_ref, s_ref) = refs
    else:
        (xsT_ref, b_ref, c_ref, nat_ref, tr_ref, dexp_ref, o_ref, s_ref) = refs
    G, R, P, N = SSD_GROUPS, 8, SSD_HEAD_DIM, SSD_STATE
    gw = R * P
    h0 = G * R if backward else 0

    @pl.when(pl.program_id(1) == 0)
    def _():
        s_ref[...] = jnp.zeros_like(s_ref)

    s_i = lax.broadcasted_iota(jnp.int32, (CHUNK, CHUNK), 0)
    l_i = lax.broadcasted_iota(jnp.int32, (CHUNK, CHUNK), 1)
    mask = (l_i <= s_i) if backward else (l_i >= s_i)
    k_i = lax.broadcasted_iota(jnp.int32, (128, 2 * CHUNK), 0)
    c_i = lax.broadcasted_iota(jnp.int32, (128, 2 * CHUNK), 1)
    acum_split = {}
    chunk_order = range(cb - 1, -1, -1) if backward else range(cb)
    for j, g in [(j, g) for j in chunk_order for g in range(G)]:
        ts = slice(j * CHUNK, (j + 1) * CHUNK)
        gs = slice(g * gw, (g + 1) * gw)
        hg = h0 + g * R
        col_pairs = []
        if mxu_cols:
            if j not in acum_split:
                acum_split[j] = _split3(nat_ref[ts, 128:256])
            hi, mid, lo = acum_split[j]
            for pr in range(mxu_cols // 2):
                sel = (k_i == hg + 2 * pr + c_i // CHUNK).astype(BF16)
                col_pairs.append(_dot(hi, sel) + _dot(mid, sel) + _dot(lo, sel))
        bm = b_ref[ts, g * N:(g + 1) * N]
        cm = c_ref[ts, g * N:(g + 1) * N]
        dt = tr_ref[hg:hg + R, ts]
        acum = tr_ref[128 + hg:128 + hg + R, ts]
        tot = tr_ref[256 + hg:256 + hg + R, ts]
        cbT = _dot_nt(bm, cm)
        s_old = s_ref[gs, :]
        yoffT = _dot_nt(s_old.astype(BF16), cm)
        e_acum = jnp.exp(acum)
        dte = jnp.exp(tot - acum)
        e_tot = jnp.exp(tot)
        y_pieces = []
        xd_pieces = []
        for r in range(R):
            hs = slice(g * gw + r * P, g * gw + (r + 1) * P)
            xr = xsT_ref[hs, ts].astype(F32)
            xdt = xr * dt[r:r + 1, :]
            if r < mxu_cols:
                col = col_pairs[r // 2][:, (r % 2) * CHUNK:(r % 2 + 1) * CHUNK]
            else:
                col = nat_ref[ts, 128 + hg + r:128 + hg + r + 1]
            seg = acum[r:r + 1, :] - col
            dec = jnp.exp(jnp.where(mask, seg, -jnp.inf))
            mt = (cbT * dec).astype(BF16)
            y_r = _dot(xdt.astype(BF16), mt) + yoffT[r * P:(r + 1) * P, :] * e_acum[r:r + 1, :]
            if not backward:
                y_r = y_r + dexp_ref[hs, :] * xr
            y_pieces.append(y_r)
            xd_pieces.append((xdt * dte[r:r + 1, :]).astype(BF16))
        yT = jnp.concatenate(y_pieces, axis=0)
        upd = _dot(jnp.concatenate(xd_pieces, axis=0), bm)
        for r in range(R):
            hs = slice(r * P, (r + 1) * P)
            s_ref[g * gw + r * P:g * gw + (r + 1) * P, :] = s_old[hs, :] * e_tot[r:r + 1, :] + upd[hs, :]
        y = yT.T
        if backward:
            y = y + yf_ref[ts, gs]
            z = z_ref[ts, gs].astype(F32)
            y = y * (z * _sigmoid(z))
            y = y * lax.rsqrt(jnp.mean(y * y, axis=-1, keepdims=True) + EPS)
            o_ref[ts, gs] = (y * nw_ref[:, gs]).astype(o_ref.dtype)
        else:
            o_ref[ts, gs] = y


def ssd_scan(xsT, bc, ssd, ssdT, batch, seq_len, *, backward, dexp=None,
             y_fwd=None, proj=None, norm_w=None):
    m = xsT.shape[1]
    cb = CHUNKS_PER_STEP
    tb = CHUNK * cb
    nc = seq_len // tb
    G = SSD_GROUPS
    d_ssd = xsT.shape[0]

    def cg(b, c):
        return b * nc + ((nc - 1 - c) if backward else c)

    in_specs = [
        pl.BlockSpec((d_ssd, tb), lambda b, c: (0, cg(b, c))),
        pl.BlockSpec((tb, G * SSD_STATE), lambda b, c: (cg(b, c), 0)),
        pl.BlockSpec((tb, G * SSD_STATE), lambda b, c: (cg(b, c), 1)),
        pl.BlockSpec((tb, 384), lambda b, c: (cg(b, c), 0)),
        pl.BlockSpec((384, tb), lambda b, c: (0, cg(b, c))),
    ]
    args = [xsT, bc, bc, ssd, ssdT]
    if backward:
        in_specs += [
            pl.BlockSpec((tb, d_ssd), lambda b, c: (cg(b, c), 0)),
            pl.BlockSpec((tb, d_ssd), lambda b, c: (cg(b, c), 0)),
            pl.BlockSpec((1, d_ssd), lambda b, c: (0, 0)),
        ]
        args += [y_fwd, proj, norm_w]
        out_dtype = BF16
    else:
        in_specs += [pl.BlockSpec((d_ssd, 128), lambda b, c: (0, 0))]
        args += [dexp]
        out_dtype = F32
    return pl.pallas_call(
        functools.partial(_ssd_kernel, backward=backward, cb=cb, mxu_cols=2 if backward else 0),
        grid=(batch, nc),
        in_specs=in_specs,
        out_specs=pl.BlockSpec((tb, d_ssd), lambda b, c: (cg(b, c), 0)),
        out_shape=jax.ShapeDtypeStruct((m, d_ssd), out_dtype),
        scratch_shapes=[pltpu.VMEM((d_ssd, SSD_STATE), F32)],
        compiler_params=_cparams(("parallel", "arbitrary")),
        name="ssd_bwd" if backward else "ssd_fwd",
    )(*args)


def _mlstm_kernel(*refs, backward, dk, dv, heads_per_group, cb):
    if backward:
        (q_ref, k_ref, v_ref, nat_ref, tr_ref, hf_ref, og_ref, nw_ref, o_ref, c_ref, m_ref) = refs
    else:
        (q_ref, k_ref, v_ref, nat_ref, tr_ref, o_ref, c_ref, m_ref) = refs
    H = MLSTM_HEADS

    @pl.when(pl.program_id(1) == 0)
    def _():
        c_ref[...] = jnp.zeros_like(c_ref)
        m_ref[...] = jnp.zeros_like(m_ref)

    t_i = lax.broadcasted_iota(jnp.int32, (CHUNK, CHUNK), 0)
    s_i = lax.broadcasted_iota(jnp.int32, (CHUNK, CHUNK), 1)
    mask = (s_i >= t_i) if backward else (s_i <= t_i)
    ones_blk = jnp.ones((CHUNK, 128), BF16)
    scale = dk ** -0.5
    dense = (CHUNK, CHUNK)

    chunk_order = range(cb - 1, -1, -1) if backward else range(cb)
    for j, g0 in [(j, g0) for j in chunk_order for g0 in range(0, H, heads_per_group)]:
        ts = slice(j * CHUNK, (j + 1) * CHUNK)
        hs = list(range(g0, g0 + heads_per_group))
        lane_of = {h: h + (H if backward else 0) for h in hs}
        st = {h: {} for h in hs}
        for h in hs:
            d, hh = st[h], lane_of[h]
            li_row = tr_ref[hh:hh + 1, ts]
            cum_row = tr_ref[160 + hh:161 + hh, ts]
            d["tot"] = tr_ref[288 + hh:289 + hh, ts]
            d["base_row"] = cum_row - li_row
            grow = d["tot"] - d["base_row"]
            d["m_in"] = m_ref[h, 0:1, :]
            m_loc = jnp.broadcast_to(jnp.max(grow, axis=1, keepdims=True), (1, CHUNK))
            d["m_new"] = jnp.maximum(d["tot"] + d["m_in"], m_loc)
            d["w_row"] = jnp.exp(grow - d["m_new"])
            d["cum_d"] = jnp.broadcast_to(nat_ref[ts, 160 + hh:161 + hh], dense)
            d["rmax_d"] = jnp.broadcast_to(nat_ref[ts, 416 + hh:417 + hh], dense)
            d["v_aug"] = jnp.concatenate([v_ref[ts, h * dv:(h + 1) * dv].astype(BF16), ones_blk], axis=1)
        for h in hs:
            d = st[h]
            k = k_ref[ts, h * dk:(h + 1) * dk].astype(F32)
            d["kb"] = k.astype(BF16)
            d["kwT"] = (k.T * d["w_row"]).astype(BF16)
            d["qs"] = (q_ref[ts, h * dk:(h + 1) * dk].astype(F32) * scale).astype(BF16)
        for h in hs:
            d = st[h]
            d["c_loc"] = _dot(d["kwT"], d["v_aug"])
            d["sqk"] = _dot_nt(d["qs"], d["kb"])
            d["c_in"] = c_ref[h]
            d["qc"] = _dot(d["qs"], d["c_in"].astype(BF16))
        for h in hs:
            d = st[h]
            dlog = jnp.where(mask, d["cum_d"] - d["base_row"], -jnp.inf)
            inter = d["cum_d"] + d["m_in"]
            d["m_t"] = jnp.maximum(d["rmax_d"], inter)
            d["pm"] = (jnp.exp(dlog - d["m_t"]) * d["sqk"]).astype(BF16)
            d["a_inter"] = jnp.exp(inter - d["m_t"])
        for h in hs:
            d = st[h]
            a3 = jnp.concatenate([d["a_inter"]] * (dv // 128 + 1), axis=1)
            num = _dot(d["pm"], d["v_aug"]) + d["qc"] * a3
            den = num[:, dv:dv + 128]
            inv = 1.0 / jnp.maximum(jnp.abs(den), jnp.exp(-d["m_t"]))
            d["hout"] = num[:, 0:dv] * jnp.concatenate([inv] * (dv // 128), axis=1)
        for h in hs:
            d = st[h]
            a_prev = jnp.exp(d["tot"] + d["m_in"] - d["m_new"])
            a_prev3 = jnp.concatenate([a_prev] * (dv // 128 + 1), axis=1)
            c_ref[h] = a_prev3 * d["c_in"] + d["c_loc"]
            m_ref[h] = jnp.broadcast_to(d["m_new"], (8, 128))
        for h in hs:
            hout = st[h]["hout"]
            vs = slice(h * dv, (h + 1) * dv)
            if backward:
                hout = hout + hf_ref[ts, vs]
                hout = hout * lax.rsqrt(jnp.mean(hout * hout, axis=-1, keepdims=True) + EPS)
                hout = hout * nw_ref[:, vs]
                o_ref[ts, vs] = (_sigmoid(og_ref[ts, vs].astype(F32)) * hout).astype(o_ref.dtype)
            else:
                o_ref[ts, vs] = hout


def mlstm_scan(proj, ml, mlT, batch, seq_len, cols, *, backward, h_fwd=None, norm_w=None):
    m = proj.shape[0]
    cb = CHUNKS_PER_STEP
    tb = CHUNK * cb
    nc = seq_len // tb
    H = MLSTM_HEADS
    dk, dv = cols["dk"], cols["dv"]
    qw, vw = H * dk, H * dv
    qb, kb, vb, ob = cols["q"] // qw, cols["k"] // qw, cols["v"] // vw, cols["o"] // vw
    assert qb * qw == cols["q"] and kb * qw == cols["k"] and vb * vw == cols["v"] and ob * vw == cols["o"]

    def cg(b, c):
        return b * nc + ((nc - 1 - c) if backward else c)

    in_specs = [
        pl.BlockSpec((tb, qw), lambda b, c: (cg(b, c), qb)),
        pl.BlockSpec((tb, qw), lambda b, c: (cg(b, c), kb)),
        pl.BlockSpec((tb, vw), lambda b, c: (cg(b, c), vb)),
        pl.BlockSpec((tb, 512), lambda b, c: (cg(b, c), 0)),
        pl.BlockSpec((384, tb), lambda b, c: (0, cg(b, c))),
    ]
    args = [proj, proj, proj, ml, mlT]
    if backward:
        in_specs += [
            pl.BlockSpec((tb, vw), lambda b, c: (cg(b, c), 0)),
            pl.BlockSpec((tb, vw), lambda b, c: (cg(b, c), ob)),
            pl.BlockSpec((1, vw), lambda b, c: (0, 0)),
        ]
        args += [h_fwd, proj, norm_w]
        out_dtype = BF16
    else:
        out_dtype = F32
    return pl.pallas_call(
        functools.partial(_mlstm_kernel, backward=backward, dk=dk, dv=dv,
                          heads_per_group=8 if backward else 4, cb=cb),
        grid=(batch, nc),
        in_specs=in_specs,
        out_specs=pl.BlockSpec((tb, vw), lambda b, c: (cg(b, c), 0)),
        out_shape=jax.ShapeDtypeStruct((m, vw), out_dtype),
        scratch_shapes=[pltpu.VMEM((H, dk, dv + 128), F32), pltpu.VMEM((H, 8, 128), F32)],
        compiler_params=_cparams(("parallel", "arbitrary")),
        name="mlstm_bwd" if backward else "mlstm_fwd",
    )(*args)


def _outproj_kernel(a1_ref, a2_ref, w1_ref, w2_ref, x_ref, nw_ref, o_ref, xw_ref, ssq_ref):
    acc = _dot(a1_ref[...], w1_ref[...]) + _dot(a2_ref[...], w2_ref[...])
    x1 = x_ref[...] + acc
    o_ref[...] = x1
    xw_ref[...] = (x1 * nw_ref[...]).astype(xw_ref.dtype)
    ssq_ref[...] = jnp.broadcast_to(jnp.sum(x1 * x1, axis=-1, keepdims=True), ssq_ref.shape)


def outproj_residual(a1, a2, w1, w2, x, norm_w, bm=512, bn=512):
    m, k = a1.shape
    n = w1.shape[1]
    return pl.pallas_call(
        _outproj_kernel,
        grid=(n // bn, m // bm),
        in_specs=[pl.BlockSpec((bm, k), lambda j, i: (i, 0)),
                  pl.BlockSpec((bm, k), lambda j, i: (i, 0)),
                  pl.BlockSpec((k, bn), lambda j, i: (0, j)),
                  pl.BlockSpec((k, bn), lambda j, i: (0, j)),
                  pl.BlockSpec((bm, bn), lambda j, i: (i, j)),
                  pl.BlockSpec((1, bn), lambda j, i: (0, j))],
        out_specs=[pl.BlockSpec((bm, bn), lambda j, i: (i, j)),
                   pl.BlockSpec((bm, bn), lambda j, i: (i, j)),
                   pl.BlockSpec((bm, 128), lambda j, i: (i, j))],
        out_shape=[jax.ShapeDtypeStruct((m, n), F32), jax.ShapeDtypeStruct((m, n), BF16),
                   jax.ShapeDtypeStruct((m, (n // bn) * 128), F32)],
        compiler_params=_cparams(("parallel", "parallel")),
        name="outproj_residual",
    )(a1, a2, w1, w2, x, norm_w.reshape(1, n).astype(F32))


def _prep_layer(norm1_w, w_in, conv_w, conv_b, dt_bias, a_log, d_skip, ssd_norm_w, b_i, b_f,
                mlstm_norm_w, w_out, norm2_w, w_up, w_down):
    d_model = w_in.shape[0]
    d_mix = w_out.shape[0]
    d_ssd = d_mix // 2
    d_ml = d_mix - d_ssd
    n_ssd_heads = d_ssd // SSD_HEAD_DIM
    xbc_w = d_ssd + 2 * SSD_GROUPS * SSD_STATE
    dv = d_ml // MLSTM_HEADS
    dk = dv // 2
    widths = (d_ssd, xbc_w, 2 * n_ssd_heads, MLSTM_HEADS * dk, MLSTM_HEADS * dk, d_ml, d_ml,
              2 * MLSTM_HEADS, 2 * MLSTM_HEADS)
    offs = [0]
    for wd in widths:
        offs.append(offs[-1] + wd)
    assert offs[-1] == w_in.shape[1]
    assert 2 * n_ssd_heads == 128 and 2 * MLSTM_HEADS == 32
    seg = lambda i: w_in[:, offs[i]:offs[i + 1]]
    w_bf = w_in.astype(BF16)
    w_gate = jnp.concatenate([seg(2), seg(7), seg(8), jnp.zeros((d_model, 64), w_in.dtype)],
                             axis=1).astype(BF16)
    gate_bias = jnp.concatenate([dt_bias.reshape(-1), b_i.reshape(-1), b_f.reshape(-1),
                                 jnp.zeros((64,), F32)]).astype(F32).reshape(1, 256)
    cols = {"q": 0, "k": MLSTM_HEADS * dk, "v": 2 * MLSTM_HEADS * dk}
    cols["o"] = cols["v"] + d_ml
    cols["dk"], cols["dv"], cols["d_ssd"] = dk, dv, d_ssd
    return dict(
        norm1_w=norm1_w, w_bf=w_bf, offs=offs, w_gate=w_gate, gate_bias=gate_bias,
        alog=a_log.reshape(1, 128).astype(F32),
        conv_w=conv_w.astype(F32), conv_b=conv_b.reshape(1, -1).astype(F32),
        dexp=jnp.broadcast_to(jnp.repeat(d_skip.astype(F32), SSD_HEAD_DIM)[:, None], (d_ssd, 128)),
        ssd_norm_w=ssd_norm_w.reshape(1, -1).astype(F32),
        mlstm_norm_w=mlstm_norm_w.reshape(1, -1).astype(F32),
        w_out1=w_out[:d_ssd].astype(BF16), w_out2=w_out[d_ssd:].astype(BF16),
        norm2_w=norm2_w, w_up=w_up.astype(BF16), w_down=w_down.astype(BF16), cols=cols)


def _layer(x, p, batch, seq_len):
    cols = p["cols"]
    d_ssd = cols["d_ssd"]
    h, ssd, ssdT, ml, mlT = gates(x, p["norm1_w"], p["w_gate"], p["gate_bias"], p["alog"])
    offs = p["offs"]
    proj_z = matmul(h, p["w_bf"], F32, col0=offs[0], n=d_ssd)
    proj_b = matmul(h, p["w_bf"], BF16, col0=offs[3], n=offs[7] - offs[3])
    xsT = matmul_conv(h, p["w_bf"], offs[1], d_ssd, p["conv_w"][:, :d_ssd], p["conv_b"][:, :d_ssd],
                      seq_len, True)
    bc = matmul_conv(h, p["w_bf"], offs[1] + d_ssd, 2 * SSD_GROUPS * SSD_STATE,
                     p["conv_w"][:, d_ssd:], p["conv_b"][:, d_ssd:], seq_len, False)
    y_f = ssd_scan(xsT, bc, ssd, ssdT, batch, seq_len, backward=False, dexp=p["dexp"])
    mix1 = ssd_scan(xsT, bc, ssd, ssdT, batch, seq_len, backward=True, y_fwd=y_f,
                    proj=proj_z, norm_w=p["ssd_norm_w"])
    h_f = mlstm_scan(proj_b, ml, mlT, batch, seq_len, cols, backward=False)
    mix2 = mlstm_scan(proj_b, ml, mlT, batch, seq_len, cols, backward=True, h_fwd=h_f,
                      norm_w=p["mlstm_norm_w"])
    x1, x1w, ssq = outproj_residual(mix1, mix2, p["w_out1"], p["w_out2"], x, p["norm2_w"])
    u = matmul(x1w, p["w_up"], BF16, relu2=True, row_ssq=ssq)
    return matmul_ksplit_residual(u, p["w_down"], x1)


def _trunk(x, layers, final_norm_w):
    batch, seq_len, d = x.shape
    xf = x.reshape(batch * seq_len, d)
    for p in layers:
        xf = _layer(xf, p, batch, seq_len)
    return rmsnorm_rows(xf, final_norm_w, F32).reshape(batch, seq_len, d)


def kernel(x_prompt, x_sample, norm1_w, w_in, conv_w, conv_b, dt_bias, a_log, d_skip, ssd_norm_w,
           b_i, b_f, mlstm_norm_w, w_out, norm2_w, w_up, w_down, final_norm_w):
    depth = w_in.shape[0]
    layers = [_prep_layer(norm1_w[l], w_in[l], conv_w[l], conv_b[l], dt_bias[l], a_log[l], d_skip[l],
                          ssd_norm_w[l], b_i[l], b_f[l], mlstm_norm_w[l], w_out[l], norm2_w[l],
                          w_up[l], w_down[l]) for l in range(depth)]
    y_prompt = _trunk(x_prompt, layers, final_norm_w)
    y_sample = _trunk(x_sample, layers, final_norm_w)
    return (y_prompt, y_sample)
```

```python
import functools

import jax
import jax.numpy as jnp
from jax import lax
from jax.experimental import pallas as pl
from jax.experimental.pallas import tpu as pltpu

F32 = jnp.float32
BF16 = jnp.bfloat16

CHUNK = 128
CHUNKS_PER_STEP = 2
EPS = 1e-5
D_CONV = 5
SSD_GROUPS = 8
SSD_HEAD_DIM = 64
SSD_STATE = 128
MLSTM_HEADS = 16
VMEM_LIMIT = 56 * 1024 * 1024
KSPLIT_VMEM_LIMIT = 60 * 1024 * 1024


def _cparams(sem, vmem_limit=VMEM_LIMIT):
    return pltpu.CompilerParams(dimension_semantics=sem, vmem_limit_bytes=vmem_limit)


def _sigmoid(x):
    return 1.0 / (1.0 + jnp.exp(-x))


def _softplus(x):
    return jnp.maximum(x, 0.0) + jnp.log1p(jnp.exp(-jnp.abs(x)))


def _dot(a, b):
    return jnp.dot(a, b, preferred_element_type=F32)


def _dot_nt(a, b):
    return lax.dot_general(a, b, (((1,), (1,)), ((), ())), preferred_element_type=F32)


def _rmsnorm_kernel(x_ref, w_ref, o_ref):
    x = x_ref[...].astype(F32)
    y = x * lax.rsqrt(jnp.mean(x * x, axis=-1, keepdims=True) + EPS)
    o_ref[...] = (y * w_ref[...]).astype(o_ref.dtype)


def rmsnorm_rows(x, w, out_dtype, bm=512):
    m, d = x.shape
    return pl.pallas_call(
        _rmsnorm_kernel,
        grid=(m // bm,),
        in_specs=[pl.BlockSpec((bm, d), lambda i: (i, 0)),
                  pl.BlockSpec((1, d), lambda i: (0, 0))],
        out_specs=pl.BlockSpec((bm, d), lambda i: (i, 0)),
        out_shape=jax.ShapeDtypeStruct((m, d), out_dtype),
        compiler_params=_cparams(("parallel",)),
        name="rmsnorm_rows",
    )(x, w.reshape(1, d).astype(F32))


def _matmul_kernel(*refs, relu2, row_ssq_dim):
    if row_ssq_dim:
        a_ref, b_ref, ssq_ref, o_ref = refs
    else:
        a_ref, b_ref, o_ref = refs
    acc = _dot(a_ref[...], b_ref[...])
    if relu2:
        acc = jnp.maximum(acc, 0.0)
        acc = acc * acc
    if row_ssq_dim:
        ssq = ssq_ref[...]
        tot = ssq[:, 0:128]
        for part in range(1, ssq.shape[1] // 128):
            tot = tot + ssq[:, part * 128:(part + 1) * 128]
        r2 = 1.0 / (tot * (1.0 / row_ssq_dim) + EPS)
        acc = acc * jnp.concatenate([r2] * (acc.shape[1] // 128), axis=1)
    o_ref[...] = acc.astype(o_ref.dtype)


def matmul(a, b, out_dtype, bm=1024, bn=1024, relu2=False, col0=0, n=None, row_ssq=None):
    m, k = a.shape
    n = b.shape[1] if n is None else n
    in_specs = [pl.BlockSpec((bm, k), lambda j, i: (i, 0)),
                pl.BlockSpec((pl.Element(k), pl.Element(bn)),
                             lambda j, i: (0, pl.multiple_of(col0 + j * bn, 128)))]
    args = [a, b]
    if row_ssq is not None:
        assert relu2
        in_specs.append(pl.BlockSpec((bm, row_ssq.shape[1]), lambda j, i: (i, 0)))
        args.append(row_ssq)
    return pl.pallas_call(
        functools.partial(_matmul_kernel, relu2=relu2, row_ssq_dim=k if row_ssq is not None else 0),
        grid=(n // bn, m // bm),
        in_specs=in_specs,
        out_specs=pl.BlockSpec((bm, bn), lambda j, i: (i, j)),
        out_shape=jax.ShapeDtypeStruct((m, n), out_dtype),
        compiler_params=_cparams(("parallel", "parallel")),
        name="matmul_relu2" if relu2 else "matmul",
    )(*args)


def _matmul_ksplit_res_kernel(a_ref, b_ref, x_ref, o_ref):
    @pl.when(pl.program_id(2) == 0)
    def _():
        o_ref[...] = x_ref[...] + _dot(a_ref[...], b_ref[...])

    @pl.when(pl.program_id(2) != 0)
    def _():
        o_ref[...] = o_ref[...] + _dot(a_ref[...], b_ref[...])


def matmul_ksplit_residual(a, b, x, bm=1024, bn=1024, bk=4096):
    m, k = a.shape
    n = b.shape[1]
    return pl.pallas_call(
        _matmul_ksplit_res_kernel,
        grid=(n // bn, m // bm, k // bk),
        in_specs=[pl.BlockSpec((bm, bk), lambda j, i, kk: (i, kk)),
                  pl.BlockSpec((bk, bn), lambda j, i, kk: (kk, j)),
                  pl.BlockSpec((bm, bn), lambda j, i, kk: (i, j))],
        out_specs=pl.BlockSpec((bm, bn), lambda j, i, kk: (i, j)),
        out_shape=jax.ShapeDtypeStruct((m, n), F32),
        compiler_params=_cparams(("parallel", "parallel", "arbitrary"), KSPLIT_VMEM_LIMIT),
        name="matmul_ksplit_residual",
    )(a, b, x)


def _split3(x):
    hi = x.astype(BF16)
    r1 = x - hi.astype(F32)
    mid = r1.astype(BF16)
    r2 = r1 - mid.astype(F32)
    return hi, mid, r2.astype(BF16)


def _gates_kernel(x_ref, nw_ref, w_ref, bias_ref, alog_ref,
                  h_ref, ssd_ref, ssdT_ref, ml_ref, mlT_ref, *, bm):
    x = x_ref[...]
    h = (x * lax.rsqrt(jnp.mean(x * x, axis=-1, keepdims=True) + EPS) * nw_ref[...]).astype(BF16)
    h_ref[...] = h
    raw = _dot(h, w_ref[...]) + bias_ref[...]
    dt = _softplus(raw[:, 0:128])
    a = dt * (-jnp.exp(alog_ref[...]))
    t2 = raw[:, 128:256]
    lane = lax.broadcasted_iota(jnp.int32, (CHUNK, 128), 1)
    row_t = lax.broadcasted_iota(jnp.int32, (CHUNK, 128), 0)
    lsig = -_softplus(-t2)
    row_i = lax.broadcasted_iota(jnp.int32, (CHUNK, CHUNK), 0)
    col_i = lax.broadcasted_iota(jnp.int32, (CHUNK, CHUNK), 1)
    lower = (col_i <= row_i).astype(BF16)
    upper = (col_i >= row_i).astype(BF16)
    ones = jnp.ones((CHUNK, CHUNK), BF16)
    ssd_fwd_lane = lane < 64
    ml_fwd_lane = (lane % 32) < 16

    def cums(x, fwd_lane):
        hi, mid, lo = _split3(x)
        cum_f = _dot(lower, hi) + _dot(lower, mid) + _dot(lower, lo)
        cum_b = _dot(upper, hi) + _dot(upper, mid) + _dot(upper, lo)
        tot = _dot(ones, hi) + _dot(ones, mid) + _dot(ones, lo)
        return jnp.where(fwd_lane, cum_f, cum_b), tot

    for c in range(bm // CHUNK):
        sl = slice(c * CHUNK, (c + 1) * CHUNK)
        dt_c = dt[sl]
        acum, tot = cums(a[sl], ssd_fwd_lane)
        ssd_ref[sl, 0:128] = dt_c
        ssd_ref[sl, 128:256] = acum
        ssd_ref[sl, 256:384] = tot
        ssdT_ref[0:128, sl] = dt_c.T
        ssdT_ref[128:256, sl] = acum.T
        ssdT_ref[256:384, sl] = tot.T
        t2_c = t2[sl]
        mcum, mtot = cums(lsig[sl], ml_fwd_lane)
        y = mcum - pltpu.roll(t2_c, 32, 1)
        y_f, y_b = y, y
        for d in (1, 2, 4, 8, 16, 32, 64):
            y_f = jnp.minimum(y_f, jnp.where(row_t >= d, pltpu.roll(y_f, d, 0), jnp.inf))
            y_b = jnp.minimum(y_b, jnp.where(row_t < CHUNK - d, pltpu.roll(y_b, CHUNK - d, 0), jnp.inf))
        rmax = mcum - jnp.where(ml_fwd_lane, y_f, y_b)
        ml_ref[sl, 0:128] = t2_c
        ml_ref[sl, 128:256] = mcum
        ml_ref[sl, 256:384] = mtot
        ml_ref[sl, 384:512] = rmax
        mlT_ref[0:128, sl] = t2_c.T
        mlT_ref[128:256, sl] = mcum.T
        mlT_ref[256:384, sl] = mtot.T


def gates(x, norm_w, w_gate, bias, alog, bm=512):
    m, d = x.shape
    nat = pl.BlockSpec((bm, 384), lambda i: (i, 0))
    nat4 = pl.BlockSpec((bm, 512), lambda i: (i, 0))
    tr = pl.BlockSpec((384, bm), lambda i: (0, i))
    return pl.pallas_call(
        functools.partial(_gates_kernel, bm=bm),
        grid=(m // bm,),
        in_specs=[pl.BlockSpec((bm, d), lambda i: (i, 0)),
                  pl.BlockSpec((1, d), lambda i: (0, 0)),
                  pl.BlockSpec((d, 256), lambda i: (0, 0)),
                  pl.BlockSpec((1, 256), lambda i: (0, 0)),
                  pl.BlockSpec((1, 128), lambda i: (0, 0))],
        out_specs=[pl.BlockSpec((bm, d), lambda i: (i, 0)), nat, tr, nat4, tr],
        out_shape=[jax.ShapeDtypeStruct((m, d), BF16),
                   jax.ShapeDtypeStruct((m, 384), F32), jax.ShapeDtypeStruct((384, m), F32),
                   jax.ShapeDtypeStruct((m, 512), F32), jax.ShapeDtypeStruct((384, m), F32)],
        compiler_params=_cparams(("parallel",)),
        name="gates",
    )(x, norm_w.reshape(1, d).astype(F32), w_gate, bias, alog)


def _matmul_conv_kernel(h_ref, hp_ref, hn_ref, w_ref, cw_ref, cb_ref, o_ref, *, bm, seq_len,
                        transpose_out, sub):
    i = pl.program_id(1)
    w = w_ref[...]
    at_start = (i * bm) % seq_len == 0
    at_end = ((i + 1) * bm) % seq_len == 0
    cw = cw_ref[...]
    cb = cb_ref[...]
    nsub = bm // sub
    row = lax.broadcasted_iota(jnp.int32, (sub, w.shape[1]), 0)

    def conv(cur, prev8, next8):
        m2 = pltpu.roll(cur, 2, 0)
        m2 = jnp.where(row == 0, prev8[6:7], jnp.where(row == 1, prev8[7:8], m2))
        m1 = pltpu.roll(cur, 1, 0)
        m1 = jnp.where(row == 0, prev8[7:8], m1)
        p1 = pltpu.roll(cur, sub - 1, 0)
        p1 = jnp.where(row == sub - 1, next8[0:1], p1)
        p2 = pltpu.roll(cur, sub - 2, 0)
        p2 = jnp.where(row == sub - 2, next8[0:1], jnp.where(row == sub - 1, next8[1:2], p2))
        out = cw[0:1] * m2
        out = out + cw[1:2] * m1
        out = out + cw[2:3] * cur
        out = out + cw[3:4] * p1
        out = out + cw[4:5] * p2
        out = out + cb
        return out * _sigmoid(out)

    def emit(r, res):
        if transpose_out:
            o_ref[:, r * sub:(r + 1) * sub] = res.T.astype(o_ref.dtype)
        else:
            o_ref[r * sub:(r + 1) * sub, :] = res.astype(o_ref.dtype)

    prev8 = jnp.where(at_start, 0.0, _dot(hp_ref[...], w))[8:16]
    blocks = [_dot(h_ref[0:sub, :], w)]
    for r in range(nsub):
        if r + 1 < nsub:
            blocks.append(_dot(h_ref[(r + 1) * sub:(r + 2) * sub, :], w))
            next8 = blocks[r + 1][0:8]
        else:
            next8 = jnp.where(at_end, 0.0, _dot(hn_ref[...], w))[0:8]
        emit(r, conv(blocks[r], prev8, next8))
        prev8 = blocks[r][sub - 8:sub]


def matmul_conv(h, w, col0, n, conv_w, conv_b, seq_len, transpose_out, bm=1024, bn=1024, sub=256):
    m, k = h.shape
    assert seq_len % bm == 0 and bm % sub == 0
    nhb = m // 16
    if transpose_out:
        out_spec = pl.BlockSpec((bn, bm), lambda j, i: (j, i))
        out_shape = jax.ShapeDtypeStruct((n, m), BF16)
    else:
        out_spec = pl.BlockSpec((bm, bn), lambda j, i: (i, j))
        out_shape = jax.ShapeDtypeStruct((m, n), BF16)
    return pl.pallas_call(
        functools.partial(_matmul_conv_kernel, bm=bm, seq_len=seq_len, transpose_out=transpose_out,
                          sub=sub),
        grid=(n // bn, m // bm),
        in_specs=[pl.BlockSpec((bm, k), lambda j, i: (i, 0)),
                  pl.BlockSpec((16, k), lambda j, i: (jnp.maximum(i * (bm // 16) - 1, 0), 0)),
                  pl.BlockSpec((16, k), lambda j, i: (jnp.minimum((i + 1) * (bm // 16), nhb - 1), 0)),
                  pl.BlockSpec((pl.Element(k), pl.Element(bn)),
                               lambda j, i: (0, pl.multiple_of(col0 + j * bn, 128))),
                  pl.BlockSpec((D_CONV, bn), lambda j, i: (0, j)),
                  pl.BlockSpec((1, bn), lambda j, i: (0, j))],
        out_specs=out_spec,
        out_shape=out_shape,
        compiler_params=_cparams(("parallel", "parallel")),
        name="matmul_conv_t" if transpose_out else "matmul_conv",
    )(h, h, h, w, conv_w, conv_b)


def _ssd_kernel(*refs, backward, cb, mxu_cols):
    if backward:
        (xsT_ref, b_ref, c_ref, nat_ref, tr_ref, yf_ref, z_ref, nw_ref, o_ref, s_ref) = refs
    else:
        (xsT_ref, b_ref, c_ref, nat_ref, tr_ref, dexp_ref, o_ref, s_ref) = refs
    G, R, P, N = SSD_GROUPS, 8, SSD_HEAD_DIM, SSD_STATE
    gw = R * P
    h0 = G * R if backward else 0

    @pl.when(pl.program_id(1) == 0)
    def _():
        s_ref[...] = jnp.zeros_like(s_ref)

    s_i = lax.broadcasted_iota(jnp.int32, (CHUNK, CHUNK), 0)
    l_i = lax.broadcasted_iota(jnp.int32, (CHUNK, CHUNK), 1)
    mask = (l_i <= s_i) if backward else (l_i >= s_i)
    if mxu_cols:
        k_i = lax.broadcasted_iota(jnp.int32, (128, 2 * CHUNK), 0)
        c_i = lax.broadcasted_iota(jnp.int32, (128, 2 * CHUNK), 1)
    acum_split = {}
    chunk_order = range(cb - 1, -1, -1) if backward else range(cb)
    for j, g in [(j, g) for j in chunk_order for g in range(G)]:
        ts = slice(j * CHUNK, (j + 1) * CHUNK)
        gs = slice(g * gw, (g + 1) * gw)
        hg = h0 + g * R
        col_pairs = []
        if mxu_cols:
            if j not in acum_split:
                acum_split[j] = _split3(nat_ref[ts, 128:256])
            hi, mid, lo = acum_split[j]
            for pr in range(mxu_cols // 2):
                sel = (k_i == hg + 2 * pr + c_i // CHUNK).astype(BF16)
                col_pairs.append(_dot(hi, sel) + _dot(mid, sel) + _dot(lo, sel))
        bm = b_ref[ts, g * N:(g + 1) * N]
        cm = c_ref[ts, g * N:(g + 1) * N]
        dt = tr_ref[hg:hg + R, ts]
        acum = tr_ref[128 + hg:128 + hg + R, ts]
        tot = tr_ref[256 + hg:256 + hg + R, ts]
        cbT = _dot_nt(bm, cm)
        s_old = s_ref[gs, :]
        yoffT = _dot_nt(s_old.astype(BF16), cm)
        e_acum = jnp.exp(acum)
        dte = jnp.exp(tot - acum)
        e_tot = jnp.exp(tot)
        y_pieces = []
        xd_pieces = []
        for r in range(R):
            hs = slice(g * gw + r * P, g * gw + (r + 1) * P)
            xr = xsT_ref[hs, ts].astype(F32)
            xdt = xr * dt[r:r + 1, :]
            if r < mxu_cols:
                col = col_pairs[r // 2][:, (r % 2) * CHUNK:(r % 2 + 1) * CHUNK]
            else:
                col = nat_ref[ts, 128 + hg + r:128 + hg + r + 1]
            seg = acum[r:r + 1, :] - col
            dec = jnp.exp(jnp.where(mask, seg, -jnp.inf))
            mt = (cbT * dec).astype(BF16)
            y_r = _dot(xdt.astype(BF16), mt) + yoffT[r * P:(r + 1) * P, :] * e_acum[r:r + 1, :]
            if not backward:
                y_r = y_r + dexp_ref[hs, :] * xr
            y_pieces.append(y_r)
            xd_pieces.append((xdt * dte[r:r + 1, :]).astype(BF16))
        yT = jnp.concatenate(y_pieces, axis=0)
        upd = _dot(jnp.concatenate(xd_pieces, axis=0), bm)
        for r in range(R):
            hs = slice(r * P, (r + 1) * P)
            s_ref[g * gw + r * P:g * gw + (r + 1) * P, :] = s_old[hs, :] * e_tot[r:r + 1, :] + upd[hs, :]
        y = yT.T
        if backward:
            y = y + yf_ref[ts, gs]
            z = z_ref[ts, gs].astype(F32)
            y = y * (z * _sigmoid(z))
            y = y * lax.rsqrt(jnp.mean(y * y, axis=-1, keepdims=True) + EPS)
            o_ref[ts, gs] = (y * nw_ref[:, gs]).astype(o_ref.dtype)
        else:
            o_ref[ts, gs] = y


def ssd_scan(xsT, bc, ssd, ssdT, batch, seq_len, *, backward, dexp=None,
             y_fwd=None, proj=None, norm_w=None):
    m = xsT.shape[1]
    cb = CHUNKS_PER_STEP
    tb = CHUNK * cb
    assert seq_len % tb == 0 and m == batch * seq_len
    nc = seq_len // tb
    G = SSD_GROUPS
    d_ssd = xsT.shape[0]

    def cg(b, c):
        return b * nc + ((nc - 1 - c) if backward else c)

    in_specs = [
        pl.BlockSpec((d_ssd, tb), lambda b, c: (0, cg(b, c))),
        pl.BlockSpec((tb, G * SSD_STATE), lambda b, c: (cg(b, c), 0)),
        pl.BlockSpec((tb, G * SSD_STATE), lambda b, c: (cg(b, c), 1)),
        pl.BlockSpec((tb, 384), lambda b, c: (cg(b, c), 0)),
        pl.BlockSpec((384, tb), lambda b, c: (0, cg(b, c))),
    ]
    args = [xsT, bc, bc, ssd, ssdT]
    if backward:
        in_specs += [
            pl.BlockSpec((tb, d_ssd), lambda b, c: (cg(b, c), 0)),
            pl.BlockSpec((tb, d_ssd), lambda b, c: (cg(b, c), 0)),
            pl.BlockSpec((1, d_ssd), lambda b, c: (0, 0)),
        ]
        args += [y_fwd, proj, norm_w]
        out_dtype = BF16
    else:
        in_specs += [pl.BlockSpec((d_ssd, 128), lambda b, c: (0, 0))]
        args += [dexp]
        out_dtype = F32
    return pl.pallas_call(
        functools.partial(_ssd_kernel, backward=backward, cb=cb, mxu_cols=2 if backward else 0),
        grid=(batch, nc),
        in_specs=in_specs,
        out_specs=pl.BlockSpec((tb, d_ssd), lambda b, c: (cg(b, c), 0)),
        out_shape=jax.ShapeDtypeStruct((m, d_ssd), out_dtype),
        scratch_shapes=[pltpu.VMEM((d_ssd, SSD_STATE), F32)],
        compiler_params=_cparams(("parallel", "arbitrary")),
        name="ssd_bwd" if backward else "ssd_fwd",
    )(*args)


def _mlstm_kernel(*refs, backward, dk, dv, heads_per_group, cb):
    if backward:
        (q_ref, k_ref, v_ref, nat_ref, tr_ref, hf_ref, og_ref, nw_ref, o_ref, c_ref, m_ref) = refs
    else:
        (q_ref, k_ref, v_ref, nat_ref, tr_ref, o_ref, c_ref, m_ref) = refs
    H = MLSTM_HEADS

    @pl.when(pl.program_id(1) == 0)
    def _():
        c_ref[...] = jnp.zeros_like(c_ref)
        m_ref[...] = jnp.zeros_like(m_ref)

    t_i = lax.broadcasted_iota(jnp.int32, (CHUNK, CHUNK), 0)
    s_i = lax.broadcasted_iota(jnp.int32, (CHUNK, CHUNK), 1)
    mask = (s_i >= t_i) if backward else (s_i <= t_i)
    ones_blk = jnp.ones((CHUNK, 128), BF16)
    scale = dk ** -0.5
    dense = (CHUNK, CHUNK)

    chunk_order = range(cb - 1, -1, -1) if backward else range(cb)
    for j, g0 in [(j, g0) for j in chunk_order for g0 in range(0, H, heads_per_group)]:
        ts = slice(j * CHUNK, (j + 1) * CHUNK)
        hs = list(range(g0, g0 + heads_per_group))
        lane_of = {h: h + (H if backward else 0) for h in hs}
        st = {h: {} for h in hs}
        for h in hs:
            d, hh = st[h], lane_of[h]
            li_row = tr_ref[hh:hh + 1, ts]
            cum_row = tr_ref[160 + hh:161 + hh, ts]
            d["tot"] = tr_ref[288 + hh:289 + hh, ts]
            d["base_row"] = cum_row - li_row
            grow = d["tot"] - d["base_row"]
            d["m_in"] = m_ref[h, 0:1, :]
            m_loc = jnp.broadcast_to(jnp.max(grow, axis=1, keepdims=True), (1, CHUNK))
            d["m_new"] = jnp.maximum(d["tot"] + d["m_in"], m_loc)
            d["w_row"] = jnp.exp(grow - d["m_new"])
            d["cum_d"] = jnp.broadcast_to(nat_ref[ts, 160 + hh:161 + hh], dense)
            d["rmax_d"] = jnp.broadcast_to(nat_ref[ts, 416 + hh:417 + hh], dense)
            d["v_aug"] = jnp.concatenate([v_ref[ts, h * dv:(h + 1) * dv].astype(BF16), ones_blk], axis=1)
        for h in hs:
            d = st[h]
            k = k_ref[ts, h * dk:(h + 1) * dk].astype(F32)
            d["kb"] = k.astype(BF16)
            d["kwT"] = (k.T * d["w_row"]).astype(BF16)
            d["qs"] = (q_ref[ts, h * dk:(h + 1) * dk].astype(F32) * scale).astype(BF16)
        for h in hs:
            d = st[h]
            d["c_loc"] = _dot(d["kwT"], d["v_aug"])
            d["sqk"] = _dot_nt(d["qs"], d["kb"])
            d["c_in"] = c_ref[h]
            d["qc"] = _dot(d["qs"], d["c_in"].astype(BF16))
        for h in hs:
            d = st[h]
            dlog = jnp.where(mask, d["cum_d"] - d["base_row"], -jnp.inf)
            inter = d["cum_d"] + d["m_in"]
            d["m_t"] = jnp.maximum(d["rmax_d"], inter)
            d["pm"] = (jnp.exp(dlog - d["m_t"]) * d["sqk"]).astype(BF16)
            d["a_inter"] = jnp.exp(inter - d["m_t"])
        for h in hs:
            d = st[h]
            a3 = jnp.concatenate([d["a_inter"]] * (dv // 128 + 1), axis=1)
            num = _dot(d["pm"], d["v_aug"]) + d["qc"] * a3
            den = num[:, dv:dv + 128]
            inv = 1.0 / jnp.maximum(jnp.abs(den), jnp.exp(-d["m_t"]))
            d["hout"] = num[:, 0:dv] * jnp.concatenate([inv] * (dv // 128), axis=1)
        for h in hs:
            d = st[h]
            a_prev = jnp.exp(d["tot"] + d["m_in"] - d["m_new"])
            a_prev3 = jnp.concatenate([a_prev] * (dv // 128 + 1), axis=1)
            c_ref[h] = a_prev3 * d["c_in"] + d["c_loc"]
            m_ref[h] = jnp.broadcast_to(d["m_new"], (8, 128))
        for h in hs:
            hout = st[h]["hout"]
            vs = slice(h * dv, (h + 1) * dv)
            if backward:
                hout = hout + hf_ref[ts, vs]
                hout = hout * lax.rsqrt(jnp.mean(hout * hout, axis=-1, keepdims=True) + EPS)
                hout = hout * nw_ref[:, vs]
                o_ref[ts, vs] = (_sigmoid(og_ref[ts, vs].astype(F32)) * hout).astype(o_ref.dtype)
            else:
                o_ref[ts, vs] = hout


def mlstm_scan(proj, ml, mlT, batch, seq_len, cols, *, backward, h_fwd=None, norm_w=None):
    m = proj.shape[0]
    cb = CHUNKS_PER_STEP
    tb = CHUNK * cb
    assert seq_len % tb == 0 and m == batch * seq_len
    nc = seq_len // tb
    H = MLSTM_HEADS
    dk, dv = cols["dk"], cols["dv"]
    qw, vw = H * dk, H * dv
    qb, kb, vb, ob = cols["q"] // qw, cols["k"] // qw, cols["v"] // vw, cols["o"] // vw
    assert qb * qw == cols["q"] and kb * qw == cols["k"] and vb * vw == cols["v"] and ob * vw == cols["o"]

    def cg(b, c):
        return b * nc + ((nc - 1 - c) if backward else c)

    in_specs = [
        pl.BlockSpec((tb, qw), lambda b, c: (cg(b, c), qb)),
        pl.BlockSpec((tb, qw), lambda b, c: (cg(b, c), kb)),
        pl.BlockSpec((tb, vw), lambda b, c: (cg(b, c), vb)),
        pl.BlockSpec((tb, 512), lambda b, c: (cg(b, c), 0)),
        pl.BlockSpec((384, tb), lambda b, c: (0, cg(b, c))),
    ]
    args = [proj, proj, proj, ml, mlT]
    if backward:
        in_specs += [
            pl.BlockSpec((tb, vw), lambda b, c: (cg(b, c), 0)),
            pl.BlockSpec((tb, vw), lambda b, c: (cg(b, c), ob)),
            pl.BlockSpec((1, vw), lambda b, c: (0, 0)),
        ]
        args += [h_fwd, proj, norm_w]
        out_dtype = BF16
    else:
        out_dtype = F32
    return pl.pallas_call(
        functools.partial(_mlstm_kernel, backward=backward, dk=dk, dv=dv,
                          heads_per_group=8 if backward else 4, cb=cb),
        grid=(batch, nc),
        in_specs=in_specs,
        out_specs=pl.BlockSpec((tb, vw), lambda b, c: (cg(b, c), 0)),
        out_shape=jax.ShapeDtypeStruct((m, vw), out_dtype),
        scratch_shapes=[pltpu.VMEM((H, dk, dv + 128), F32), pltpu.VMEM((H, 8, 128), F32)],
        compiler_params=_cparams(("parallel", "arbitrary")),
        name="mlstm_bwd" if backward else "mlstm_fwd",
    )(*args)


def _outproj_kernel(a1_ref, a2_ref, w1_ref, w2_ref, x_ref, nw_ref, o_ref, xw_ref, ssq_ref):
    acc = _dot(a1_ref[...], w1_ref[...]) + _dot(a2_ref[...], w2_ref[...])
    x1 = x_ref[...] + acc
    o_ref[...] = x1
    xw_ref[...] = (x1 * nw_ref[...]).astype(xw_ref.dtype)
    ssq_ref[...] = jnp.broadcast_to(jnp.sum(x1 * x1, axis=-1, keepdims=True), ssq_ref.shape)


def outproj_residual(a1, a2, w1, w2, x, norm_w, bm=512, bn=512):
    m, k = a1.shape
    n = w1.shape[1]
    return pl.pallas_call(
        _outproj_kernel,
        grid=(n // bn, m // bm),
        in_specs=[pl.BlockSpec((bm, k), lambda j, i: (i, 0)),
                  pl.BlockSpec((bm, k), lambda j, i: (i, 0)),
                  pl.BlockSpec((k, bn), lambda j, i: (0, j)),
                  pl.BlockSpec((k, bn), lambda j, i: (0, j)),
                  pl.BlockSpec((bm, bn), lambda j, i: (i, j)),
                  pl.BlockSpec((1, bn), lambda j, i: (0, j))],
        out_specs=[pl.BlockSpec((bm, bn), lambda j, i: (i, j)),
                   pl.BlockSpec((bm, bn), lambda j, i: (i, j)),
                   pl.BlockSpec((bm, 128), lambda j, i: (i, j))],
        out_shape=[jax.ShapeDtypeStruct((m, n), F32), jax.ShapeDtypeStruct((m, n), BF16),
                   jax.ShapeDtypeStruct((m, (n // bn) * 128), F32)],
        compiler_params=_cparams(("parallel", "parallel")),
        name="outproj_residual",
    )(a1, a2, w1, w2, x, norm_w.reshape(1, n).astype(F32))


def _prep_layer(norm1_w, w_in, conv_w, conv_b, dt_bias, a_log, d_skip, ssd_norm_w, b_i, b_f,
                mlstm_norm_w, w_out, norm2_w, w_up, w_down):
    d_model = w_in.shape[0]
    d_mix = w_out.shape[0]
    d_ssd = d_mix // 2
    d_ml = d_mix - d_ssd
    n_ssd_heads = d_ssd // SSD_HEAD_DIM
    xbc_w = d_ssd + 2 * SSD_GROUPS * SSD_STATE
    dv = d_ml // MLSTM_HEADS
    dk = dv // 2
    widths = (d_ssd, xbc_w, 2 * n_ssd_heads, MLSTM_HEADS * dk, MLSTM_HEADS * dk, d_ml, d_ml,
              2 * MLSTM_HEADS, 2 * MLSTM_HEADS)
    offs = [0]
    for wd in widths:
        offs.append(offs[-1] + wd)
    assert offs[-1] == w_in.shape[1]
    assert 2 * n_ssd_heads == 128 and 2 * MLSTM_HEADS == 32
    seg = lambda i: w_in[:, offs[i]:offs[i + 1]]
    w_bf = w_in.astype(BF16)
    w_gate = jnp.concatenate([seg(2), seg(7), seg(8), jnp.zeros((d_model, 64), w_in.dtype)],
                             axis=1).astype(BF16)
    gate_bias = jnp.concatenate([dt_bias.reshape(-1), b_i.reshape(-1), b_f.reshape(-1),
                                 jnp.zeros((64,), F32)]).astype(F32).reshape(1, 256)
    cols = {"q": 0, "k": MLSTM_HEADS * dk, "v": 2 * MLSTM_HEADS * dk}
    cols["o"] = cols["v"] + d_ml
    cols["dk"], cols["dv"], cols["d_ssd"] = dk, dv, d_ssd
    return dict(
        norm1_w=norm1_w, w_bf=w_bf, offs=offs, w_gate=w_gate, gate_bias=gate_bias,
        alog=a_log.reshape(1, 128).astype(F32),
        conv_w=conv_w.astype(F32), conv_b=conv_b.reshape(1, -1).astype(F32),
        dexp=jnp.broadcast_to(jnp.repeat(d_skip.astype(F32), SSD_HEAD_DIM)[:, None], (d_ssd, 128)),
        ssd_norm_w=ssd_norm_w.reshape(1, -1).astype(F32),
        mlstm_norm_w=mlstm_norm_w.reshape(1, -1).astype(F32),
        w_out1=w_out[:d_ssd].astype(BF16), w_out2=w_out[d_ssd:].astype(BF16),
        norm2_w=norm2_w, w_up=w_up.astype(BF16), w_down=w_down.astype(BF16), cols=cols)


def _layer(x, p, batch, seq_len):
    cols = p["cols"]
    d_ssd = cols["d_ssd"]
    h, ssd, ssdT, ml, mlT = gates(x, p["norm1_w"], p["w_gate"], p["gate_bias"], p["alog"])
    offs = p["offs"]
    proj_z = matmul(h, p["w_bf"], F32, col0=offs[0], n=d_ssd)
    proj_b = matmul(h, p["w_bf"], BF16, col0=offs[3], n=offs[7] - offs[3])
    xsT = matmul_conv(h, p["w_bf"], offs[1], d_ssd, p["conv_w"][:, :d_ssd], p["conv_b"][:, :d_ssd],
                      seq_len, True)
    bc = matmul_conv(h, p["w_bf"], offs[1] + d_ssd, 2 * SSD_GROUPS * SSD_STATE,
                     p["conv_w"][:, d_ssd:], p["conv_b"][:, d_ssd:], seq_len, False)
    y_f = ssd_scan(xsT, bc, ssd, ssdT, batch, seq_len, backward=False, dexp=p["dexp"])
    mix1 = ssd_scan(xsT, bc, ssd, ssdT, batch, seq_len, backward=True, y_fwd=y_f,
                    proj=proj_z, norm_w=p["ssd_norm_w"])
    h_f = mlstm_scan(proj_b, ml, mlT, batch, seq_len, cols, backward=False)
    mix2 = mlstm_scan(proj_b, ml, mlT, batch, seq_len, cols, backward=True, h_fwd=h_f,
                      norm_w=p["mlstm_norm_w"])
    x1, x1w, ssq = outproj_residual(mix1, mix2, p["w_out1"], p["w_out2"], x, p["norm2_w"])
    u = matmul(x1w, p["w_up"], BF16, relu2=True, row_ssq=ssq)
    return matmul_ksplit_residual(u, p["w_down"], x1)


def _trunk(x, layers, final_norm_w):
    batch, seq_len, d = x.shape
    xf = x.reshape(batch * seq_len, d)
    for p in layers:
        xf = _layer(xf, p, batch, seq_len)
    return rmsnorm_rows(xf, final_norm_w, F32).reshape(batch, seq_len, d)


def kernel(x_prompt, x_sample, norm1_w, w_in, conv_w, conv_b, dt_bias, a_log, d_skip, ssd_norm_w,
           b_i, b_f, mlstm_norm_w, w_out, norm2_w, w_up, w_down, final_norm_w):
    depth = w_in.shape[0]
    layers = [_prep_layer(norm1_w[l], w_in[l], conv_w[l], conv_b[l], dt_bias[l], a_log[l], d_skip[l],
                          ssd_norm_w[l], b_i[l], b_f[l], mlstm_norm_w[l], w_out[l], norm2_w[l],
                          w_up[l], w_down[l]) for l in range(depth)]
    y_prompt = _trunk(x_prompt, layers, final_norm_w)
    y_sample = _trunk(x_sample, layers, final_norm_w)
    return (y_prompt, y_sample)
```

```python
import functools

import jax
import jax.numpy as jnp
from jax import lax
from jax.experimental import pallas as pl
from jax.experimental.pallas import tpu as pltpu

F32 = jnp.float32
BF16 = jnp.bfloat16

CHUNK = 128
CHUNKS_PER_STEP = 2
EPS = 1e-5
D_CONV = 5
SSD_GROUPS = 8
SSD_HEAD_DIM = 64
SSD_STATE = 128
MLSTM_HEADS = 16
VMEM_LIMIT = 56 * 1024 * 1024
KSPLIT_VMEM_LIMIT = 60 * 1024 * 1024


def _cparams(sem, vmem_limit=VMEM_LIMIT):
    return pltpu.CompilerParams(dimension_semantics=sem, vmem_limit_bytes=vmem_limit)


def _sigmoid(x):
    return 1.0 / (1.0 + jnp.exp(-x))


def _softplus(x):
    return jnp.maximum(x, 0.0) + jnp.log1p(jnp.exp(-jnp.abs(x)))


def _dot(a, b):
    return jnp.dot(a, b, preferred_element_type=F32)


def _dot_nt(a, b):
    return lax.dot_general(a, b, (((1,), (1,)), ((), ())), preferred_element_type=F32)


def _rmsnorm_kernel(x_ref, w_ref, o_ref):
    x = x_ref[...].astype(F32)
    y = x * lax.rsqrt(jnp.mean(x * x, axis=-1, keepdims=True) + EPS)
    o_ref[...] = (y * w_ref[...]).astype(o_ref.dtype)


def rmsnorm_rows(x, w, out_dtype, bm=512):
    m, d = x.shape
    return pl.pallas_call(
        _rmsnorm_kernel,
        grid=(m // bm,),
        in_specs=[pl.BlockSpec((bm, d), lambda i: (i, 0)),
                  pl.BlockSpec((1, d), lambda i: (0, 0))],
        out_specs=pl.BlockSpec((bm, d), lambda i: (i, 0)),
        out_shape=jax.ShapeDtypeStruct((m, d), out_dtype),
        compiler_params=_cparams(("parallel",)),
        name="rmsnorm_rows",
    )(x, w.reshape(1, d).astype(F32))


def _matmul_kernel(*refs, relu2, row_ssq_dim):
    if row_ssq_dim:
        a_ref, b_ref, ssq_ref, o_ref = refs
    else:
        a_ref, b_ref, o_ref = refs
    acc = _dot(a_ref[...], b_ref[...])
    if relu2:
        acc = jnp.maximum(acc, 0.0)
        acc = acc * acc
    if row_ssq_dim:
        ssq = ssq_ref[...]
        tot = ssq[:, 0:128]
        for part in range(1, ssq.shape[1] // 128):
            tot = tot + ssq[:, part * 128:(part + 1) * 128]
        r2 = 1.0 / (tot * (1.0 / row_ssq_dim) + EPS)
        acc = acc * jnp.concatenate([r2] * (acc.shape[1] // 128), axis=1)
    o_ref[...] = acc.astype(o_ref.dtype)


def matmul(a, b, out_dtype, bm=1024, bn=1024, relu2=False, col0=0, n=None, row_ssq=None):
    m, k = a.shape
    n = b.shape[1] if n is None else n
    in_specs = [pl.BlockSpec((bm, k), lambda j, i: (i, 0)),
                pl.BlockSpec((pl.Element(k), pl.Element(bn)),
                             lambda j, i: (0, pl.multiple_of(col0 + j * bn, 128)))]
    args = [a, b]
    if row_ssq is not None:
        assert relu2
        in_specs.append(pl.BlockSpec((bm, row_ssq.shape[1]), lambda j, i: (i, 0)))
        args.append(row_ssq)
    return pl.pallas_call(
        functools.partial(_matmul_kernel, relu2=relu2, row_ssq_dim=k if row_ssq is not None else 0),
        grid=(n // bn, m // bm),
        in_specs=in_specs,
        out_specs=pl.BlockSpec((bm, bn), lambda j, i: (i, j)),
        out_shape=jax.ShapeDtypeStruct((m, n), out_dtype),
        compiler_params=_cparams(("parallel", "parallel")),
        name="matmul_relu2" if relu2 else "matmul",
    )(*args)


def _matmul_ksplit_res_kernel(a_ref, b_ref, x_ref, o_ref):
    @pl.when(pl.program_id(2) == 0)
    def _():
        o_ref[...] = x_ref[...] + _dot(a_ref[...], b_ref[...])

    @pl.when(pl.program_id(2) != 0)
    def _():
        o_ref[...] = o_ref[...] + _dot(a_ref[...], b_ref[...])


def matmul_ksplit_residual(a, b, x, bm=1024, bn=1024, bk=4096):
    m, k = a.shape
    n = b.shape[1]
    return pl.pallas_call(
        _matmul_ksplit_res_kernel,
        grid=(n // bn, m // bm, k // bk),
        in_specs=[pl.BlockSpec((bm, bk), lambda j, i, kk: (i, kk)),
                  pl.BlockSpec((bk, bn), lambda j, i, kk: (kk, j)),
                  pl.BlockSpec((bm, bn), lambda j, i, kk: (i, j))],
        out_specs=pl.BlockSpec((bm, bn), lambda j, i, kk: (i, j)),
        out_shape=jax.ShapeDtypeStruct((m, n), F32),
        compiler_params=_cparams(("parallel", "parallel", "arbitrary"), KSPLIT_VMEM_LIMIT),
        name="matmul_ksplit_residual",
    )(a, b, x)


def _split3(x):
    hi = x.astype(BF16)
    r1 = x - hi.astype(F32)
    mid = r1.astype(BF16)
    r2 = r1 - mid.astype(F32)
    return hi, mid, r2.astype(BF16)


def _gates_kernel(x_ref, nw_ref, w_ref, bias_ref, alog_ref,
                  h_ref, ssd_ref, ssdT_ref, ml_ref, mlT_ref, *, bm):
    x = x_ref[...]
    h = (x * lax.rsqrt(jnp.mean(x * x, axis=-1, keepdims=True) + EPS) * nw_ref[...]).astype(BF16)
    h_ref[...] = h
    raw = _dot(h, w_ref[...]) + bias_ref[...]
    dt = _softplus(raw[:, 0:128])
    a = dt * (-jnp.exp(alog_ref[...]))
    t2 = raw[:, 128:256]
    lane = lax.broadcasted_iota(jnp.int32, (CHUNK, 128), 1)
    row_t = lax.broadcasted_iota(jnp.int32, (CHUNK, 128), 0)
    lsig = -_softplus(-t2)
    row_i = lax.broadcasted_iota(jnp.int32, (CHUNK, CHUNK), 0)
    col_i = lax.broadcasted_iota(jnp.int32, (CHUNK, CHUNK), 1)
    lower = (col_i <= row_i).astype(BF16)
    upper = (col_i >= row_i).astype(BF16)
    ones = jnp.ones((CHUNK, CHUNK), BF16)
    ssd_fwd_lane = lane < 64
    ml_fwd_lane = (lane % 32) < 16

    def cums(x, fwd_lane):
        hi, mid, lo = _split3(x)
        cum_f = _dot(lower, hi) + _dot(lower, mid) + _dot(lower, lo)
        cum_b = _dot(upper, hi) + _dot(upper, mid) + _dot(upper, lo)
        tot = _dot(ones, hi) + _dot(ones, mid) + _dot(ones, lo)
        return jnp.where(fwd_lane, cum_f, cum_b), tot

    for c in range(bm // CHUNK):
        sl = slice(c * CHUNK, (c + 1) * CHUNK)
        dt_c = dt[sl]
        acum, tot = cums(a[sl], ssd_fwd_lane)
        ssd_ref[sl, 0:128] = dt_c
        ssd_ref[sl, 128:256] = acum
        ssd_ref[sl, 256:384] = tot
        ssdT_ref[0:128, sl] = dt_c.T
        ssdT_ref[128:256, sl] = acum.T
        ssdT_ref[256:384, sl] = tot.T
        t2_c = t2[sl]
        mcum, mtot = cums(lsig[sl], ml_fwd_lane)
        y = mcum - pltpu.roll(t2_c, 32, 1)
        y_f, y_b = y, y
        for d in (1, 2, 4, 8, 16, 32, 64):
            y_f = jnp.minimum(y_f, jnp.where(row_t >= d, pltpu.roll(y_f, d, 0), jnp.inf))
            y_b = jnp.minimum(y_b, jnp.where(row_t < CHUNK - d, pltpu.roll(y_b, CHUNK - d, 0), jnp.inf))
        rmax = mcum - jnp.where(ml_fwd_lane, y_f, y_b)
        ml_ref[sl, 0:128] = t2_c
        ml_ref[sl, 128:256] = mcum
        ml_ref[sl, 256:384] = mtot
        ml_ref[sl, 384:512] = rmax
        mlT_ref[0:128, sl] = t2_c.T
        mlT_ref[128:256, sl] = mcum.T
        mlT_ref[256:384, sl] = mtot.T


def gates(x, norm_w, w_gate, bias, alog, bm=512):
    m, d = x.shape
    nat = pl.BlockSpec((bm, 384), lambda i: (i, 0))
    nat4 = pl.BlockSpec((bm, 512), lambda i: (i, 0))
    tr = pl.BlockSpec((384, bm), lambda i: (0, i))
    return pl.pallas_call(
        functools.partial(_gates_kernel, bm=bm),
        grid=(m // bm,),
        in_specs=[pl.BlockSpec((bm, d), lambda i: (i, 0)),
                  pl.BlockSpec((1, d), lambda i: (0, 0)),
                  pl.BlockSpec((d, 256), lambda i: (0, 0)),
                  pl.BlockSpec((1, 256), lambda i: (0, 0)),
                  pl.BlockSpec((1, 128), lambda i: (0, 0))],
        out_specs=[pl.BlockSpec((bm, d), lambda i: (i, 0)), nat, tr, nat4, tr],
        out_shape=[jax.ShapeDtypeStruct((m, d), BF16),
                   jax.ShapeDtypeStruct((m, 384), F32), jax.ShapeDtypeStruct((384, m), F32),
                   jax.ShapeDtypeStruct((m, 512), F32), jax.ShapeDtypeStruct((384, m), F32)],
        compiler_params=_cparams(("parallel",)),
        name="gates",
    )(x, norm_w.reshape(1, d).astype(F32), w_gate, bias, alog)


def _matmul_conv_kernel(h_ref, hp_ref, hn_ref, w_ref, cw_ref, cb_ref, o_ref, *, bm, seq_len,
                        transpose_out, sub):
    i = pl.program_id(1)
    w = w_ref[...]
    at_start = (i * bm) % seq_len == 0
    at_end = ((i + 1) * bm) % seq_len == 0
    cw = cw_ref[...]
    cb = cb_ref[...]
    nsub = bm // sub
    row = lax.broadcasted_iota(jnp.int32, (sub, w.shape[1]), 0)

    def conv(cur, prev8, next8):
        m2 = pltpu.roll(cur, 2, 0)
        m2 = jnp.where(row == 0, prev8[6:7], jnp.where(row == 1, prev8[7:8], m2))
        m1 = pltpu.roll(cur, 1, 0)
        m1 = jnp.where(row == 0, prev8[7:8], m1)
        p1 = pltpu.roll(cur, sub - 1, 0)
        p1 = jnp.where(row == sub - 1, next8[0:1], p1)
        p2 = pltpu.roll(cur, sub - 2, 0)
        p2 = jnp.where(row == sub - 2, next8[0:1], jnp.where(row == sub - 1, next8[1:2], p2))
        out = cw[0:1] * m2
        out = out + cw[1:2] * m1
        out = out + cw[2:3] * cur
        out = out + cw[3:4] * p1
        out = out + cw[4:5] * p2
        out = out + cb
        return out * _sigmoid(out)

    def emit(r, res):
        if transpose_out:
            o_ref[:, r * sub:(r + 1) * sub] = res.T.astype(o_ref.dtype)
        else:
            o_ref[r * sub:(r + 1) * sub, :] = res.astype(o_ref.dtype)

    prev8 = jnp.where(at_start, 0.0, _dot(hp_ref[...], w))[8:16]
    blocks = [_dot(h_ref[0:sub, :], w)]
    for r in range(nsub):
        if r + 1 < nsub:
            blocks.append(_dot(h_ref[(r + 1) * sub:(r + 2) * sub, :], w))
            next8 = blocks[r + 1][0:8]
        else:
            next8 = jnp.where(at_end, 0.0, _dot(hn_ref[...], w))[0:8]
        emit(r, conv(blocks[r], prev8, next8))
        prev8 = blocks[r][sub - 8:sub]


def matmul_conv(h, w, col0, n, conv_w, conv_b, seq_len, transpose_out, bm=1024, bn=1024, sub=256):
    m, k = h.shape
    assert seq_len % bm == 0 and bm % sub == 0
    nhb = m // 16
    if transpose_out:
        out_spec = pl.BlockSpec((bn, bm), lambda j, i: (j, i))
        out_shape = jax.ShapeDtypeStruct((n, m), BF16)
    else:
        out_spec = pl.BlockSpec((bm, bn), lambda j, i: (i, j))
        out_shape = jax.ShapeDtypeStruct((m, n), BF16)
    return pl.pallas_call(
        functools.partial(_matmul_conv_kernel, bm=bm, seq_len=seq_len, transpose_out=transpose_out,
                          sub=sub),
        grid=(n // bn, m // bm),
        in_specs=[pl.BlockSpec((bm, k), lambda j, i: (i, 0)),
                  pl.BlockSpec((16, k), lambda j, i: (jnp.maximum(i * (bm // 16) - 1, 0), 0)),
                  pl.BlockSpec((16, k), lambda j, i: (jnp.minimum((i + 1) * (bm // 16), nhb - 1), 0)),
                  pl.BlockSpec((pl.Element(k), pl.Element(bn)),
                               lambda j, i: (0, pl.multiple_of(col0 + j * bn, 128))),
                  pl.BlockSpec((D_CONV, bn), lambda j, i: (0, j)),
                  pl.BlockSpec((1, bn), lambda j, i: (0, j))],
        out_specs=out_spec,
        out_shape=out_shape,
        compiler_params=_cparams(("parallel", "parallel")),
        name="matmul_conv_t" if transpose_out else "matmul_conv",
    )(h, h, h, w, conv_w, conv_b)


def _ssd_kernel(*refs, backward, cb, mxu_cols):
    if backward:
        (xsT_ref, b_ref, c_ref, nat_ref, tr_ref, yf_ref, z_ref, nw_ref, o_ref, s_ref) = refs
    else:
        (xsT_ref, b_ref, c_ref, nat_ref, tr_ref, dexp_ref, o_ref, s_ref) = refs
    G, R, P, N = SSD_GROUPS, 8, SSD_HEAD_DIM, SSD_STATE
    gw = R * P
    h0 = G * R if backward else 0

    @pl.when(pl.program_id(1) == 0)
    def _():
        s_ref[...] = jnp.zeros_like(s_ref)

    s_i = lax.broadcasted_iota(jnp.int32, (CHUNK, CHUNK), 0)
    l_i = lax.broadcasted_iota(jnp.int32, (CHUNK, CHUNK), 1)
    mask = (l_i <= s_i) if backward else (l_i >= s_i)
    if mxu_cols:
        k_i = lax.broadcasted_iota(jnp.int32, (128, 2 * CHUNK), 0)
        c_i = lax.broadcasted_iota(jnp.int32, (128, 2 * CHUNK), 1)
    acum_split = {}
    chunk_order = range(cb - 1, -1, -1) if backward else range(cb)
    for j, g in [(j, g) for j in chunk_order for g in range(G)]:
        ts = slice(j * CHUNK, (j + 1) * CHUNK)
        gs = slice(g * gw, (g + 1) * gw)
        hg = h0 + g * R
        col_pairs = []
        if mxu_cols:
            if j not in acum_split:
                acum_split[j] = _split3(nat_ref[ts, 128:256])
            hi, mid, lo = acum_split[j]
            for pr in range(mxu_cols // 2):
                sel = (k_i == hg + 2 * pr + c_i // CHUNK).astype(BF16)
                col_pairs.append(_dot(hi, sel) + _dot(mid, sel) + _dot(lo, sel))
        bm = b_ref[ts, g * N:(g + 1) * N]
        cm = c_ref[ts, g * N:(g + 1) * N]
        dt = tr_ref[hg:hg + R, ts]
        acum = tr_ref[128 + hg:128 + hg + R, ts]
        tot = tr_ref[256 + hg:256 + hg + R, ts]
        cbT = _dot_nt(bm, cm)
        s_old = s_ref[gs, :]
        yoffT = _dot_nt(s_old.astype(BF16), cm)
        e_acum = jnp.exp(acum)
        dte = jnp.exp(tot - acum)
        e_tot = jnp.exp(tot)
        y_pieces = []
        xd_pieces = []
        for r in range(R):
            hs = slice(g * gw + r * P, g * gw + (r + 1) * P)
            xr = xsT_ref[hs, ts].astype(F32)
            xdt = xr * dt[r:r + 1, :]
            if r < mxu_cols:
                col = col_pairs[r // 2][:, (r % 2) * CHUNK:(r % 2 + 1) * CHUNK]
            else:
                col = nat_ref[ts, 128 + hg + r:128 + hg + r + 1]
            seg = acum[r:r + 1, :] - col
            dec = jnp.exp(jnp.where(mask, seg, -jnp.inf))
            mt = (cbT * dec).astype(BF16)
            y_r = _dot(xdt.astype(BF16), mt) + yoffT[r * P:(r + 1) * P, :] * e_acum[r:r + 1, :]
            if not backward:
                y_r = y_r + dexp_ref[hs, :] * xr
            y_pieces.append(y_r)
            xd_pieces.append((xdt * dte[r:r + 1, :]).astype(BF16))
        yT = jnp.concatenate(y_pieces, axis=0)
        upd = _dot(jnp.concatenate(xd_pieces, axis=0), bm)
        for r in range(R):
            hs = slice(r * P, (r + 1) * P)
            s_ref[g * gw + r * P:g * gw + (r + 1) * P, :] = s_old[hs, :] * e_tot[r:r + 1, :] + upd[hs, :]
        if not backward:
            o_ref[gs, ts] = yT
            continue
        y = (yT + yf_ref[gs, ts]).T
        z = z_ref[ts, gs].astype(F32)
        y = y * (z * _sigmoid(z))
        y = y * lax.rsqrt(jnp.mean(y * y, axis=-1, keepdims=True) + EPS)
        o_ref[ts, gs] = (y * nw_ref[:, gs]).astype(o_ref.dtype)


def ssd_scan(xsT, bc, ssd, ssdT, batch, seq_len, *, backward, dexp=None,
             y_fwd=None, proj=None, norm_w=None):
    m = xsT.shape[1]
    cb = CHUNKS_PER_STEP
    tb = CHUNK * cb
    assert seq_len % tb == 0 and m == batch * seq_len
    nc = seq_len // tb
    G = SSD_GROUPS
    d_ssd = xsT.shape[0]

    def cg(b, c):
        return b * nc + ((nc - 1 - c) if backward else c)

    in_specs = [
        pl.BlockSpec((d_ssd, tb), lambda b, c: (0, cg(b, c))),
        pl.BlockSpec((tb, G * SSD_STATE), lambda b, c: (cg(b, c), 0)),
        pl.BlockSpec((tb, G * SSD_STATE), lambda b, c: (cg(b, c), 1)),
        pl.BlockSpec((tb, 384), lambda b, c: (cg(b, c), 0)),
        pl.BlockSpec((384, tb), lambda b, c: (0, cg(b, c))),
    ]
    args = [xsT, bc, bc, ssd, ssdT]
    if backward:
        in_specs += [
            pl.BlockSpec((d_ssd, tb), lambda b, c: (0, cg(b, c))),
            pl.BlockSpec((tb, d_ssd), lambda b, c: (cg(b, c), 0)),
            pl.BlockSpec((1, d_ssd), lambda b, c: (0, 0)),
        ]
        args += [y_fwd, proj, norm_w]
        out_spec = pl.BlockSpec((tb, d_ssd), lambda b, c: (cg(b, c), 0))
        out_shape = jax.ShapeDtypeStruct((m, d_ssd), BF16)
    else:
        in_specs += [pl.BlockSpec((d_ssd, 128), lambda b, c: (0, 0))]
        args += [dexp]
        out_spec = pl.BlockSpec((d_ssd, tb), lambda b, c: (0, cg(b, c)))
        out_shape = jax.ShapeDtypeStruct((d_ssd, m), F32)
    return pl.pallas_call(
        functools.partial(_ssd_kernel, backward=backward, cb=cb, mxu_cols=2 if backward else 0),
        grid=(batch, nc),
        in_specs=in_specs,
        out_specs=out_spec,
        out_shape=out_shape,
        scratch_shapes=[pltpu.VMEM((d_ssd, SSD_STATE), F32)],
        compiler_params=_cparams(("parallel", "arbitrary")),
        name="ssd_bwd" if backward else "ssd_fwd",
    )(*args)


def _mlstm_kernel(*refs, backward, dk, dv, heads_per_group, cb):
    if backward:
        (q_ref, k_ref, v_ref, nat_ref, tr_ref, hf_ref, og_ref, nw_ref, o_ref, c_ref, m_ref) = refs
    else:
        (q_ref, k_ref, v_ref, nat_ref, tr_ref, o_ref, c_ref, m_ref) = refs
    H = MLSTM_HEADS

    @pl.when(pl.program_id(1) == 0)
    def _():
        c_ref[...] = jnp.zeros_like(c_ref)
        m_ref[...] = jnp.zeros_like(m_ref)

    t_i = lax.broadcasted_iota(jnp.int32, (CHUNK, CHUNK), 0)
    s_i = lax.broadcasted_iota(jnp.int32, (CHUNK, CHUNK), 1)
    mask = (s_i >= t_i) if backward else (s_i <= t_i)
    ones_blk = jnp.ones((CHUNK, 128), BF16)
    scale = dk ** -0.5
    dense = (CHUNK, CHUNK)

    chunk_order = range(cb - 1, -1, -1) if backward else range(cb)
    for j, g0 in [(j, g0) for j in chunk_order for g0 in range(0, H, heads_per_group)]:
        ts = slice(j * CHUNK, (j + 1) * CHUNK)
        hs = list(range(g0, g0 + heads_per_group))
        lane_of = {h: h + (H if backward else 0) for h in hs}
        st = {h: {} for h in hs}
        for h in hs:
            d, hh = st[h], lane_of[h]
            li_row = tr_ref[hh:hh + 1, ts]
            cum_row = tr_ref[160 + hh:161 + hh, ts]
            d["tot"] = tr_ref[288 + hh:289 + hh, ts]
            d["base_row"] = cum_row - li_row
            grow = d["tot"] - d["base_row"]
            d["m_in"] = m_ref[h, 0:1, :]
            m_loc = jnp.broadcast_to(jnp.max(grow, axis=1, keepdims=True), (1, CHUNK))
            d["m_new"] = jnp.maximum(d["tot"] + d["m_in"], m_loc)
            d["w_row"] = jnp.exp(grow - d["m_new"])
            d["cum_d"] = jnp.broadcast_to(nat_ref[ts, 160 + hh:161 + hh], dense)
            d["rmax_d"] = jnp.broadcast_to(nat_ref[ts, 416 + hh:417 + hh], dense)
            d["v_aug"] = jnp.concatenate([v_ref[ts, h * dv:(h + 1) * dv].astype(BF16), ones_blk], axis=1)
        for h in hs:
            d = st[h]
            k = k_ref[ts, h * dk:(h + 1) * dk].astype(F32)
            d["kb"] = k.astype(BF16)
            d["kwT"] = (k.T * d["w_row"]).astype(BF16)
            d["q32"] = q_ref[ts, h * dk:(h + 1) * dk].astype(F32) * scale
            d["qs"] = d["q32"].astype(BF16)
        for h in hs:
            d = st[h]
            d["c_loc"] = _dot(d["kwT"], d["v_aug"])
            d["sqk"] = _dot_nt(d["qs"], d["kb"])
            d["c_in"] = c_ref[h]
        for h in hs:
            d = st[h]
            dlog = jnp.where(mask, d["cum_d"] - d["base_row"], -jnp.inf)
            inter = d["cum_d"] + d["m_in"]
            d["m_t"] = jnp.maximum(d["rmax_d"], inter)
            d["pm"] = (jnp.exp(dlog - d["m_t"]) * d["sqk"]).astype(BF16)
            d["qa"] = (d["q32"] * jnp.exp(inter - d["m_t"])).astype(BF16)
        for h in hs:
            d = st[h]
            lhs = jnp.concatenate([d["pm"], d["qa"]], axis=1)
            rhs = jnp.concatenate([d["v_aug"], d["c_in"].astype(BF16)], axis=0)
            num = _dot(lhs, rhs)
            den = num[:, dv:dv + 128]
            inv = 1.0 / jnp.maximum(jnp.abs(den), jnp.exp(-d["m_t"]))
            d["hout"] = num[:, 0:dv] * jnp.concatenate([inv] * (dv // 128), axis=1)
        for h in hs:
            d = st[h]
            a_prev = jnp.exp(d["tot"] + d["m_in"] - d["m_new"])
            a_prev3 = jnp.concatenate([a_prev] * (dv // 128 + 1), axis=1)
            c_ref[h] = a_prev3 * d["c_in"] + d["c_loc"]
            m_ref[h] = jnp.broadcast_to(d["m_new"], (8, 128))
        for h in hs:
            hout = st[h]["hout"]
            vs = slice(h * dv, (h + 1) * dv)
            if backward:
                hout = hout + hf_ref[ts, vs]
                hout = hout * lax.rsqrt(jnp.mean(hout * hout, axis=-1, keepdims=True) + EPS)
                hout = hout * nw_ref[:, vs]
                o_ref[ts, vs] = (_sigmoid(og_ref[ts, vs].astype(F32)) * hout).astype(o_ref.dtype)
            else:
                o_ref[ts, vs] = hout


def mlstm_scan(proj, ml, mlT, batch, seq_len, cols, *, backward, h_fwd=None, norm_w=None):
    m = proj.shape[0]
    cb = CHUNKS_PER_STEP
    tb = CHUNK * cb
    assert seq_len % tb == 0 and m == batch * seq_len
    nc = seq_len // tb
    H = MLSTM_HEADS
    dk, dv = cols["dk"], cols["dv"]
    qw, vw = H * dk, H * dv
    qb, kb, vb, ob = cols["q"] // qw, cols["k"] // qw, cols["v"] // vw, cols["o"] // vw
    assert qb * qw == cols["q"] and kb * qw == cols["k"] and vb * vw == cols["v"] and ob * vw == cols["o"]

    def cg(b, c):
        return b * nc + ((nc - 1 - c) if backward else c)

    in_specs = [
        pl.BlockSpec((tb, qw), lambda b, c: (cg(b, c), qb)),
        pl.BlockSpec((tb, qw), lambda b, c: (cg(b, c), kb)),
        pl.BlockSpec((tb, vw), lambda b, c: (cg(b, c), vb)),
        pl.BlockSpec((tb, 512), lambda b, c: (cg(b, c), 0)),
        pl.BlockSpec((384, tb), lambda b, c: (0, cg(b, c))),
    ]
    args = [proj, proj, proj, ml, mlT]
    if backward:
        in_specs += [
            pl.BlockSpec((tb, vw), lambda b, c: (cg(b, c), 0)),
            pl.BlockSpec((tb, vw), lambda b, c: (cg(b, c), ob)),
            pl.BlockSpec((1, vw), lambda b, c: (0, 0)),
        ]
        args += [h_fwd, proj, norm_w]
        out_dtype = BF16
    else:
        out_dtype = F32
    return pl.pallas_call(
        functools.partial(_mlstm_kernel, backward=backward, dk=dk, dv=dv,
                          heads_per_group=8 if backward else 4, cb=cb),
        grid=(batch, nc),
        in_specs=in_specs,
        out_specs=pl.BlockSpec((tb, vw), lambda b, c: (cg(b, c), 0)),
        out_shape=jax.ShapeDtypeStruct((m, vw), out_dtype),
        scratch_shapes=[pltpu.VMEM((H, dk, dv + 128), F32), pltpu.VMEM((H, 8, 128), F32)],
        compiler_params=_cparams(("parallel", "arbitrary")),
        name="mlstm_bwd" if backward else "mlstm_fwd",
    )(*args)


def _outproj_kernel(a1_ref, a2_ref, w1_ref, w2_ref, x_ref, nw_ref, o_ref, xw_ref, ssq_ref):
    acc = _dot(a1_ref[...], w1_ref[...]) + _dot(a2_ref[...], w2_ref[...])
    x1 = x_ref[...] + acc
    o_ref[...] = x1
    xw_ref[...] = (x1 * nw_ref[...]).astype(xw_ref.dtype)
    ssq_ref[...] = jnp.broadcast_to(jnp.sum(x1 * x1, axis=-1, keepdims=True), ssq_ref.shape)


def outproj_residual(a1, a2, w1, w2, x, norm_w, bm=512, bn=512):
    m, k = a1.shape
    n = w1.shape[1]
    return pl.pallas_call(
        _outproj_kernel,
        grid=(n // bn, m // bm),
        in_specs=[pl.BlockSpec((bm, k), lambda j, i: (i, 0)),
                  pl.BlockSpec((bm, k), lambda j, i: (i, 0)),
                  pl.BlockSpec((k, bn), lambda j, i: (0, j)),
                  pl.BlockSpec((k, bn), lambda j, i: (0, j)),
                  pl.BlockSpec((bm, bn), lambda j, i: (i, j)),
                  pl.BlockSpec((1, bn), lambda j, i: (0, j))],
        out_specs=[pl.BlockSpec((bm, bn), lambda j, i: (i, j)),
                   pl.BlockSpec((bm, bn), lambda j, i: (i, j)),
                   pl.BlockSpec((bm, 128), lambda j, i: (i, j))],
        out_shape=[jax.ShapeDtypeStruct((m, n), F32), jax.ShapeDtypeStruct((m, n), BF16),
                   jax.ShapeDtypeStruct((m, (n // bn) * 128), F32)],
        compiler_params=_cparams(("parallel", "parallel")),
        name="outproj_residual",
    )(a1, a2, w1, w2, x, norm_w.reshape(1, n).astype(F32))


def _prep_layer(norm1_w, w_in, conv_w, conv_b, dt_bias, a_log, d_skip, ssd_norm_w, b_i, b_f,
                mlstm_norm_w, w_out, norm2_w, w_up, w_down):
    d_model = w_in.shape[0]
    d_mix = w_out.shape[0]
    d_ssd = d_mix // 2
    d_ml = d_mix - d_ssd
    n_ssd_heads = d_ssd // SSD_HEAD_DIM
    xbc_w = d_ssd + 2 * SSD_GROUPS * SSD_STATE
    dv = d_ml // MLSTM_HEADS
    dk = dv // 2
    widths = (d_ssd, xbc_w, 2 * n_ssd_heads, MLSTM_HEADS * dk, MLSTM_HEADS * dk, d_ml, d_ml,
              2 * MLSTM_HEADS, 2 * MLSTM_HEADS)
    offs = [0]
    for wd in widths:
        offs.append(offs[-1] + wd)
    assert offs[-1] == w_in.shape[1]
    assert 2 * n_ssd_heads == 128 and 2 * MLSTM_HEADS == 32
    seg = lambda i: w_in[:, offs[i]:offs[i + 1]]
    w_bf = w_in.astype(BF16)
    w_gate = jnp.concatenate([seg(2), seg(7), seg(8), jnp.zeros((d_model, 64), w_in.dtype)],
                             axis=1).astype(BF16)
    gate_bias = jnp.concatenate([dt_bias.reshape(-1), b_i.reshape(-1), b_f.reshape(-1),
                                 jnp.zeros((64,), F32)]).astype(F32).reshape(1, 256)
    cols = {"q": 0, "k": MLSTM_HEADS * dk, "v": 2 * MLSTM_HEADS * dk}
    cols["o"] = cols["v"] + d_ml
    cols["dk"], cols["dv"], cols["d_ssd"] = dk, dv, d_ssd
    return dict(
        norm1_w=norm1_w, w_bf=w_bf, offs=offs, w_gate=w_gate, gate_bias=gate_bias,
        alog=a_log.reshape(1, 128).astype(F32),
        conv_w=conv_w.astype(F32), conv_b=conv_b.reshape(1, -1).astype(F32),
        dexp=jnp.broadcast_to(jnp.repeat(d_skip.astype(F32), SSD_HEAD_DIM)[:, None], (d_ssd, 128)),
        ssd_norm_w=ssd_norm_w.reshape(1, -1).astype(F32),
        mlstm_norm_w=mlstm_norm_w.reshape(1, -1).astype(F32),
        w_out1=w_out[:d_ssd].astype(BF16), w_out2=w_out[d_ssd:].astype(BF16),
        norm2_w=norm2_w, w_up=w_up.astype(BF16), w_down=w_down.astype(BF16), cols=cols)


def _layer(x, p, batch, seq_len):
    cols = p["cols"]
    d_ssd = cols["d_ssd"]
    h, ssd, ssdT, ml, mlT = gates(x, p["norm1_w"], p["w_gate"], p["gate_bias"], p["alog"])
    offs = p["offs"]
    proj_z = matmul(h, p["w_bf"], F32, col0=offs[0], n=d_ssd)
    proj_b = matmul(h, p["w_bf"], BF16, col0=offs[3], n=offs[7] - offs[3])
    xsT = matmul_conv(h, p["w_bf"], offs[1], d_ssd, p["conv_w"][:, :d_ssd], p["conv_b"][:, :d_ssd],
                      seq_len, True)
    bc = matmul_conv(h, p["w_bf"], offs[1] + d_ssd, 2 * SSD_GROUPS * SSD_STATE,
                     p["conv_w"][:, d_ssd:], p["conv_b"][:, d_ssd:], seq_len, False)
    y_f = ssd_scan(xsT, bc, ssd, ssdT, batch, seq_len, backward=False, dexp=p["dexp"])
    mix1 = ssd_scan(xsT, bc, ssd, ssdT, batch, seq_len, backward=True, y_fwd=y_f,
                    proj=proj_z, norm_w=p["ssd_norm_w"])
    h_f = mlstm_scan(proj_b, ml, mlT, batch, seq_len, cols, backward=False)
    mix2 = mlstm_scan(proj_b, ml, mlT, batch, seq_len, cols, backward=True, h_fwd=h_f,
                      norm_w=p["mlstm_norm_w"])
    x1, x1w, ssq = outproj_residual(mix1, mix2, p["w_out1"], p["w_out2"], x, p["norm2_w"])
    u = matmul(x1w, p["w_up"], BF16, relu2=True, row_ssq=ssq)
    return matmul_ksplit_residual(u, p["w_down"], x1)


def _trunk(x, layers, final_norm_w):
    batch, seq_len, d = x.shape
    xf = x.reshape(batch * seq_len, d)
    for p in layers:
        xf = _layer(xf, p, batch, seq_len)
    return rmsnorm_rows(xf, final_norm_w, F32).reshape(batch, seq_len, d)


def kernel(x_prompt, x_sample, norm1_w, w_in, conv_w, conv_b, dt_bias, a_log, d_skip, ssd_norm_w,
           b_i, b_f, mlstm_norm_w, w_out, norm2_w, w_up, w_down, final_norm_w):
    depth = w_in.shape[0]
    layers = [_prep_layer(norm1_w[l], w_in[l], conv_w[l], conv_b[l], dt_bias[l], a_log[l], d_skip[l],
                          ssd_norm_w[l], b_i[l], b_f[l], mlstm_norm_w[l], w_out[l], norm2_w[l],
                          w_up[l], w_down[l]) for l in range(depth)]
    y_prompt = _trunk(x_prompt, layers, final_norm_w)
    y_sample = _trunk(x_sample, layers, final_norm_w)
    return (y_prompt, y_sample)
```

```python
import functools

import jax
import jax.numpy as jnp
from jax import lax
from jax.experimental import pallas as pl
from jax.experimental.pallas import tpu as pltpu

F32 = jnp.float32
BF16 = jnp.bfloat16

CHUNK = 128
CHUNKS_PER_STEP = 2
EPS = 1e-5
D_CONV = 5
SSD_GROUPS = 8
SSD_HEAD_DIM = 64
SSD_STATE = 128
MLSTM_HEADS = 16
VMEM_LIMIT = 56 * 1024 * 1024
KSPLIT_VMEM_LIMIT = 60 * 1024 * 1024


def _cparams(sem, vmem_limit=VMEM_LIMIT):
    return pltpu.CompilerParams(dimension_semantics=sem, vmem_limit_bytes=vmem_limit)


def _sigmoid(x):
    return 1.0 / (1.0 + jnp.exp(-x))


def _softplus(x):
    return jnp.maximum(x, 0.0) + jnp.log1p(jnp.exp(-jnp.abs(x)))


def _dot(a, b):
    return jnp.dot(a, b, preferred_element_type=F32)


def _dot_nt(a, b):
    return lax.dot_general(a, b, (((1,), (1,)), ((), ())), preferred_element_type=F32)


def _rmsnorm_kernel(x_ref, w_ref, o_ref):
    x = x_ref[...].astype(F32)
    y = x * lax.rsqrt(jnp.mean(x * x, axis=-1, keepdims=True) + EPS)
    o_ref[...] = (y * w_ref[...]).astype(o_ref.dtype)


def rmsnorm_rows(x, w, out_dtype, bm=512):
    m, d = x.shape
    return pl.pallas_call(
        _rmsnorm_kernel,
        grid=(m // bm,),
        in_specs=[pl.BlockSpec((bm, d), lambda i: (i, 0)),
                  pl.BlockSpec((1, d), lambda i: (0, 0))],
        out_specs=pl.BlockSpec((bm, d), lambda i: (i, 0)),
        out_shape=jax.ShapeDtypeStruct((m, d), out_dtype),
        compiler_params=_cparams(("parallel",)),
        name="rmsnorm_rows",
    )(x, w.reshape(1, d).astype(F32))


def _matmul_kernel(*refs, relu2, row_ssq_dim):
    if row_ssq_dim:
        a_ref, b_ref, ssq_ref, o_ref = refs
    else:
        a_ref, b_ref, o_ref = refs
    acc = _dot(a_ref[...], b_ref[...])
    if relu2:
        acc = jnp.maximum(acc, 0.0)
        acc = acc * acc
    if row_ssq_dim:
        ssq = ssq_ref[...]
        tot = ssq[:, 0:128]
        for part in range(1, ssq.shape[1] // 128):
            tot = tot + ssq[:, part * 128:(part + 1) * 128]
        r2 = 1.0 / (tot * (1.0 / row_ssq_dim) + EPS)
        acc = acc * jnp.concatenate([r2] * (acc.shape[1] // 128), axis=1)
    o_ref[...] = acc.astype(o_ref.dtype)


def matmul(a, b, out_dtype, bm=1024, bn=1024, relu2=False, col0=0, n=None, row_ssq=None):
    m, k = a.shape
    n = b.shape[1] if n is None else n
    in_specs = [pl.BlockSpec((bm, k), lambda j, i: (i, 0)),
                pl.BlockSpec((pl.Element(k), pl.Element(bn)),
                             lambda j, i: (0, pl.multiple_of(col0 + j * bn, 128)))]
    args = [a, b]
    if row_ssq is not None:
        assert relu2
        in_specs.append(pl.BlockSpec((bm, row_ssq.shape[1]), lambda j, i: (i, 0)))
        args.append(row_ssq)
    return pl.pallas_call(
        functools.partial(_matmul_kernel, relu2=relu2, row_ssq_dim=k if row_ssq is not None else 0),
        grid=(n // bn, m // bm),
        in_specs=in_specs,
        out_specs=pl.BlockSpec((bm, bn), lambda j, i: (i, j)),
        out_shape=jax.ShapeDtypeStruct((m, n), out_dtype),
        compiler_params=_cparams(("parallel", "parallel")),
        name="matmul_relu2" if relu2 else "matmul",
    )(*args)


def matmul_relu2_pipelined(a, b, row_ssq, out_dtype, bm=1024, bn=1024):
    m, k = a.shape
    n = b.shape[1]
    body = functools.partial(_matmul_kernel, relu2=True, row_ssq_dim=k)

    def outer(a_hbm, b_hbm, ssq_hbm, o_hbm):
        pltpu.emit_pipeline(
            body,
            grid=(n // bn, m // bm),
            in_specs=[pl.BlockSpec((bm, k), lambda j, i: (i, 0), pipeline_mode=pl.Buffered(3)),
                      pl.BlockSpec((k, bn), lambda j, i: (0, j)),
                      pl.BlockSpec((bm, row_ssq.shape[1]), lambda j, i: (i, 0))],
            out_specs=[pl.BlockSpec((bm, bn), lambda j, i: (i, j))],
        )(a_hbm, b_hbm, ssq_hbm, o_hbm)

    return pl.pallas_call(
        outer,
        in_specs=[pl.BlockSpec(memory_space=pl.ANY)] * 3,
        out_specs=pl.BlockSpec(memory_space=pl.ANY),
        out_shape=jax.ShapeDtypeStruct((m, n), out_dtype),
        compiler_params=pltpu.CompilerParams(vmem_limit_bytes=VMEM_LIMIT),
        name="matmul_relu2_pipelined",
    )(a, b, row_ssq)


def _matmul_ksplit_res_kernel(a_ref, b_ref, x_ref, o_ref):
    @pl.when(pl.program_id(2) == 0)
    def _():
        o_ref[...] = x_ref[...] + _dot(a_ref[...], b_ref[...])

    @pl.when(pl.program_id(2) != 0)
    def _():
        o_ref[...] = o_ref[...] + _dot(a_ref[...], b_ref[...])


def matmul_ksplit_residual(a, b, x, bm=1024, bn=1024, bk=4096):
    m, k = a.shape
    n = b.shape[1]
    return pl.pallas_call(
        _matmul_ksplit_res_kernel,
        grid=(n // bn, m // bm, k // bk),
        in_specs=[pl.BlockSpec((bm, bk), lambda j, i, kk: (i, kk)),
                  pl.BlockSpec((bk, bn), lambda j, i, kk: (kk, j)),
                  pl.BlockSpec((bm, bn), lambda j, i, kk: (i, j))],
        out_specs=pl.BlockSpec((bm, bn), lambda j, i, kk: (i, j)),
        out_shape=jax.ShapeDtypeStruct((m, n), F32),
        compiler_params=_cparams(("parallel", "parallel", "arbitrary"), KSPLIT_VMEM_LIMIT),
        name="matmul_ksplit_residual",
    )(a, b, x)


def _split3(x):
    hi = x.astype(BF16)
    r1 = x - hi.astype(F32)
    mid = r1.astype(BF16)
    r2 = r1 - mid.astype(F32)
    return hi, mid, r2.astype(BF16)


def _gates_kernel(x_ref, nw_ref, w_ref, bias_ref, alog_ref,
                  h_ref, ssd_ref, ssdT_ref, ml_ref, mlT_ref, *, bm):
    x = x_ref[...]
    h = (x * lax.rsqrt(jnp.mean(x * x, axis=-1, keepdims=True) + EPS) * nw_ref[...]).astype(BF16)
    h_ref[...] = h
    raw = _dot(h, w_ref[...]) + bias_ref[...]
    dt = _softplus(raw[:, 0:128])
    a = dt * (-jnp.exp(alog_ref[...]))
    t2 = raw[:, 128:256]
    lane = lax.broadcasted_iota(jnp.int32, (CHUNK, 128), 1)
    row_t = lax.broadcasted_iota(jnp.int32, (CHUNK, 128), 0)
    lsig = -_softplus(-t2)
    row_i = lax.broadcasted_iota(jnp.int32, (CHUNK, CHUNK), 0)
    col_i = lax.broadcasted_iota(jnp.int32, (CHUNK, CHUNK), 1)
    lower = (col_i <= row_i).astype(BF16)
    upper = (col_i >= row_i).astype(BF16)
    ones = jnp.ones((CHUNK, CHUNK), BF16)
    ssd_fwd_lane = lane < 64
    ml_fwd_lane = (lane % 32) < 16

    def cums(x, fwd_lane):
        hi, mid, lo = _split3(x)
        cum_f = _dot(lower, hi) + _dot(lower, mid) + _dot(lower, lo)
        cum_b = _dot(upper, hi) + _dot(upper, mid) + _dot(upper, lo)
        tot = _dot(ones, hi) + _dot(ones, mid) + _dot(ones, lo)
        return jnp.where(fwd_lane, cum_f, cum_b), tot

    for c in range(bm // CHUNK):
        sl = slice(c * CHUNK, (c + 1) * CHUNK)
        dt_c = dt[sl]
        acum, tot = cums(a[sl], ssd_fwd_lane)
        ssd_ref[sl, 0:128] = dt_c
        ssd_ref[sl, 128:256] = acum
        ssd_ref[sl, 256:384] = tot
        ssdT_ref[0:128, sl] = dt_c.T
        ssdT_ref[128:256, sl] = acum.T
        ssdT_ref[256:384, sl] = tot.T
        t2_c = t2[sl]
        mcum, mtot = cums(lsig[sl], ml_fwd_lane)
        y = mcum - pltpu.roll(t2_c, 32, 1)
        y_f, y_b = y, y
        for d in (1, 2, 4, 8, 16, 32, 64):
            y_f = jnp.minimum(y_f, jnp.where(row_t >= d, pltpu.roll(y_f, d, 0), jnp.inf))
            y_b = jnp.minimum(y_b, jnp.where(row_t < CHUNK - d, pltpu.roll(y_b, CHUNK - d, 0), jnp.inf))
        rmax = mcum - jnp.where(ml_fwd_lane, y_f, y_b)
        ml_ref[sl, 0:128] = t2_c
        ml_ref[sl, 128:256] = mcum
        ml_ref[sl, 256:384] = mtot
        ml_ref[sl, 384:512] = rmax
        mlT_ref[0:128, sl] = t2_c.T
        mlT_ref[128:256, sl] = mcum.T
        mlT_ref[256:384, sl] = mtot.T


def gates(x, norm_w, w_gate, bias, alog, bm=512):
    m, d = x.shape
    nat = pl.BlockSpec((bm, 384), lambda i: (i, 0))
    nat4 = pl.BlockSpec((bm, 512), lambda i: (i, 0))
    tr = pl.BlockSpec((384, bm), lambda i: (0, i))
    return pl.pallas_call(
        functools.partial(_gates_kernel, bm=bm),
        grid=(m // bm,),
        in_specs=[pl.BlockSpec((bm, d), lambda i: (i, 0)),
                  pl.BlockSpec((1, d), lambda i: (0, 0)),
                  pl.BlockSpec((d, 256), lambda i: (0, 0)),
                  pl.BlockSpec((1, 256), lambda i: (0, 0)),
                  pl.BlockSpec((1, 128), lambda i: (0, 0))],
        out_specs=[pl.BlockSpec((bm, d), lambda i: (i, 0)), nat, tr, nat4, tr],
        out_shape=[jax.ShapeDtypeStruct((m, d), BF16),
                   jax.ShapeDtypeStruct((m, 384), F32), jax.ShapeDtypeStruct((384, m), F32),
                   jax.ShapeDtypeStruct((m, 512), F32), jax.ShapeDtypeStruct((384, m), F32)],
        compiler_params=_cparams(("parallel",)),
        name="gates",
    )(x, norm_w.reshape(1, d).astype(F32), w_gate, bias, alog)


def _matmul_conv_kernel(h_ref, hp_ref, hn_ref, w_ref, cw_ref, cb_ref, o_ref, *, bm, seq_len,
                        transpose_out, sub):
    i = pl.program_id(1)
    w = w_ref[...]
    at_start = (i * bm) % seq_len == 0
    at_end = ((i + 1) * bm) % seq_len == 0
    cw = cw_ref[...]
    cb = cb_ref[...]
    nsub = bm // sub
    row = lax.broadcasted_iota(jnp.int32, (sub, w.shape[1]), 0)

    def conv(cur, prev8, next8):
        m2 = pltpu.roll(cur, 2, 0)
        m2 = jnp.where(row == 0, prev8[6:7], jnp.where(row == 1, prev8[7:8], m2))
        m1 = pltpu.roll(cur, 1, 0)
        m1 = jnp.where(row == 0, prev8[7:8], m1)
        p1 = pltpu.roll(cur, sub - 1, 0)
        p1 = jnp.where(row == sub - 1, next8[0:1], p1)
        p2 = pltpu.roll(cur, sub - 2, 0)
        p2 = jnp.where(row == sub - 2, next8[0:1], jnp.where(row == sub - 1, next8[1:2], p2))
        out = cw[0:1] * m2
        out = out + cw[1:2] * m1
        out = out + cw[2:3] * cur
        out = out + cw[3:4] * p1
        out = out + cw[4:5] * p2
        out = out + cb
        return out * _sigmoid(out)

    def emit(r, res):
        if transpose_out:
            o_ref[:, r * sub:(r + 1) * sub] = res.T.astype(o_ref.dtype)
        else:
            o_ref[r * sub:(r + 1) * sub, :] = res.astype(o_ref.dtype)

    prev8 = jnp.where(at_start, 0.0, _dot(hp_ref[...], w))[8:16]
    blocks = [_dot(h_ref[0:sub, :], w)]
    for r in range(nsub):
        if r + 1 < nsub:
            blocks.append(_dot(h_ref[(r + 1) * sub:(r + 2) * sub, :], w))
            next8 = blocks[r + 1][0:8]
        else:
            next8 = jnp.where(at_end, 0.0, _dot(hn_ref[...], w))[0:8]
        emit(r, conv(blocks[r], prev8, next8))
        prev8 = blocks[r][sub - 8:sub]


def matmul_conv(h, w, col0, n, conv_w, conv_b, seq_len, transpose_out, bm=1024, bn=1024, sub=256):
    m, k = h.shape
    assert seq_len % bm == 0 and bm % sub == 0
    nhb = m // 16
    if transpose_out:
        out_spec = pl.BlockSpec((bn, bm), lambda j, i: (j, i))
        out_shape = jax.ShapeDtypeStruct((n, m), BF16)
    else:
        out_spec = pl.BlockSpec((bm, bn), lambda j, i: (i, j))
        out_shape = jax.ShapeDtypeStruct((m, n), BF16)
    return pl.pallas_call(
        functools.partial(_matmul_conv_kernel, bm=bm, seq_len=seq_len, transpose_out=transpose_out,
                          sub=sub),
        grid=(n // bn, m // bm),
        in_specs=[pl.BlockSpec((bm, k), lambda j, i: (i, 0)),
                  pl.BlockSpec((16, k), lambda j, i: (jnp.maximum(i * (bm // 16) - 1, 0), 0)),
                  pl.BlockSpec((16, k), lambda j, i: (jnp.minimum((i + 1) * (bm // 16), nhb - 1), 0)),
                  pl.BlockSpec((pl.Element(k), pl.Element(bn)),
                               lambda j, i: (0, pl.multiple_of(col0 + j * bn, 128))),
                  pl.BlockSpec((D_CONV, bn), lambda j, i: (0, j)),
                  pl.BlockSpec((1, bn), lambda j, i: (0, j))],
        out_specs=out_spec,
        out_shape=out_shape,
        compiler_params=_cparams(("parallel", "parallel")),
        name="matmul_conv_t" if transpose_out else "matmul_conv",
    )(h, h, h, w, conv_w, conv_b)


def _ssd_kernel(*refs, backward, cb, mxu_cols):
    if backward:
        (xsT_ref, b_ref, c_ref, nat_ref, tr_ref, yf_ref, z_ref, nw_ref, o_ref, s_ref) = refs
    else:
        (xsT_ref, b_ref, c_ref, nat_ref, tr_ref, dexp_ref, o_ref, s_ref) = refs
    G, R, P, N = SSD_GROUPS, 8, SSD_HEAD_DIM, SSD_STATE
    gw = R * P
    h0 = G * R if backward else 0

    @pl.when(pl.program_id(1) == 0)
    def _():
        s_ref[...] = jnp.zeros_like(s_ref)

    s_i = lax.broadcasted_iota(jnp.int32, (CHUNK, CHUNK), 0)
    l_i = lax.broadcasted_iota(jnp.int32, (CHUNK, CHUNK), 1)
    mask = (l_i <= s_i) if backward else (l_i >= s_i)
    if mxu_cols:
        k_i = lax.broadcasted_iota(jnp.int32, (128, 2 * CHUNK), 0)
        c_i = lax.broadcasted_iota(jnp.int32, (128, 2 * CHUNK), 1)
    acum_split = {}
    chunk_order = range(cb - 1, -1, -1) if backward else range(cb)
    for j, g in [(j, g) for j in chunk_order for g in range(G)]:
        ts = slice(j * CHUNK, (j + 1) * CHUNK)
        gs = slice(g * gw, (g + 1) * gw)
        hg = h0 + g * R
        col_pairs = []
        if mxu_cols:
            if j not in acum_split:
                acum_split[j] = _split3(nat_ref[ts, 128:256])
            hi, mid, lo = acum_split[j]
            for pr in range(mxu_cols // 2):
                sel = (k_i == hg + 2 * pr + c_i // CHUNK).astype(BF16)
                col_pairs.append(_dot(hi, sel) + _dot(mid, sel) + _dot(lo, sel))
        bm = b_ref[ts, g * N:(g + 1) * N]
        cm = c_ref[ts, g * N:(g + 1) * N]
        dt = tr_ref[hg:hg + R, ts]
        acum = tr_ref[128 + hg:128 + hg + R, ts]
        tot = tr_ref[256 + hg:256 + hg + R, ts]
        cbT = _dot_nt(bm, cm)
        s_old = s_ref[gs, :]
        yoffT = _dot_nt(s_old.astype(BF16), cm)
        e_acum = jnp.exp(acum)
        dte = jnp.exp(tot - acum)
        e_tot = jnp.exp(tot)
        y_pieces = []
        xd_pieces = []
        for r in range(R):
            hs = slice(g * gw + r * P, g * gw + (r + 1) * P)
            xr = xsT_ref[hs, ts].astype(F32)
            xdt = xr * dt[r:r + 1, :]
            if r < mxu_cols:
                col = col_pairs[r // 2][:, (r % 2) * CHUNK:(r % 2 + 1) * CHUNK]
            else:
                col = nat_ref[ts, 128 + hg + r:128 + hg + r + 1]
            seg = acum[r:r + 1, :] - col
            dec = jnp.exp(jnp.where(mask, seg, -jnp.inf))
            mt = (cbT * dec).astype(BF16)
            y_r = _dot(xdt.astype(BF16), mt) + yoffT[r * P:(r + 1) * P, :] * e_acum[r:r + 1, :]
            if not backward:
                y_r = y_r + dexp_ref[hs, :] * xr
            y_pieces.append(y_r)
            xd_pieces.append((xdt * dte[r:r + 1, :]).astype(BF16))
        yT = jnp.concatenate(y_pieces, axis=0)
        upd = _dot(jnp.concatenate(xd_pieces, axis=0), bm)
        for r in range(R):
            hs = slice(r * P, (r + 1) * P)
            s_ref[g * gw + r * P:g * gw + (r + 1) * P, :] = s_old[hs, :] * e_tot[r:r + 1, :] + upd[hs, :]
        if not backward:
            o_ref[gs, ts] = yT
            continue
        y = (yT + yf_ref[gs, ts]).T
        z = z_ref[ts, gs].astype(F32)
        y = y * (z * _sigmoid(z))
        y = y * lax.rsqrt(jnp.mean(y * y, axis=-1, keepdims=True) + EPS)
        o_ref[ts, gs] = (y * nw_ref[:, gs]).astype(o_ref.dtype)


def ssd_scan(xsT, bc, ssd, ssdT, batch, seq_len, *, backward, dexp=None,
             y_fwd=None, proj=None, norm_w=None):
    m = xsT.shape[1]
    cb = CHUNKS_PER_STEP
    tb = CHUNK * cb
    assert seq_len % tb == 0 and m == batch * seq_len
    nc = seq_len // tb
    G = SSD_GROUPS
    d_ssd = xsT.shape[0]

    def cg(b, c):
        return b * nc + ((nc - 1 - c) if backward else c)

    in_specs = [
        pl.BlockSpec((d_ssd, tb), lambda b, c: (0, cg(b, c))),
        pl.BlockSpec((tb, G * SSD_STATE), lambda b, c: (cg(b, c), 0)),
        pl.BlockSpec((tb, G * SSD_STATE), lambda b, c: (cg(b, c), 1)),
        pl.BlockSpec((tb, 384), lambda b, c: (cg(b, c), 0)),
        pl.BlockSpec((384, tb), lambda b, c: (0, cg(b, c))),
    ]
    args = [xsT, bc, bc, ssd, ssdT]
    if backward:
        in_specs += [
            pl.BlockSpec((d_ssd, tb), lambda b, c: (0, cg(b, c))),
            pl.BlockSpec((tb, d_ssd), lambda b, c: (cg(b, c), 0)),
            pl.BlockSpec((1, d_ssd), lambda b, c: (0, 0)),
        ]
        args += [y_fwd, proj, norm_w]
        out_spec = pl.BlockSpec((tb, d_ssd), lambda b, c: (cg(b, c), 0))
        out_shape = jax.ShapeDtypeStruct((m, d_ssd), BF16)
    else:
        in_specs += [pl.BlockSpec((d_ssd, 128), lambda b, c: (0, 0))]
        args += [dexp]
        out_spec = pl.BlockSpec((d_ssd, tb), lambda b, c: (0, cg(b, c)))
        out_shape = jax.ShapeDtypeStruct((d_ssd, m), F32)
    return pl.pallas_call(
        functools.partial(_ssd_kernel, backward=backward, cb=cb, mxu_cols=2 if backward else 0),
        grid=(batch, nc),
        in_specs=in_specs,
        out_specs=out_spec,
        out_shape=out_shape,
        scratch_shapes=[pltpu.VMEM((d_ssd, SSD_STATE), F32)],
        compiler_params=_cparams(("parallel", "arbitrary")),
        name="ssd_bwd" if backward else "ssd_fwd",
    )(*args)


def _mlstm_kernel(*refs, backward, dk, dv, heads_per_group, cb):
    if backward:
        (q_ref, k_ref, v_ref, nat_ref, tr_ref, hf_ref, og_ref, nw_ref, o_ref, c_ref, m_ref) = refs
    else:
        (q_ref, k_ref, v_ref, nat_ref, tr_ref, o_ref, c_ref, m_ref) = refs
    H = MLSTM_HEADS

    @pl.when(pl.program_id(1) == 0)
    def _():
        c_ref[...] = jnp.zeros_like(c_ref)
        m_ref[...] = jnp.zeros_like(m_ref)

    t_i = lax.broadcasted_iota(jnp.int32, (CHUNK, CHUNK), 0)
    s_i = lax.broadcasted_iota(jnp.int32, (CHUNK, CHUNK), 1)
    mask = (s_i >= t_i) if backward else (s_i <= t_i)
    ones_blk = jnp.ones((CHUNK, 128), BF16)
    scale = dk ** -0.5
    dense = (CHUNK, CHUNK)

    chunk_order = range(cb - 1, -1, -1) if backward else range(cb)
    for j, g0 in [(j, g0) for j in chunk_order for g0 in range(0, H, heads_per_group)]:
        ts = slice(j * CHUNK, (j + 1) * CHUNK)
        hs = list(range(g0, g0 + heads_per_group))
        lane_of = {h: h + (H if backward else 0) for h in hs}
        st = {h: {} for h in hs}
        for h in hs:
            d, hh = st[h], lane_of[h]
            li_row = tr_ref[hh:hh + 1, ts]
            cum_row = tr_ref[160 + hh:161 + hh, ts]
            d["tot"] = tr_ref[288 + hh:289 + hh, ts]
            d["base_row"] = cum_row - li_row
            grow = d["tot"] - d["base_row"]
            d["m_in"] = m_ref[h, 0:1, :]
            m_loc = jnp.broadcast_to(jnp.max(grow, axis=1, keepdims=True), (1, CHUNK))
            d["m_new"] = jnp.maximum(d["tot"] + d["m_in"], m_loc)
            d["w_row"] = jnp.exp(grow - d["m_new"])
            d["cum_d"] = jnp.broadcast_to(nat_ref[ts, 160 + hh:161 + hh], dense)
            d["rmax_d"] = jnp.broadcast_to(nat_ref[ts, 416 + hh:417 + hh], dense)
            d["v_aug"] = jnp.concatenate([v_ref[ts, h * dv:(h + 1) * dv].astype(BF16), ones_blk], axis=1)
        for h in hs:
            d = st[h]
            k = k_ref[ts, h * dk:(h + 1) * dk].astype(F32)
            d["kb"] = k.astype(BF16)
            d["kwT"] = (k.T * d["w_row"]).astype(BF16)
            d["q32"] = q_ref[ts, h * dk:(h + 1) * dk].astype(F32) * scale
            d["qs"] = d["q32"].astype(BF16)
        for h in hs:
            d = st[h]
            d["c_loc"] = _dot(d["kwT"], d["v_aug"])
            d["sqk"] = _dot_nt(d["qs"], d["kb"])
            d["c_in"] = c_ref[h]
        for h in hs:
            d = st[h]
            dlog = jnp.where(mask, d["cum_d"] - d["base_row"], -jnp.inf)
            inter = d["cum_d"] + d["m_in"]
            d["m_t"] = jnp.maximum(d["rmax_d"], inter)
            d["pm"] = (jnp.exp(dlog - d["m_t"]) * d["sqk"]).astype(BF16)
            d["qa"] = (d["q32"] * jnp.exp(inter - d["m_t"])).astype(BF16)
        for h in hs:
            d = st[h]
            lhs = jnp.concatenate([d["pm"], d["qa"]], axis=1)
            rhs = jnp.concatenate([d["v_aug"], d["c_in"].astype(BF16)], axis=0)
            num = _dot(lhs, rhs)
            den = num[:, dv:dv + 128]
            inv = 1.0 / jnp.maximum(jnp.abs(den), jnp.exp(-d["m_t"]))
            d["hout"] = num[:, 0:dv] * jnp.concatenate([inv] * (dv // 128), axis=1)
        for h in hs:
            d = st[h]
            a_prev = jnp.exp(d["tot"] + d["m_in"] - d["m_new"])
            a_prev3 = jnp.concatenate([a_prev] * (dv // 128 + 1), axis=1)
            c_ref[h] = a_prev3 * d["c_in"] + d["c_loc"]
            m_ref[h] = jnp.broadcast_to(d["m_new"], (8, 128))
        for h in hs:
            hout = st[h]["hout"]
            vs = slice(h * dv, (h + 1) * dv)
            if backward:
                hout = hout + hf_ref[ts, vs]
                hout = hout * lax.rsqrt(jnp.mean(hout * hout, axis=-1, keepdims=True) + EPS)
                hout = hout * nw_ref[:, vs]
                o_ref[ts, vs] = (_sigmoid(og_ref[ts, vs].astype(F32)) * hout).astype(o_ref.dtype)
            else:
                o_ref[ts, vs] = hout


def mlstm_scan(proj, ml, mlT, batch, seq_len, cols, *, backward, h_fwd=None, norm_w=None):
    m = proj.shape[0]
    cb = CHUNKS_PER_STEP
    tb = CHUNK * cb
    assert seq_len % tb == 0 and m == batch * seq_len
    nc = seq_len // tb
    H = MLSTM_HEADS
    dk, dv = cols["dk"], cols["dv"]
    qw, vw = H * dk, H * dv
    qb, kb, vb, ob = cols["q"] // qw, cols["k"] // qw, cols["v"] // vw, cols["o"] // vw
    assert qb * qw == cols["q"] and kb * qw == cols["k"] and vb * vw == cols["v"] and ob * vw == cols["o"]

    def cg(b, c):
        return b * nc + ((nc - 1 - c) if backward else c)

    in_specs = [
        pl.BlockSpec((tb, qw), lambda b, c: (cg(b, c), qb)),
        pl.BlockSpec((tb, qw), lambda b, c: (cg(b, c), kb)),
        pl.BlockSpec((tb, vw), lambda b, c: (cg(b, c), vb)),
        pl.BlockSpec((tb, 512), lambda b, c: (cg(b, c), 0)),
        pl.BlockSpec((384, tb), lambda b, c: (0, cg(b, c))),
    ]
    args = [proj, proj, proj, ml, mlT]
    if backward:
        in_specs += [
            pl.BlockSpec((tb, vw), lambda b, c: (cg(b, c), 0)),
            pl.BlockSpec((tb, vw), lambda b, c: (cg(b, c), ob)),
            pl.BlockSpec((1, vw), lambda b, c: (0, 0)),
        ]
        args += [h_fwd, proj, norm_w]
        out_dtype = BF16
    else:
        out_dtype = F32
    return pl.pallas_call(
        functools.partial(_mlstm_kernel, backward=backward, dk=dk, dv=dv,
                          heads_per_group=8 if backward else 4, cb=cb),
        grid=(batch, nc),
        in_specs=in_specs,
        out_specs=pl.BlockSpec((tb, vw), lambda b, c: (cg(b, c), 0)),
        out_shape=jax.ShapeDtypeStruct((m, vw), out_dtype),
        scratch_shapes=[pltpu.VMEM((H, dk, dv + 128), F32), pltpu.VMEM((H, 8, 128), F32)],
        compiler_params=_cparams(("parallel", "arbitrary")),
        name="mlstm_bwd" if backward else "mlstm_fwd",
    )(*args)


def _outproj_kernel(a1_ref, a2_ref, w1_ref, w2_ref, x_ref, nw_ref, o_ref, xw_ref, ssq_ref):
    acc = _dot(a1_ref[...], w1_ref[...]) + _dot(a2_ref[...], w2_ref[...])
    x1 = x_ref[...] + acc
    o_ref[...] = x1
    xw_ref[...] = (x1 * nw_ref[...]).astype(xw_ref.dtype)
    ssq_ref[...] = jnp.broadcast_to(jnp.sum(x1 * x1, axis=-1, keepdims=True), ssq_ref.shape)


def outproj_residual(a1, a2, w1, w2, x, norm_w, bm=512, bn=512):
    m, k = a1.shape
    n = w1.shape[1]
    return pl.pallas_call(
        _outproj_kernel,
        grid=(n // bn, m // bm),
        in_specs=[pl.BlockSpec((bm, k), lambda j, i: (i, 0)),
                  pl.BlockSpec((bm, k), lambda j, i: (i, 0)),
                  pl.BlockSpec((k, bn), lambda j, i: (0, j)),
                  pl.BlockSpec((k, bn), lambda j, i: (0, j)),
                  pl.BlockSpec((bm, bn), lambda j, i: (i, j)),
                  pl.BlockSpec((1, bn), lambda j, i: (0, j))],
        out_specs=[pl.BlockSpec((bm, bn), lambda j, i: (i, j)),
                   pl.BlockSpec((bm, bn), lambda j, i: (i, j)),
                   pl.BlockSpec((bm, 128), lambda j, i: (i, j))],
        out_shape=[jax.ShapeDtypeStruct((m, n), F32), jax.ShapeDtypeStruct((m, n), BF16),
                   jax.ShapeDtypeStruct((m, (n // bn) * 128), F32)],
        compiler_params=_cparams(("parallel", "parallel")),
        name="outproj_residual",
    )(a1, a2, w1, w2, x, norm_w.reshape(1, n).astype(F32))


def _prep_layer(norm1_w, w_in, conv_w, conv_b, dt_bias, a_log, d_skip, ssd_norm_w, b_i, b_f,
                mlstm_norm_w, w_out, norm2_w, w_up, w_down):
    d_model = w_in.shape[0]
    d_mix = w_out.shape[0]
    d_ssd = d_mix // 2
    d_ml = d_mix - d_ssd
    n_ssd_heads = d_ssd // SSD_HEAD_DIM
    xbc_w = d_ssd + 2 * SSD_GROUPS * SSD_STATE
    dv = d_ml // MLSTM_HEADS
    dk = dv // 2
    widths = (d_ssd, xbc_w, 2 * n_ssd_heads, MLSTM_HEADS * dk, MLSTM_HEADS * dk, d_ml, d_ml,
              2 * MLSTM_HEADS, 2 * MLSTM_HEADS)
    offs = [0]
    for wd in widths:
        offs.append(offs[-1] + wd)
    assert offs[-1] == w_in.shape[1]
    assert 2 * n_ssd_heads == 128 and 2 * MLSTM_HEADS == 32
    seg = lambda i: w_in[:, offs[i]:offs[i + 1]]
    w_bf = w_in.astype(BF16)
    w_gate = jnp.concatenate([seg(2), seg(7), seg(8), jnp.zeros((d_model, 64), w_in.dtype)],
                             axis=1).astype(BF16)
    gate_bias = jnp.concatenate([dt_bias.reshape(-1), b_i.reshape(-1), b_f.reshape(-1),
                                 jnp.zeros((64,), F32)]).astype(F32).reshape(1, 256)
    cols = {"q": 0, "k": MLSTM_HEADS * dk, "v": 2 * MLSTM_HEADS * dk}
    cols["o"] = cols["v"] + d_ml
    cols["dk"], cols["dv"], cols["d_ssd"] = dk, dv, d_ssd
    return dict(
        norm1_w=norm1_w, w_bf=w_bf, offs=offs, w_gate=w_gate, gate_bias=gate_bias,
        alog=a_log.reshape(1, 128).astype(F32),
        conv_w=conv_w.astype(F32), conv_b=conv_b.reshape(1, -1).astype(F32),
        dexp=jnp.broadcast_to(jnp.repeat(d_skip.astype(F32), SSD_HEAD_DIM)[:, None], (d_ssd, 128)),
        ssd_norm_w=ssd_norm_w.reshape(1, -1).astype(F32),
        mlstm_norm_w=mlstm_norm_w.reshape(1, -1).astype(F32),
        w_out1=w_out[:d_ssd].astype(BF16), w_out2=w_out[d_ssd:].astype(BF16),
        norm2_w=norm2_w, w_up=w_up.astype(BF16), w_down=w_down.astype(BF16), cols=cols)


def _layer(x, p, batch, seq_len):
    cols = p["cols"]
    d_ssd = cols["d_ssd"]
    h, ssd, ssdT, ml, mlT = gates(x, p["norm1_w"], p["w_gate"], p["gate_bias"], p["alog"])
    offs = p["offs"]
    proj_z = matmul(h, p["w_bf"], F32, col0=offs[0], n=d_ssd)
    proj_b = matmul(h, p["w_bf"], BF16, col0=offs[3], n=offs[7] - offs[3])
    xsT = matmul_conv(h, p["w_bf"], offs[1], d_ssd, p["conv_w"][:, :d_ssd], p["conv_b"][:, :d_ssd],
                      seq_len, True)
    bc = matmul_conv(h, p["w_bf"], offs[1] + d_ssd, 2 * SSD_GROUPS * SSD_STATE,
                     p["conv_w"][:, d_ssd:], p["conv_b"][:, d_ssd:], seq_len, False)
    y_f = ssd_scan(xsT, bc, ssd, ssdT, batch, seq_len, backward=False, dexp=p["dexp"])
    mix1 = ssd_scan(xsT, bc, ssd, ssdT, batch, seq_len, backward=True, y_fwd=y_f,
                    proj=proj_z, norm_w=p["ssd_norm_w"])
    h_f = mlstm_scan(proj_b, ml, mlT, batch, seq_len, cols, backward=False)
    mix2 = mlstm_scan(proj_b, ml, mlT, batch, seq_len, cols, backward=True, h_fwd=h_f,
                      norm_w=p["mlstm_norm_w"])
    x1, x1w, ssq = outproj_residual(mix1, mix2, p["w_out1"], p["w_out2"], x, p["norm2_w"])
    u = matmul_relu2_pipelined(x1w, p["w_up"], ssq, BF16)
    return matmul_ksplit_residual(u, p["w_down"], x1)


def _trunk(x, layers, final_norm_w):
    batch, seq_len, d = x.shape
    xf = x.reshape(batch * seq_len, d)
    for p in layers:
        xf = _layer(xf, p, batch, seq_len)
    return rmsnorm_rows(xf, final_norm_w, F32).reshape(batch, seq_len, d)


def kernel(x_prompt, x_sample, norm1_w, w_in, conv_w, conv_b, dt_bias, a_log, d_skip, ssd_norm_w,
           b_i, b_f, mlstm_norm_w, w_out, norm2_w, w_up, w_down, final_norm_w):
    depth = w_in.shape[0]
    layers = [_prep_layer(norm1_w[l], w_in[l], conv_w[l], conv_b[l], dt_bias[l], a_log[l], d_skip[l],
                          ssd_norm_w[l], b_i[l], b_f[l], mlstm_norm_w[l], w_out[l], norm2_w[l],
                          w_up[l], w_down[l]) for l in range(depth)]
    y_prompt = _trunk(x_prompt, layers, final_norm_w)
    y_sample = _trunk(x_sample, layers, final_norm_w)
    return (y_prompt, y_sample)
```
